```python
import jax
import jax.numpy as jnp
from jax import lax
import numpy as np


D_MODEL = 1024
BATCH = 16
SEQ = 4096
DEPTH = 1

ATT_HEADS = 8
ATT_KV_HEADS = 2
ATT_HEAD_DIM = 64
ATT_GROUP = ATT_HEADS // ATT_KV_HEADS
ATT_WIDTH = ATT_HEADS * ATT_HEAD_DIM
ATT_KV_WIDTH = ATT_KV_HEADS * ATT_HEAD_DIM
WINDOW = 128
ATT_BLOCK = 128
DN_HEADS = 4
DN_HEAD_DIM = 128
DN_WIDTH = DN_HEADS * DN_HEAD_DIM
CONV_WIDTH = 4
DN_CHUNK = 64
D_MIX = ATT_WIDTH + DN_WIDTH
IN_SPLITS = (ATT_WIDTH, ATT_KV_WIDTH, ATT_KV_WIDTH, 3 * DN_WIDTH, DN_HEADS, DN_HEADS, DN_WIDTH)
D_IN_PROJ = ATT_WIDTH + 2 * ATT_KV_WIDTH + 4 * DN_WIDTH + 2 * DN_HEADS
PEER_HEADS = 8
N_KEYS = 128
N_EXPERTS = N_KEYS * N_KEYS
PEER_QDIM = 128
PEER_HALF = PEER_QDIM // 2
PEER_TOPK = 16
PEER_TOKEN_BLOCK = 128
EPS = 1e-6

kernel_name = "hybrid_swa_sink_gdn_peer_block"


def rms_norm(x, g):
    xf = x.astype(jnp.float32)
    y = xf * lax.rsqrt(jnp.mean(xf * xf, axis=-1, keepdims=True) + EPS)
    return (y * g.astype(jnp.float32)).astype(x.dtype)


def l2norm(x):
    return x * lax.rsqrt(jnp.sum(x * x, axis=-1, keepdims=True) + EPS)


def sliding_window_sink_attention(q, k, v, sinks):
    B, S = q.shape[0], q.shape[1]
    nb = S // ATT_BLOCK
    f32 = jnp.float32
    qb = q.reshape(B, nb, ATT_BLOCK, ATT_KV_HEADS, ATT_GROUP, ATT_HEAD_DIM).astype(f32)

    def band(t):
        tb = t.reshape(B, nb, ATT_BLOCK, ATT_KV_HEADS, ATT_HEAD_DIM).astype(f32)
        prev = jnp.pad(tb[:, :-1], ((0, 0), (1, 0), (0, 0), (0, 0), (0, 0)))
        return jnp.concatenate([prev, tb], axis=2)

    kb, vb = band(k), band(v)
    s = jnp.einsum('bnqhgd,bnkhd->bnhgqk', qb, kb) * (ATT_HEAD_DIM ** -0.5)
    qi = jnp.arange(ATT_BLOCK)[:, None]
    kj = jnp.arange(2 * ATT_BLOCK)[None, :]
    rel = qi + ATT_BLOCK - kj
    in_window = (rel >= 0) & (rel < WINDOW)
    key_exists = (jnp.arange(nb)[:, None, None] > 0) | (kj[None] >= ATT_BLOCK)
    mask = in_window[None] & key_exists
    s = jnp.where(mask[None, :, None, None], s, -jnp.inf)
    sink = sinks.astype(f32).reshape(ATT_KV_HEADS, ATT_GROUP)[None, None, :, :, None, None]
    m = jnp.maximum(jnp.max(s, axis=-1, keepdims=True), sink)
    p = jnp.exp(s - m)
    denom = jnp.sum(p, axis=-1, keepdims=True) + jnp.exp(sink - m)
    o = jnp.einsum('bnhgqk,bnkhd->bnqhgd', p / denom, vb)
    return o.reshape(B, S, ATT_WIDTH)


def causal_conv_silu(x, w):
    S = x.shape[1]
    xp = jnp.pad(x, ((0, 0), (CONV_WIDTH - 1, 0), (0, 0)))
    y = sum(xp[:, j:j + S] * w[j] for j in range(CONV_WIDTH))
    return jax.nn.silu(y)


def gated_delta_rule(q, k, v, g, beta):
    B, S, H, D = q.shape
    C = DN_CHUNK
    nc = S // C
    f32 = jnp.float32

    def chunks(t):
        return t.reshape(B, nc, C, H, D).transpose(0, 3, 1, 2, 4)

    q, k, v = chunks(q), chunks(k), chunks(v)
    g = g.reshape(B, nc, C, H).transpose(0, 3, 1, 2)
    beta = beta.reshape(B, nc, C, H).transpose(0, 3, 1, 2)
    gc = jnp.cumsum(g, axis=-1)
    causal = jnp.tril(jnp.ones((C, C), dtype=bool))
    strict = jnp.tril(jnp.ones((C, C), dtype=bool), k=-1)
    diff = gc[..., :, None] - gc[..., None, :]
    decay_mat = jnp.where(causal, jnp.exp(jnp.where(causal, diff, 0.0)), 0.0)
    kb = k * beta[..., None]
    m_low = jnp.where(strict, jnp.einsum('bhncd,bhnjd->bhncj', kb, k) * decay_mat, 0.0)
    a_mat = m_low + jnp.eye(C, dtype=f32)
    rhs = jnp.concatenate([v * beta[..., None], kb * jnp.exp(gc)[..., None]], axis=-1)
    sol = lax.linalg.triangular_solve(a_mat, rhs, left_side=True, lower=True, unit_diagonal=True)
    u, w = sol[..., :D], sol[..., D:]
    qk = jnp.where(causal, jnp.einsum('bhncd,bhnjd->bhncj', q, k) * decay_mat, 0.0)
    q_dec = q * jnp.exp(gc)[..., None]
    k_dec = k * jnp.exp(gc[..., -1:] - gc)[..., None]
    chunk_decay = jnp.exp(gc[..., -1])

    def step(state, inp):
        u_c, w_c, qk_c, qd_c, kd_c, cd_c = inp
        v_new = u_c - jnp.einsum('bhcd,bhde->bhce', w_c, state)
        o_c = jnp.einsum('bhcd,bhde->bhce', qd_c, state) + jnp.einsum('bhcj,bhje->bhce', qk_c, v_new)
        state = state * cd_c[..., None, None] + jnp.einsum('bhcd,bhce->bhde', kd_c, v_new)
        return state, o_c

    xs = tuple(jnp.moveaxis(t, 2, 0) for t in (u, w, qk, q_dec, k_dec, chunk_decay))
    init = jnp.zeros((B, H, D, D), dtype=f32)
    _, o = lax.scan(step, init, xs)
    return o.transpose(1, 0, 3, 2, 4).reshape(B, S, H, D)


def peer(xn, w_q, sub_keys, u_tab, v_tab):
    B, S, D = xn.shape
    T = B * S
    xt = xn.reshape(T, D)
    q = (xt @ w_q).reshape(T, PEER_HEADS, 2, PEER_HALF)
    s = jnp.einsum('thpc,hpnc->thpn', q, sub_keys)
    s_top, i_top = lax.top_k(s, PEER_TOPK)
    cand_s = (s_top[:, :, 0, :, None] + s_top[:, :, 1, None, :]).reshape(T, PEER_HEADS, PEER_TOPK * PEER_TOPK)
    cand_i = (i_top[:, :, 0, :, None] * N_KEYS + i_top[:, :, 1, None, :]).reshape(T, PEER_HEADS, PEER_TOPK * PEER_TOPK)
    best_s, pos = lax.top_k(cand_s, PEER_TOPK)
    idx = jnp.take_along_axis(cand_i, pos, axis=-1)
    gate = jax.nn.softmax(best_s.astype(jnp.float32), axis=-1).astype(xn.dtype)
    nblk = T // PEER_TOKEN_BLOCK
    hk = PEER_HEADS * PEER_TOPK

    def block(args):
        xb, ib, gb = args
        act = jax.nn.gelu(jnp.einsum('tkd,td->tk', u_tab[ib], xb), approximate=False)
        return jnp.einsum('tk,tkd->td', act * gb, v_tab[ib])

    out = lax.map(block, (xt.reshape(nblk, PEER_TOKEN_BLOCK, D),
                          idx.reshape(nblk, PEER_TOKEN_BLOCK, hk),
                          gate.reshape(nblk, PEER_TOKEN_BLOCK, hk)))
    return out.reshape(B, S, D)


def hybrid_layer(h, norm_mix_g, w_in, att_q_norm_g, att_k_norm_g, att_sinks, att_out_norm_g,
                 dn_conv_w, dn_a_log, dn_dt_bias, dn_out_norm_g, w_out, norm_ffn_g,
                 peer_w_q, peer_sub_keys, peer_u, peer_v):
    B, S, _ = h.shape
    f32 = jnp.float32
    xn = rms_norm(h, norm_mix_g)
    proj = xn @ w_in
    offs = []
    acc = 0
    for width in IN_SPLITS[:-1]:
        acc += width
        offs.append(acc)
    q_a, k_a, v_a, qkv_d, a_d, b_d, z_d = jnp.split(proj, offs, axis=-1)

    q_a = rms_norm(q_a.reshape(B, S, ATT_HEADS, ATT_HEAD_DIM), att_q_norm_g)
    k_a = rms_norm(k_a.reshape(B, S, ATT_KV_HEADS, ATT_HEAD_DIM), att_k_norm_g)
    v_a = v_a.reshape(B, S, ATT_KV_HEADS, ATT_HEAD_DIM)
    att = sliding_window_sink_attention(q_a, k_a, v_a, att_sinks).astype(h.dtype)
    att = rms_norm(att, att_out_norm_g)

    qkv = causal_conv_silu(qkv_d.astype(f32), dn_conv_w.astype(f32))
    q_d, k_d, v_d = jnp.split(qkv, 3, axis=-1)
    shp = (B, S, DN_HEADS, DN_HEAD_DIM)
    q_d = l2norm(q_d.reshape(shp)) * (DN_HEAD_DIM ** -0.5)
    k_d = l2norm(k_d.reshape(shp))
    v_d = v_d.reshape(shp)
    g = -jnp.exp(dn_a_log.astype(f32)) * jax.nn.softplus(a_d.astype(f32) + dn_dt_bias.astype(f32))
    beta = jax.nn.sigmoid(b_d.astype(f32))
    o = gated_delta_rule(q_d, k_d, v_d, g, beta)
    o = rms_norm(o, dn_out_norm_g) * jax.nn.silu(z_d.astype(f32).reshape(shp))
    dn = o.reshape(B, S, DN_WIDTH).astype(h.dtype)

    h = h + jnp.concatenate([att, dn], axis=-1) @ w_out
    h = h + peer(rms_norm(h, norm_ffn_g), peer_w_q, peer_sub_keys, peer_u, peer_v)
    return h


def setup_inputs(seed: int = 0) -> dict:
    key = jax.random.key(seed)
    ks = jax.random.split(key, 20)
    L = DEPTH
    nrm = jax.random.normal
    dt = jnp.exp(jax.random.uniform(ks[9], (L, DN_HEADS), minval=jnp.log(1e-3), maxval=jnp.log(1e-1)))
    return {
        "x": nrm(ks[0], (BATCH, SEQ, D_MODEL), jnp.float32),
        "norm_mix_g": 1.0 + 0.02 * nrm(ks[1], (L, D_MODEL), jnp.float32),
        "w_in": nrm(ks[2], (L, D_MODEL, D_IN_PROJ), jnp.float32) * D_MODEL ** -0.5,
        "att_q_norm_g": 1.0 + 0.02 * nrm(ks[3], (L, ATT_HEAD_DIM), jnp.float32),
        "att_k_norm_g": 1.0 + 0.02 * nrm(ks[4], (L, ATT_HEAD_DIM), jnp.float32),
        "att_sinks": nrm(ks[5], (L, ATT_HEADS), jnp.float32),
        "att_out_norm_g": 1.0 + 0.02 * nrm(ks[6], (L, ATT_WIDTH), jnp.float32),
        "dn_conv_w": nrm(ks[7], (L, CONV_WIDTH, 3 * DN_WIDTH), jnp.float32) * CONV_WIDTH ** -0.5,
        "dn_a_log": jnp.log(jax.random.uniform(ks[8], (L, DN_HEADS), minval=1.0, maxval=16.0)),
        "dn_dt_bias": dt + jnp.log(-jnp.expm1(-dt)),
        "dn_out_norm_g": 1.0 + 0.02 * nrm(ks[10], (L, DN_HEAD_DIM), jnp.float32),
        "w_out": nrm(ks[11], (L, D_MIX, D_MODEL), jnp.float32) * D_MIX ** -0.5,
        "norm_ffn_g": 1.0 + 0.02 * nrm(ks[12], (L, D_MODEL), jnp.float32),
        "peer_w_q": nrm(ks[13], (L, D_MODEL, PEER_HEADS * PEER_QDIM), jnp.float32) * D_MODEL ** -0.5,
        "peer_sub_keys": nrm(ks[14], (L, PEER_HEADS, 2, N_KEYS, PEER_HALF), jnp.float32) * PEER_HALF ** -0.5,
        "peer_u": nrm(ks[15], (L, N_EXPERTS, D_MODEL), jnp.float32) * D_MODEL ** -0.5,
        "peer_v": nrm(ks[16], (L, N_EXPERTS, D_MODEL), jnp.float32) * D_MODEL ** -0.5,
    }


def reference(x, norm_mix_g, w_in, att_q_norm_g, att_k_norm_g, att_sinks, att_out_norm_g,
              dn_conv_w, dn_a_log, dn_dt_bias, dn_out_norm_g, w_out, norm_ffn_g,
              peer_w_q, peer_sub_keys, peer_u, peer_v):
    h = x
    for l in range(DEPTH):
        h = hybrid_layer(h, norm_mix_g[l], w_in[l], att_q_norm_g[l], att_k_norm_g[l], att_sinks[l],
                         att_out_norm_g[l], dn_conv_w[l], dn_a_log[l], dn_dt_bias[l], dn_out_norm_g[l],
                         w_out[l], norm_ffn_g[l], peer_w_q[l], peer_sub_keys[l], peer_u[l], peer_v[l])
    return h
```

```python
import functools

import numpy as np
import jax
import jax.numpy as jnp
from jax import lax
from jax.experimental import pallas as pl
from jax.experimental.pallas import tpu as pltpu

F32 = jnp.float32
BF16 = jnp.bfloat16
U32 = jnp.uint32

D_MODEL = 1024
ATT_HEADS = 8
ATT_KV_HEADS = 2
ATT_HEAD_DIM = 64
ATT_WIDTH = 512
ATT_KV_WIDTH = 128
ATT_BLOCK = 128
DN_HEADS = 4
DN_HEAD_DIM = 128
DN_WIDTH = 512
CONV_WIDTH = 4
PEER_HEADS = 8
N_KEYS = 128
N_EXPERTS = N_KEYS * N_KEYS
PEER_HALF = 64
PEER_TOPK = 16
EPS = 1e-6

LANES = 128
SUBLANES = 8
VMEM_LIMIT_BYTES = 56 * 1024 * 1024

_C_ATT = 0
_C_DN = 768
_C_Z = _C_DN + 3 * DN_WIDTH
_C_AB = _C_Z + DN_WIDTH
_C_END = _C_AB + 2 * LANES

GDN_CHUNK = 128
_ROWS16 = 2 * SUBLANES


def _dot(a, b):
    return jnp.dot(a, b, preferred_element_type=F32)


def _dot_nt(a, b):
    return lax.dot_general(a, b, (((1,), (1,)), ((), ())), preferred_element_type=F32)


def _split2(a):
    hi = a.astype(BF16)
    lo = (a - hi.astype(F32)).astype(BF16)
    return hi, lo


def _dot3(a, b):
    ah, al = _split2(a)
    bh, bl = _split2(b)
    return _dot(ah, bh) + (_dot(ah, bl) + _dot(al, bh))


def _inproj_kernel(x_ref, g_ref, w_ref, att_ref, dn_ref, z_ref, ab_ref):
    x = x_ref[...]
    ms = jnp.mean(x * x, axis=-1, keepdims=True)
    xn = (x * lax.rsqrt(ms + EPS) * g_ref[...]).astype(BF16)
    att_ref[...] = _dot(xn, w_ref[:, _C_ATT:_C_DN])
    dn_ref[...] = _dot(xn, w_ref[:, _C_DN:_C_Z])
    z_ref[...] = _dot(xn, w_ref[:, _C_Z:_C_AB])
    ab_ref[...] = _dot(xn, w_ref[:, _C_AB:_C_END])


def _inproj(x2, g, w_all, tm):
    T = x2.shape[0]
    return pl.pallas_call(
        _inproj_kernel,
        grid=(T // tm,),
        in_specs=[pl.BlockSpec((tm, D_MODEL), lambda i: (i, 0)),
                  pl.BlockSpec((1, D_MODEL), lambda i: (0, 0)),
                  pl.BlockSpec((D_MODEL, _C_END), lambda i: (0, 0))],
        out_specs=[pl.BlockSpec((tm, 768), lambda i: (i, 0)),
                   pl.BlockSpec((tm, 3 * DN_WIDTH), lambda i: (i, 0)),
                   pl.BlockSpec((tm, DN_WIDTH), lambda i: (i, 0)),
                   pl.BlockSpec((tm, 2 * LANES), lambda i: (i, 0))],
        out_shape=[jax.ShapeDtypeStruct((T, 768), F32),
                   jax.ShapeDtypeStruct((T, 3 * DN_WIDTH), F32),
                   jax.ShapeDtypeStruct((T, DN_WIDTH), F32),
                   jax.ShapeDtypeStruct((T, 2 * LANES), F32)],
        compiler_params=pltpu.CompilerParams(dimension_semantics=("arbitrary",),
                                             vmem_limit_bytes=VMEM_LIMIT_BYTES),
        name="inproj",
    )(x2, g, w_all)


def _head_rms(t, bd, g):
    hi, lo = _split2(t * t)
    ss = _dot(hi, bd) + _dot(lo, bd)
    return t * lax.rsqrt(ss * (1.0 / ATT_HEAD_DIM) + EPS) * g


def _attn_kernel(sink_ref, cur_ref, prev_ref, gq_ref, gk_ref, go_ref, bdq_ref, bdk_ref, o_ref,
                 *, tiles_per_seq):
    i = pl.program_id(0)
    tq = cur_ref.shape[0]
    nblk = tq // ATT_BLOCK
    first = (i % tiles_per_seq) == 0

    q = cur_ref[:, 0:ATT_WIDTH]
    k = cur_ref[:, ATT_WIDTH:ATT_WIDTH + ATT_KV_WIDTH]
    v = cur_ref[:, ATT_WIDTH + ATT_KV_WIDTH:ATT_WIDTH + 2 * ATT_KV_WIDTH]
    kp = prev_ref[:, 0:ATT_KV_WIDTH]
    vp = prev_ref[:, ATT_KV_WIDTH:2 * ATT_KV_WIDTH]

    qn = (_head_rms(q, bdq_ref[...], gq_ref[...]) * (ATT_HEAD_DIM ** -0.5)).astype(BF16)
    kn = jnp.concatenate([_head_rms(kp, bdk_ref[...], gk_ref[...]),
                          _head_rms(k, bdk_ref[...], gk_ref[...])], axis=0)
    vf = jnp.concatenate([vp, v], axis=0)

    lane = lax.broadcasted_iota(jnp.int32, kn.shape, 1)
    left = lane < ATT_HEAD_DIM
    zero = jnp.zeros_like(kn)
    k0l = jnp.where(left, kn, zero)
    k1r = jnp.where(left, zero, kn)
    v0l = jnp.where(left, vf, zero)
    v1r = jnp.where(left, zero, vf)
    kl = (k0l.astype(BF16), pltpu.roll(k1r, ATT_HEAD_DIM, 1).astype(BF16))
    kr = (pltpu.roll(k0l, ATT_HEAD_DIM, 1).astype(BF16), k1r.astype(BF16))
    vl = (v0l.astype(BF16), pltpu.roll(v1r, ATT_HEAD_DIM, 1).astype(BF16))
    vr = (pltpu.roll(v0l, ATT_HEAD_DIM, 1).astype(BF16), v1r.astype(BF16))

    qi = lax.broadcasted_iota(jnp.int32, (ATT_BLOCK, 2 * ATT_BLOCK), 0)
    kj = lax.broadcasted_iota(jnp.int32, (ATT_BLOCK, 2 * ATT_BLOCK), 1)
    rel = qi + ATT_BLOCK - kj
    in_window = (rel >= 0) & (rel < ATT_BLOCK)
    first_key = jnp.where(first, ATT_BLOCK, 0)
    neg_inf = jnp.full((ATT_BLOCK, 2 * ATT_BLOCK), -jnp.inf, F32)

    def softmax_rows(s, sink):
        m = jnp.maximum(jnp.max(s, axis=-1, keepdims=True), sink)
        p = jnp.exp(s - m)
        den = jnp.sum(p, axis=-1, keepdims=True) + jnp.exp(sink - m)
        return (p * (1.0 / den)).astype(BF16)

    for j in range(nblk):
        rows = slice(j * ATT_BLOCK, (j + 1) * ATT_BLOCK)
        krows = slice(j * ATT_BLOCK, (j + 2) * ATT_BLOCK)
        if j == 0:
            mask = in_window & (kj >= first_key)
        else:
            mask = in_window
        pairs = []
        for c in range(ATT_KV_HEADS):
            qe = jnp.concatenate([qn[rows, (2 * c) * LANES:(2 * c + 1) * LANES],
                                  qn[rows, (2 * c + 1) * LANES:(2 * c + 2) * LANES]], axis=0)
            s_even = _dot_nt(qe, kl[c][krows])
            s_odd = _dot_nt(qe, kr[c][krows])
            for half in range(2):
                hr = slice(half * ATT_BLOCK, (half + 1) * ATT_BLOCK)
                h_even = 4 * c + 2 * half
                p_e = softmax_rows(jnp.where(mask, s_even[hr], neg_inf), sink_ref[h_even])
                p_o = softmax_rows(jnp.where(mask, s_odd[hr], neg_inf), sink_ref[h_even + 1])
                pairs.append(_dot(p_e, vl[c][krows]) + _dot(p_o, vr[c][krows]))
        att = jnp.concatenate(pairs, axis=1)
        ms = jnp.mean(att * att, axis=-1, keepdims=True)
        o_ref[rows, :] = (att * lax.rsqrt(ms + EPS) * go_ref[...]).astype(BF16)


def _attention(att_in, sinks, gq, gk, go, bdq, bdk, seq, tq):
    T = att_in.shape[0]
    tiles_per_seq = seq // tq
    blk_per_tile = tq // ATT_BLOCK
    kv_col_blk = ATT_WIDTH // (2 * ATT_KV_WIDTH)
    return pl.pallas_call(
        functools.partial(_attn_kernel, tiles_per_seq=tiles_per_seq),
        grid_spec=pltpu.PrefetchScalarGridSpec(
            num_scalar_prefetch=1,
            grid=(T // tq,),
            in_specs=[pl.BlockSpec((tq, 768), lambda i, s: (i, 0)),
                      pl.BlockSpec((ATT_BLOCK, 2 * ATT_KV_WIDTH),
                                   lambda i, s: (jnp.maximum(i * blk_per_tile - 1, 0), kv_col_blk)),
                      pl.BlockSpec((1, ATT_WIDTH), lambda i, s: (0, 0)),
                      pl.BlockSpec((1, ATT_KV_WIDTH), lambda i, s: (0, 0)),
                      pl.BlockSpec((1, ATT_WIDTH), lambda i, s: (0, 0)),
                      pl.BlockSpec((ATT_WIDTH, ATT_WIDTH), lambda i, s: (0, 0)),
                      pl.BlockSpec((ATT_KV_WIDTH, ATT_KV_WIDTH), lambda i, s: (0, 0))],
            out_specs=pl.BlockSpec((tq, ATT_WIDTH), lambda i, s: (i, 0)),
        ),
        out_shape=jax.ShapeDtypeStruct((T, ATT_WIDTH), BF16),
        compiler_params=pltpu.CompilerParams(dimension_semantics=("arbitrary",),
                                             vmem_limit_bytes=VMEM_LIMIT_BYTES),
        name="swa_attention",
    )(sinks, att_in, att_in, gq, gk, go, bdq, bdk)


def _unit_lower_inverse(m_low, row, col):
    n = m_low.shape[0]
    zero = jnp.zeros_like(m_low)
    eye = jnp.where(row == col, jnp.ones_like(m_low), zero)
    same = (row >> 3) == (col >> 3)
    nm = jnp.where(same, -m_low, zero)
    p = eye + nm
    n2 = _dot3(nm, nm)
    p = p + _dot3(p, n2)
    n4 = _dot3(n2, n2)
    p = p + _dot3(p, n4)
    shift = 3
    while (1 << shift) < n:
        same_next = (row >> (shift + 1)) == (col >> (shift + 1))
        off = jnp.where(same_next & jnp.logical_not(same), m_low, zero)
        p = p - _dot3(_dot3(p, off), p)
        same = same_next
        shift += 1
    return p


def _gdn_kernel(dn_ref, ab_ref, z_ref, cw_ref, nega_ref, dtb_ref, gn_ref, tri_ref, o_ref,
                state_ref, carry_ref):
    t = pl.program_id(1)
    C = GDN_CHUNK

    @pl.when(t == 0)
    def _():
        state_ref[...] = jnp.zeros_like(state_ref)
        carry_ref[...] = jnp.zeros_like(carry_ref)

    x = dn_ref[...]
    xs = jnp.concatenate([carry_ref[...], x], axis=0)
    carry_ref[...] = x[C - SUBLANES:C, :]
    cw = cw_ref[...]
    y = xs[5:5 + C] * cw[0:1, :]
    for jw in range(1, CONV_WIDTH):
        y = y + xs[5 + jw:5 + jw + C] * cw[jw:jw + 1, :]
    y = y * (1.0 / (1.0 + jnp.exp(-y)))

    a = ab_ref[:, 0:LANES]
    b = ab_ref[:, LANES:2 * LANES]
    sp_in = a + dtb_ref[...]
    softplus = jnp.maximum(sp_in, 0.0) + jnp.log1p(jnp.exp(-jnp.abs(sp_in)))
    g_all = nega_ref[...] * softplus
    beta_all = 1.0 / (1.0 + jnp.exp(-b))
    gc_all = _dot3(tri_ref[...], g_all)
    gc_t = gc_all.T

    row = lax.broadcasted_iota(jnp.int32, (C, C), 0)
    col = lax.broadcasted_iota(jnp.int32, (C, C), 1)
    causal = row >= col
    strict = row > col
    zero_cc = jnp.zeros((C, C), F32)

    z = z_ref[...]
    for h in range(DN_HEADS):
        hs = slice(h * DN_HEAD_DIM, (h + 1) * DN_HEAD_DIM)
        qh = y[:, h * DN_HEAD_DIM:(h + 1) * DN_HEAD_DIM]
        kh = y[:, DN_WIDTH + h * DN_HEAD_DIM:DN_WIDTH + (h + 1) * DN_HEAD_DIM]
        vh = y[:, 2 * DN_WIDTH + h * DN_HEAD_DIM:2 * DN_WIDTH + (h + 1) * DN_HEAD_DIM]
        qh = qh * lax.rsqrt(jnp.sum(qh * qh, axis=-1, keepdims=True) + EPS) * (DN_HEAD_DIM ** -0.5)
        kh = kh * lax.rsqrt(jnp.sum(kh * kh, axis=-1, keepdims=True) + EPS)

        gcol = gc_all[:, h:h + 1]
        grow = gc_t[h:h + 1, :]
        bcol = beta_all[:, h:h + 1]
        glast = gc_all[C - 1:C, h:h + 1]
        dec = jnp.where(causal, jnp.exp(jnp.where(causal, gcol - grow, zero_cc)), zero_cc)
        eg = jnp.exp(gcol)

        kb = kh * bcol
        kb16 = kb.astype(BF16)
        k16 = kh.astype(BF16)
        m_low = jnp.where(strict, _dot_nt(kb16, k16) * dec, zero_cc)
        qk = jnp.where(causal, _dot_nt(qh.astype(BF16), k16) * dec, zero_cc)
        tinv = _unit_lower_inverse(m_low, row, col).astype(BF16)
        u = _dot(tinv, (vh * bcol).astype(BF16))
        w = _dot(tinv, (kb * eg).astype(BF16))

        s_old = state_ref[h]
        s16 = s_old.astype(BF16)
        v_new = u - _dot(w.astype(BF16), s16)
        v16 = v_new.astype(BF16)
        o = _dot((qh * eg).astype(BF16), s16) + _dot(qk.astype(BF16), v16)
        k_dec_t = (kh * jnp.exp(glast - gcol)).T.astype(BF16)
        state_ref[h] = s_old * jnp.exp(glast) + _dot(k_dec_t, v16)

        ms = jnp.mean(o * o, axis=-1, keepdims=True)
        zh = z[:, hs]
        o_ref[:, hs] = (o * lax.rsqrt(ms + EPS) * gn_ref[...]
                        * (zh * (1.0 / (1.0 + jnp.exp(-zh))))).astype(BF16)


def _gdn(dn_in, ab, z, conv_w, neg_a, dt_bias, gnorm, tri, batch, seq):
    T = dn_in.shape[0]
    C = GDN_CHUNK
    nc = seq // C
    return pl.pallas_call(
        _gdn_kernel,
        grid=(batch, nc),
        in_specs=[pl.BlockSpec((C, 3 * DN_WIDTH), lambda bi, t: (bi * nc + t, 0)),
                  pl.BlockSpec((C, 2 * LANES), lambda bi, t: (bi * nc + t, 0)),
                  pl.BlockSpec((C, DN_WIDTH), lambda bi, t: (bi * nc + t, 0)),
                  pl.BlockSpec((CONV_WIDTH, 3 * DN_WIDTH), lambda bi, t: (0, 0)),
                  pl.BlockSpec((1, LANES), lambda bi, t: (0, 0)),
                  pl.BlockSpec((1, LANES), lambda bi, t: (0, 0)),
                  pl.BlockSpec((1, DN_HEAD_DIM), lambda bi, t: (0, 0)),
                  pl.BlockSpec((C, C), lambda bi, t: (0, 0))],
        out_specs=pl.BlockSpec((C, DN_WIDTH), lambda bi, t: (bi * nc + t, 0)),
        out_shape=jax.ShapeDtypeStruct((T, DN_WIDTH), BF16),
        scratch_shapes=[pltpu.VMEM((DN_HEADS, DN_HEAD_DIM, DN_HEAD_DIM), F32),
                        pltpu.VMEM((SUBLANES, 3 * DN_WIDTH), F32)],
        compiler_params=pltpu.CompilerParams(dimension_semantics=("arbitrary", "arbitrary"),
                                             vmem_limit_bytes=VMEM_LIMIT_BYTES),
        name="gated_deltanet",
    )(dn_in, ab, z, conv_w, neg_a, dt_bias, gnorm, tri)


def _outproj_kernel(x_ref, att_ref, dn_ref, wo_ref, g_ref, h_ref, xn_ref):
    h = x_ref[...] + (_dot(att_ref[...], wo_ref[0:ATT_WIDTH, :])
                      + _dot(dn_ref[...], wo_ref[ATT_WIDTH:ATT_WIDTH + DN_WIDTH, :]))
    h_ref[...] = h
    ms = jnp.mean(h * h, axis=-1, keepdims=True)
    xn_ref[...] = (h * lax.rsqrt(ms + EPS) * g_ref[...]).astype(BF16)


def _outproj(x2, att_n, dn_o, w_out, g, tm):
    T = x2.shape[0]
    return pl.pallas_call(
        _outproj_kernel,
        grid=(T // tm,),
        in_specs=[pl.BlockSpec((tm, D_MODEL), lambda i: (i, 0)),
                  pl.BlockSpec((tm, ATT_WIDTH), lambda i: (i, 0)),
                  pl.BlockSpec((tm, DN_WIDTH), lambda i: (i, 0)),
                  pl.BlockSpec((ATT_WIDTH + DN_WIDTH, D_MODEL), lambda i: (0, 0)),
                  pl.BlockSpec((1, D_MODEL), lambda i: (0, 0))],
        out_specs=[pl.BlockSpec((tm, D_MODEL), lambda i: (i, 0)),
                   pl.BlockSpec((tm, D_MODEL), lambda i: (i, 0))],
        out_shape=[jax.ShapeDtypeStruct((T, D_MODEL), F32),
                   jax.ShapeDtypeStruct((T, D_MODEL), BF16)],
        compiler_params=pltpu.CompilerParams(dimension_semantics=("arbitrary",),
                                             vmem_limit_bytes=VMEM_LIMIT_BYTES),
        name="outproj",
    )(x2, att_n, dn_o, w_out, g)


def _oddeven_merge(lo, hi, r):
    step = r * 2
    if step < hi - lo:
        yield from _oddeven_merge(lo, hi, step)
        yield from _oddeven_merge(lo + r, hi, step)
        for i in range(lo + r, hi - r, step):
            yield (i, i + r)
    else:
        yield (lo, lo + r)


def _oddeven_sort_net(lo, hi):
    if hi - lo >= 1:
        mid = lo + (hi - lo) // 2
        yield from _oddeven_sort_net(lo, mid)
        yield from _oddeven_sort_net(mid + 1, hi)
        yield from _oddeven_merge(lo, hi, 1)


_SORT16 = tuple(_oddeven_sort_net(0, PEER_TOPK - 1))
_BITONIC16 = tuple((i, i + d) for d in (8, 4, 2, 1) for i in range(PEER_TOPK) if (i & d) == 0)
_STAIR = tuple((r, q) for r in range(PEER_TOPK) for q in range(PEER_TOPK) if (r + 1) * (q + 1) <= PEER_TOPK)


def _ce_vi(a, b):
    va, ia = a
    vb, ib = b
    a_first = (va > vb) | ((va == vb) & (ia < ib))
    return ((jnp.where(a_first, va, vb), jnp.where(a_first, ia, ib)),
            (jnp.where(a_first, vb, va), jnp.where(a_first, ib, ia)))


def _first_vi(a, b):
    va, ia = a
    vb, ib = b
    a_first = (va > vb) | ((va == vb) & (ia < ib))
    return (jnp.where(a_first, va, vb), jnp.where(a_first, ia, ib))


def _sort16_vi(items):
    items = list(items)
    for i, j in _SORT16:
        items[i], items[j] = _ce_vi(items[i], items[j])
    return items


def _merge_top16_vi(a, b):
    n = PEER_TOPK
    items = [_first_vi(a[i], b[n - 1 - i]) for i in range(n)]
    for i, j in _BITONIC16:
        items[i], items[j] = _ce_vi(items[i], items[j])
    return items


def _sort16_v(vals):
    vals = list(vals)
    for i, j in _SORT16:
        hi = jnp.maximum(vals[i], vals[j])
        lo = jnp.minimum(vals[i], vals[j])
        vals[i], vals[j] = hi, lo
    return vals


def _merge_top16_v(a, b):
    n = PEER_TOPK
    vals = [jnp.maximum(a[i], b[n - 1 - i]) for i in range(n)]
    for i, j in _BITONIC16:
        hi = jnp.maximum(vals[i], vals[j])
        lo = jnp.minimum(vals[i], vals[j])
        vals[i], vals[j] = hi, lo
    return vals


def _top16_of_128(load_key):
    groups = []
    for gi in range(N_KEYS // PEER_TOPK):
        items = [(load_key(gi * PEER_TOPK + t), jnp.full((SUBLANES, LANES), float(gi * PEER_TOPK + t), F32))
                 for t in range(PEER_TOPK)]
        groups.append(_sort16_vi(items))
    while len(groups) > 1:
        groups = [_merge_top16_vi(groups[2 * t], groups[2 * t + 1]) for t in range(len(groups) // 2)]
    return groups[0]


def _pair16(x_f32):
    bits = pltpu.bitcast(x_f32, U32)
    return (bits & jnp.uint32(0xFFFF0000)) | (bits >> 16)


def _topk_select(load0, load1):
    zero = jnp.zeros((SUBLANES, LANES), F32)
    one = jnp.ones((SUBLANES, LANES), F32)
    a = _top16_of_128(load0)
    b = _top16_of_128(load1)
    cand = {(r, q): a[r][0] + b[q][0] for (r, q) in _STAIR}
    row0 = [cand[(0, q)] for q in range(PEER_TOPK)]
    rest = [cand[rq] for rq in _STAIR if rq[0] > 0]
    neg_inf = jnp.full((SUBLANES, LANES), -jnp.inf, F32)
    best = row0
    for s in range(0, len(rest), PEER_TOPK):
        grp = rest[s:s + PEER_TOPK]
        grp = grp + [neg_inf] * (PEER_TOPK - len(grp))
        best = _merge_top16_v(best, _sort16_v(grp))
    thr = best[PEER_TOPK - 1]
    zsum = one
    for jj in range(1, PEER_TOPK):
        zsum = zsum + jnp.exp(best[jj] - best[0])
    inv_z_half = 0.5 / zsum

    n_gt = zero
    for rq in _STAIR:
        n_gt = n_gt + jnp.where(cand[rq] > thr, one, zero)
    need = float(PEER_TOPK) - n_gt
    cnt = zero
    n_row = [zero] * PEER_TOPK
    for (r, q) in _STAIR:
        c = cand[(r, q)]
        eq = c == thr
        take = (c > thr) | (eq & (cnt < need))
        cnt = cnt + jnp.where(eq, one, zero)
        n_row[r] = n_row[r] + jnp.where(take, one, zero)
    return a, b, n_row, inv_z_half


def _count_of_key(k, a, n_row):
    n_k = jnp.zeros((SUBLANES, LANES), F32)
    for r in reversed(range(PEER_TOPK)):
        n_k = jnp.where(a[r][1] == float(k), n_row[r], n_k)
    return n_k


def _rank_of_keys(kv, idx_rows):
    rk = jnp.full(kv.shape, float(PEER_TOPK), F32)
    for q in reversed(range(PEER_TOPK)):
        rk = jnp.where(idx_rows[q] == kv, jnp.full_like(rk, float(q)), rk)
    return rk


def _peer_topk_kernel(xn_ref, wqt_ref, k0_ref, k1_ref, k1h_ref,
                      r1_ref, e1_ref, n2_ref, c2_ref,
                      s0_ref, s1_ref, s1h_ref):
    tb = xn_ref.shape[0]
    half_w = PEER_HEADS * PEER_HALF
    qt = _dot_nt(wqt_ref[...], xn_ref[...]).astype(BF16)
    s0_ref[...] = _dot(k0_ref[...], qt[0:half_w])
    s1_ref[...] = _dot(k1_ref[...], qt[half_w:2 * half_w])
    s1h_ref[...] = _dot(k1h_ref[...], qt[half_w:2 * half_w])

    sub_iota = lax.broadcasted_iota(jnp.int32, (_ROWS16, LANES), 0).astype(F32)

    def slab(gi, carry):
        ls = pl.ds(pl.multiple_of(gi * LANES, LANES), LANES)
        a, b, n_row, inv_z_half = _topk_select(
            lambda k: s0_ref[pl.ds(k * SUBLANES, SUBLANES), ls],
            lambda k: s1_ref[pl.ds(k * SUBLANES, SUBLANES), ls])

        for k in range(N_KEYS):
            rs = pl.ds(k * SUBLANES, SUBLANES)
            c_k = jnp.exp(s0_ref[rs, ls] - a[0][0]) * inv_z_half
            n2_ref[rs, ls] = _pair16(_count_of_key(k, a, n_row))
            c2_ref[rs, ls] = _pair16(c_k.astype(BF16).astype(F32))

        for h in range(PEER_HEADS):
            idx_rows = [jnp.broadcast_to(b[q][1][h:h + 1, :], (_ROWS16, LANES)) for q in range(PEER_TOPK)]
            b0h = jnp.broadcast_to(b[0][0][h:h + 1, :], (_ROWS16, LANES))
            for kb in range(N_KEYS // _ROWS16):
                rs = pl.ds(h * N_KEYS + kb * _ROWS16, _ROWS16)
                rk = _rank_of_keys(sub_iota + float(kb * _ROWS16), idx_rows)
                r1_ref[rs, ls] = rk.astype(BF16)
                e1_ref[rs, ls] = jnp.exp(s1h_ref[rs, ls] - b0h).astype(BF16)
        return carry

    lax.fori_loop(0, tb // LANES, slab, 0)


def _peer_topk(xn2, wqt, k0, k1, k1h, tb):
    T = xn2.shape[0]
    rows = PEER_HEADS * N_KEYS
    kspec = pl.BlockSpec((rows, PEER_HEADS * PEER_HALF), lambda i: (0, 0))
    ospec = pl.BlockSpec((rows, tb), lambda i: (0, i))
    return pl.pallas_call(
        _peer_topk_kernel,
        grid=(T // tb,),
        in_specs=[pl.BlockSpec((tb, D_MODEL), lambda i: (i, 0)),
                  pl.BlockSpec((D_MODEL, D_MODEL), lambda i: (0, 0)),
                  kspec, kspec, kspec],
        out_specs=[ospec, ospec, ospec, ospec],
        out_shape=[jax.ShapeDtypeStruct((rows, T), BF16),
                   jax.ShapeDtypeStruct((rows, T), BF16),
                   jax.ShapeDtypeStruct((rows, T), U32),
                   jax.ShapeDtypeStruct((rows, T), U32)],
        scratch_shapes=[pltpu.VMEM((rows, tb), F32),
                        pltpu.VMEM((rows, tb), F32),
                        pltpu.VMEM((rows, tb), F32)],
        compiler_params=pltpu.CompilerParams(dimension_semantics=("arbitrary",),
                                             vmem_limit_bytes=VMEM_LIMIT_BYTES),
        name="peer_topk",
    )(xn2, wqt, k0, k1, k1h)


_SUB_E = 512


def _peer_dense_kernel(xn_ref, h_ref, r1_ref, e1_ref, n2_ref, c2_ref, u_ref, vt_ref, o_ref, acc_ref):
    c = pl.program_id(1)
    tb = xn_ref.shape[0]
    ec = u_ref.shape[0]
    i0_per_sub = _SUB_E // N_KEYS

    @pl.when(c == 0)
    def _():
        acc_ref[...] = jnp.zeros_like(acc_ref)

    xn = xn_ref[...]
    zero16 = jnp.zeros((_ROWS16, tb), BF16)
    for sub in range(ec // _SUB_E):
        at = _dot_nt(u_ref[sub * _SUB_E:(sub + 1) * _SUB_E, :], xn)
        tiles = []
        for ii in range(i0_per_sub):
            i0 = c * (ec // N_KEYS) + sub * i0_per_sub + ii
            wacc = [zero16] * (N_KEYS // _ROWS16)
            for h in range(PEER_HEADS):
                rowi = i0 * PEER_HEADS + h
                nb = pltpu.bitcast(jnp.broadcast_to(n2_ref[pl.ds(rowi, 1), :], (SUBLANES, tb)), BF16)
                cb = pltpu.bitcast(jnp.broadcast_to(c2_ref[pl.ds(rowi, 1), :], (SUBLANES, tb)), BF16)
                for j in range(N_KEYS // _ROWS16):
                    rs = slice(h * N_KEYS + j * _ROWS16, h * N_KEYS + (j + 1) * _ROWS16)
                    wacc[j] = wacc[j] + jnp.where(r1_ref[rs, :] < nb, e1_ref[rs, :] * cb, zero16)
            for j in range(N_KEYS // _ROWS16):
                a16 = at[ii * N_KEYS + j * _ROWS16:ii * N_KEYS + (j + 1) * _ROWS16, :]
                gl = a16 * (1.0 + lax.erf(a16 * 0.7071067811865476))
                tiles.append(gl.astype(BF16) * wacc[j])
        hmat = jnp.concatenate(tiles, axis=0)
        acc_ref[...] += _dot(vt_ref[:, sub * _SUB_E:(sub + 1) * _SUB_E], hmat)

    @pl.when(c == pl.num_programs(1) - 1)
    def _():
        o_ref[...] = h_ref[...] + acc_ref[...].T


def _peer_dense(xn2, h, r1, e1, n2, c2, u16, vt16, tb, ec):
    T = xn2.shape[0]
    rows = PEER_HEADS * N_KEYS
    tspec = pl.BlockSpec((rows, tb), lambda i, c: (0, i))
    return pl.pallas_call(
        _peer_dense_kernel,
        grid=(T // tb, N_EXPERTS // ec),
        in_specs=[pl.BlockSpec((tb, D_MODEL), lambda i, c: (i, 0)),
                  pl.BlockSpec((tb, D_MODEL), lambda i, c: (i, 0)),
                  tspec, tspec, tspec, tspec,
                  pl.BlockSpec((ec, D_MODEL), lambda i, c: (c, 0)),
                  pl.BlockSpec((D_MODEL, ec), lambda i, c: (0, c))],
        out_specs=pl.BlockSpec((tb, D_MODEL), lambda i, c: (i, 0)),
        out_shape=jax.ShapeDtypeStruct((T, D_MODEL), F32),
        scratch_shapes=[pltpu.VMEM((D_MODEL, tb), F32)],
        compiler_params=pltpu.CompilerParams(dimension_semantics=("arbitrary", "arbitrary"),
                                             vmem_limit_bytes=VMEM_LIMIT_BYTES),
        name="peer_dense",
    )(xn2, h, r1, e1, n2, c2, u16, vt16)


def _block_diag_ones(width, blk):
    idx = np.arange(width) // blk
    return jnp.asarray((idx[:, None] == idx[None, :]).astype(np.float32), dtype=BF16)


def _layer(x, norm_mix_g, w_in, att_q_norm_g, att_k_norm_g, att_sinks, att_out_norm_g, dn_conv_w,
           dn_a_log, dn_dt_bias, dn_out_norm_g, w_out, norm_ffn_g, peer_w_q, peer_sub_keys, peer_u, peer_v):
    B, S, _ = x.shape
    T = B * S
    x2 = x.reshape(T, D_MODEL)

    o_q, o_k, o_v = 0, ATT_WIDTH, ATT_WIDTH + ATT_KV_WIDTH
    o_dn = ATT_WIDTH + 2 * ATT_KV_WIDTH
    o_a = o_dn + 3 * DN_WIDTH
    o_b = o_a + DN_HEADS
    o_z = o_b + DN_HEADS
    pad = jnp.zeros((D_MODEL, LANES - DN_HEADS), w_in.dtype)
    w_all = jnp.concatenate([w_in[:, o_q:o_dn], w_in[:, o_dn:o_a], w_in[:, o_z:o_z + DN_WIDTH],
                             w_in[:, o_a:o_b], pad, w_in[:, o_b:o_z], pad], axis=1).astype(BF16)
    del o_k, o_v

    def lane_pad(vec):
        return jnp.concatenate([vec.astype(F32), jnp.zeros((LANES - vec.shape[0],), F32)]).reshape(1, LANES)

    att_in, dn_in, z_in, ab_in = _inproj(x2, norm_mix_g.reshape(1, D_MODEL), w_all, tm=512)

    att_n = _attention(
        att_in, att_sinks.astype(F32),
        jnp.tile(att_q_norm_g.astype(F32), ATT_HEADS).reshape(1, ATT_WIDTH),
        jnp.tile(att_k_norm_g.astype(F32), ATT_KV_HEADS).reshape(1, ATT_KV_WIDTH),
        att_out_norm_g.astype(F32).reshape(1, ATT_WIDTH),
        _block_diag_ones(ATT_WIDTH, ATT_HEAD_DIM), _block_diag_ones(ATT_KV_WIDTH, ATT_HEAD_DIM),
        seq=S, tq=512)

    tri = jnp.asarray(np.tril(np.ones((GDN_CHUNK, GDN_CHUNK), np.float32)))
    dn_o = _gdn(dn_in, ab_in, z_in, dn_conv_w.astype(F32),
                lane_pad(-jnp.exp(dn_a_log.astype(F32))), lane_pad(dn_dt_bias),
                dn_out_norm_g.astype(F32).reshape(1, DN_HEAD_DIM), tri, batch=B, seq=S)

    h, xn2 = _outproj(x2, att_n, dn_o, w_out.astype(BF16), norm_ffn_g.reshape(1, D_MODEL), tm=512)

    wqt = (peer_w_q.T.reshape(PEER_HEADS, 2, PEER_HALF, D_MODEL).transpose(1, 0, 2, 3)
           .reshape(D_MODEL, D_MODEL).astype(BF16))
    eye = jnp.eye(PEER_HEADS, dtype=peer_sub_keys.dtype)
    kd = jnp.einsum('hpkc,hg->pkhgc', peer_sub_keys, eye)
    k_km = kd.reshape(2, N_KEYS * PEER_HEADS, PEER_HEADS * PEER_HALF).astype(BF16)
    k1_hm = kd[1].transpose(1, 0, 2, 3).reshape(PEER_HEADS * N_KEYS, PEER_HEADS * PEER_HALF).astype(BF16)
    r1, e1, n2, c2 = _peer_topk(xn2, wqt, k_km[0], k_km[1], k1_hm, tb=512)

    out = _peer_dense(xn2, h, r1, e1, n2, c2, peer_u.astype(BF16), peer_v.T.astype(BF16),
                      tb=512, ec=2048)
    return out.reshape(B, S, D_MODEL)


def kernel(x, norm_mix_g, w_in, att_q_norm_g, att_k_norm_g, att_sinks, att_out_norm_g, dn_conv_w, dn_a_log, dn_dt_bias, dn_out_norm_g, w_out, norm_ffn_g, peer_w_q, peer_sub_keys, peer_u, peer_v):
    h = x
    for l in range(norm_mix_g.shape[0]):
        h = _layer(h, norm_mix_g[l], w_in[l], att_q_norm_g[l], att_k_norm_g[l], att_sinks[l],
                   att_out_norm_g[l], dn_conv_w[l], dn_a_log[l], dn_dt_bias[l], dn_out_norm_g[l],
                   w_out[l], norm_ffn_g[l], peer_w_q[l], peer_sub_keys[l], peer_u[l], peer_v[l])
    return h
```

```python
import functools

import numpy as np
import jax
import jax.numpy as jnp
from jax import lax
from jax.experimental import pallas as pl
from jax.experimental.pallas import tpu as pltpu

F32 = jnp.float32
BF16 = jnp.bfloat16
U32 = jnp.uint32

D_MODEL = 1024
ATT_HEADS = 8
ATT_KV_HEADS = 2
ATT_HEAD_DIM = 64
ATT_WIDTH = 512
ATT_KV_WIDTH = 128
ATT_BLOCK = 128
DN_HEADS = 4
DN_HEAD_DIM = 128
DN_WIDTH = 512
CONV_WIDTH = 4
PEER_HEADS = 8
N_KEYS = 128
N_EXPERTS = N_KEYS * N_KEYS
PEER_HALF = 64
PEER_TOPK = 16
EPS = 1e-6

LANES = 128
SUBLANES = 8
VMEM_LIMIT_BYTES = 56 * 1024 * 1024

_C_ATT = 0
_C_DN = 768
_C_Z = _C_DN + 3 * DN_WIDTH
_C_AB = _C_Z + DN_WIDTH
_C_END = _C_AB + 2 * LANES

GDN_CHUNK = 128
_ROWS16 = 2 * SUBLANES


def _dot(a, b):
    return jnp.dot(a, b, preferred_element_type=F32)


def _dot_nt(a, b):
    return lax.dot_general(a, b, (((1,), (1,)), ((), ())), preferred_element_type=F32)


def _split2(a):
    hi = a.astype(BF16)
    lo = (a - hi.astype(F32)).astype(BF16)
    return hi, lo


def _dot3(a, b):
    ah, al = _split2(a)
    bh, bl = _split2(b)
    return _dot(ah, bh) + (_dot(ah, bl) + _dot(al, bh))


def _inproj_kernel(x_ref, g_ref, w_ref, att_ref, dn_ref, z_ref, ab_ref):
    x = x_ref[...]
    ms = jnp.mean(x * x, axis=-1, keepdims=True)
    xn = (x * lax.rsqrt(ms + EPS) * g_ref[...]).astype(BF16)
    att_ref[...] = _dot(xn, w_ref[:, _C_ATT:_C_DN])
    dn_ref[...] = _dot(xn, w_ref[:, _C_DN:_C_Z])
    z_ref[...] = _dot(xn, w_ref[:, _C_Z:_C_AB])
    ab_ref[...] = _dot(xn, w_ref[:, _C_AB:_C_END])


def _inproj(x2, g, w_all, tm):
    T = x2.shape[0]
    return pl.pallas_call(
        _inproj_kernel,
        grid=(T // tm,),
        in_specs=[pl.BlockSpec((tm, D_MODEL), lambda i: (i, 0)),
                  pl.BlockSpec((1, D_MODEL), lambda i: (0, 0)),
                  pl.BlockSpec((D_MODEL, _C_END), lambda i: (0, 0))],
        out_specs=[pl.BlockSpec((tm, 768), lambda i: (i, 0)),
                   pl.BlockSpec((tm, 3 * DN_WIDTH), lambda i: (i, 0)),
                   pl.BlockSpec((tm, DN_WIDTH), lambda i: (i, 0)),
                   pl.BlockSpec((tm, 2 * LANES), lambda i: (i, 0))],
        out_shape=[jax.ShapeDtypeStruct((T, 768), F32),
                   jax.ShapeDtypeStruct((T, 3 * DN_WIDTH), F32),
                   jax.ShapeDtypeStruct((T, DN_WIDTH), F32),
                   jax.ShapeDtypeStruct((T, 2 * LANES), F32)],
        compiler_params=pltpu.CompilerParams(dimension_semantics=("arbitrary",),
                                             vmem_limit_bytes=VMEM_LIMIT_BYTES),
        name="inproj",
    )(x2, g, w_all)


def _head_rms(t, bd, g):
    hi, lo = _split2(t * t)
    ss = _dot(hi, bd) + _dot(lo, bd)
    return t * lax.rsqrt(ss * (1.0 / ATT_HEAD_DIM) + EPS) * g


def _attn_kernel(sink_ref, cur_ref, prev_ref, gq_ref, gk_ref, go_ref, bdq_ref, bdk_ref, o_ref,
                 *, tiles_per_seq):
    i = pl.program_id(0)
    tq = cur_ref.shape[0]
    nblk = tq // ATT_BLOCK
    first = (i % tiles_per_seq) == 0

    q = cur_ref[:, 0:ATT_WIDTH]
    k = cur_ref[:, ATT_WIDTH:ATT_WIDTH + ATT_KV_WIDTH]
    v = cur_ref[:, ATT_WIDTH + ATT_KV_WIDTH:ATT_WIDTH + 2 * ATT_KV_WIDTH]
    kp = prev_ref[:, 0:ATT_KV_WIDTH]
    vp = prev_ref[:, ATT_KV_WIDTH:2 * ATT_KV_WIDTH]

    qn = (_head_rms(q, bdq_ref[...], gq_ref[...]) * (ATT_HEAD_DIM ** -0.5)).astype(BF16)
    kn = jnp.concatenate([_head_rms(kp, bdk_ref[...], gk_ref[...]),
                          _head_rms(k, bdk_ref[...], gk_ref[...])], axis=0)
    vf = jnp.concatenate([vp, v], axis=0)

    lane = lax.broadcasted_iota(jnp.int32, kn.shape, 1)
    left = lane < ATT_HEAD_DIM
    zero = jnp.zeros_like(kn)
    k0l = jnp.where(left, kn, zero)
    k1r = jnp.where(left, zero, kn)
    v0l = jnp.where(left, vf, zero)
    v1r = jnp.where(left, zero, vf)
    kl = (k0l.astype(BF16), pltpu.roll(k1r, ATT_HEAD_DIM, 1).astype(BF16))
    kr = (pltpu.roll(k0l, ATT_HEAD_DIM, 1).astype(BF16), k1r.astype(BF16))
    vl = (v0l.astype(BF16), pltpu.roll(v1r, ATT_HEAD_DIM, 1).astype(BF16))
    vr = (pltpu.roll(v0l, ATT_HEAD_DIM, 1).astype(BF16), v1r.astype(BF16))

    qi = lax.broadcasted_iota(jnp.int32, (ATT_BLOCK, 2 * ATT_BLOCK), 0)
    kj = lax.broadcasted_iota(jnp.int32, (ATT_BLOCK, 2 * ATT_BLOCK), 1)
    rel = qi + ATT_BLOCK - kj
    in_window = (rel >= 0) & (rel < ATT_BLOCK)
    first_key = jnp.where(first, ATT_BLOCK, 0)
    neg_inf = jnp.full((ATT_BLOCK, 2 * ATT_BLOCK), -jnp.inf, F32)

    def softmax_rows(s, sink):
        m = jnp.maximum(jnp.max(s, axis=-1, keepdims=True), sink)
        p = jnp.exp(s - m)
        den = jnp.sum(p, axis=-1, keepdims=True) + jnp.exp(sink - m)
        return (p * (1.0 / den)).astype(BF16)

    for j in range(nblk):
        rows = slice(j * ATT_BLOCK, (j + 1) * ATT_BLOCK)
        krows = slice(j * ATT_BLOCK, (j + 2) * ATT_BLOCK)
        if j == 0:
            mask = in_window & (kj >= first_key)
        else:
            mask = in_window
        pairs = []
        for c in range(ATT_KV_HEADS):
            qe = jnp.concatenate([qn[rows, (2 * c) * LANES:(2 * c + 1) * LANES],
                                  qn[rows, (2 * c + 1) * LANES:(2 * c + 2) * LANES]], axis=0)
            s_even = _dot_nt(qe, kl[c][krows])
            s_odd = _dot_nt(qe, kr[c][krows])
            for half in range(2):
                hr = slice(half * ATT_BLOCK, (half + 1) * ATT_BLOCK)
                h_even = 4 * c + 2 * half
                p_e = softmax_rows(jnp.where(mask, s_even[hr], neg_inf), sink_ref[h_even])
                p_o = softmax_rows(jnp.where(mask, s_odd[hr], neg_inf), sink_ref[h_even + 1])
                pairs.append(_dot(p_e, vl[c][krows]) + _dot(p_o, vr[c][krows]))
        att = jnp.concatenate(pairs, axis=1)
        ms = jnp.mean(att * att, axis=-1, keepdims=True)
        o_ref[rows, :] = (att * lax.rsqrt(ms + EPS) * go_ref[...]).astype(BF16)


def _attention(att_in, sinks, gq, gk, go, bdq, bdk, seq, tq):
    T = att_in.shape[0]
    tiles_per_seq = seq // tq
    blk_per_tile = tq // ATT_BLOCK
    kv_col_blk = ATT_WIDTH // (2 * ATT_KV_WIDTH)
    return pl.pallas_call(
        functools.partial(_attn_kernel, tiles_per_seq=tiles_per_seq),
        grid_spec=pltpu.PrefetchScalarGridSpec(
            num_scalar_prefetch=1,
            grid=(T // tq,),
            in_specs=[pl.BlockSpec((tq, 768), lambda i, s: (i, 0)),
                      pl.BlockSpec((ATT_BLOCK, 2 * ATT_KV_WIDTH),
                                   lambda i, s: (jnp.maximum(i * blk_per_tile - 1, 0), kv_col_blk)),
                      pl.BlockSpec((1, ATT_WIDTH), lambda i, s: (0, 0)),
                      pl.BlockSpec((1, ATT_KV_WIDTH), lambda i, s: (0, 0)),
                      pl.BlockSpec((1, ATT_WIDTH), lambda i, s: (0, 0)),
                      pl.BlockSpec((ATT_WIDTH, ATT_WIDTH), lambda i, s: (0, 0)),
                      pl.BlockSpec((ATT_KV_WIDTH, ATT_KV_WIDTH), lambda i, s: (0, 0))],
            out_specs=pl.BlockSpec((tq, ATT_WIDTH), lambda i, s: (i, 0)),
        ),
        out_shape=jax.ShapeDtypeStruct((T, ATT_WIDTH), BF16),
        compiler_params=pltpu.CompilerParams(dimension_semantics=("arbitrary",),
                                             vmem_limit_bytes=VMEM_LIMIT_BYTES),
        name="swa_attention",
    )(sinks, att_in, att_in, gq, gk, go, bdq, bdk)


def _unit_lower_inverses(m_lows, row, col):
    n = m_lows[0].shape[0]
    zero = jnp.zeros_like(m_lows[0])
    eye = jnp.where(row == col, jnp.ones_like(zero), zero)
    same = (row >> 3) == (col >> 3)
    nms = [jnp.where(same, -m, zero) for m in m_lows]
    ps = [eye + nm for nm in nms]
    n2s = [_dot3(nm, nm) for nm in nms]
    ps = [p + _dot3(p, n2) for p, n2 in zip(ps, n2s)]
    n4s = [_dot3(n2, n2) for n2 in n2s]
    ps = [p + _dot3(p, n4) for p, n4 in zip(ps, n4s)]
    shift = 3
    while (1 << shift) < n:
        same_next = (row >> (shift + 1)) == (col >> (shift + 1))
        lower_left = same_next & jnp.logical_not(same)
        ts = [_dot3(p, jnp.where(lower_left, m, zero)) for p, m in zip(ps, m_lows)]
        ps = [p - _dot3(t, p) for p, t in zip(ps, ts)]
        same = same_next
        shift += 1
    return ps


def _silu(v):
    return v * (1.0 / (1.0 + jnp.exp(-v)))


def _gdn_kernel(dn_ref, ab_ref, z_ref, cw_ref, nega_ref, dtb_ref, gn_ref, tri_ref, o_ref,
                state_ref, carry_ref):
    t = pl.program_id(1)
    C = GDN_CHUNK
    nb = dn_ref.shape[0]

    @pl.when(t == 0)
    def _():
        state_ref[...] = jnp.zeros_like(state_ref)
        carry_ref[...] = jnp.zeros_like(carry_ref)

    row = lax.broadcasted_iota(jnp.int32, (C, C), 0)
    col = lax.broadcasted_iota(jnp.int32, (C, C), 1)
    causal = row >= col
    strict = row > col
    zero_cc = jnp.zeros((C, C), F32)
    cw = cw_ref[...]

    ids, qs, ks, vbs, kbes, decs, egs, kdecs, cds = [], [], [], [], [], [], [], [], []
    for bb in range(nb):
        x = dn_ref[bb]
        xs = jnp.concatenate([carry_ref[bb], x], axis=0)
        carry_ref[bb] = x[C - SUBLANES:C, :]
        y = xs[5:5 + C] * cw[0:1, :]
        for jw in range(1, CONV_WIDTH):
            y = y + xs[5 + jw:5 + jw + C] * cw[jw:jw + 1, :]
        y = _silu(y)

        ab = ab_ref[bb]
        sp_in = ab[:, 0:LANES] + dtb_ref[...]
        softplus = jnp.maximum(sp_in, 0.0) + jnp.log1p(jnp.exp(-jnp.abs(sp_in)))
        g_all = nega_ref[...] * softplus
        beta_all = 1.0 / (1.0 + jnp.exp(-ab[:, LANES:2 * LANES]))
        gc_all = _dot3(tri_ref[...], g_all)
        gc_t = gc_all.T

        for h in range(DN_HEADS):
            qh = y[:, h * DN_HEAD_DIM:(h + 1) * DN_HEAD_DIM]
            kh = y[:, DN_WIDTH + h * DN_HEAD_DIM:DN_WIDTH + (h + 1) * DN_HEAD_DIM]
            vh = y[:, 2 * DN_WIDTH + h * DN_HEAD_DIM:2 * DN_WIDTH + (h + 1) * DN_HEAD_DIM]
            qh = qh * lax.rsqrt(jnp.sum(qh * qh, axis=-1, keepdims=True) + EPS) * (DN_HEAD_DIM ** -0.5)
            kh = kh * lax.rsqrt(jnp.sum(kh * kh, axis=-1, keepdims=True) + EPS)
            gcol = gc_all[:, h:h + 1]
            grow = gc_t[h:h + 1, :]
            bcol = beta_all[:, h:h + 1]
            glast = gc_all[C - 1:C, h:h + 1]
            eg = jnp.exp(gcol)
            kb = kh * bcol
            ids.append((bb, h))
            qs.append(qh)
            ks.append(kh.astype(BF16))
            vbs.append(jnp.concatenate([vh * bcol, kb * eg], axis=1).astype(BF16))
            kbes.append(kb.astype(BF16))
            decs.append(jnp.where(causal, jnp.exp(jnp.where(causal, gcol - grow, zero_cc)), zero_cc))
            egs.append(eg)
            kdecs.append((kh * jnp.exp(glast - gcol)).T.astype(BF16))
            cds.append(jnp.exp(glast))

    n = len(ids)
    m_lows = [jnp.where(strict, _dot_nt(kbes[i], ks[i]) * decs[i], zero_cc) for i in range(n)]
    qks = [jnp.where(causal, _dot_nt(qs[i].astype(BF16), ks[i]) * decs[i], zero_cc).astype(BF16)
           for i in range(n)]
    tinvs = _unit_lower_inverses(m_lows, row, col)
    uws = [_dot(tinvs[i].astype(BF16), vbs[i]) for i in range(n)]
    s_olds = [state_ref[bb, h] for (bb, h) in ids]
    s16s = [s.astype(BF16) for s in s_olds]
    v16s = [(uws[i][:, 0:DN_HEAD_DIM] - _dot(uws[i][:, DN_HEAD_DIM:].astype(BF16), s16s[i])).astype(BF16)
            for i in range(n)]
    outs = [_dot((qs[i] * egs[i]).astype(BF16), s16s[i]) + _dot(qks[i], v16s[i]) for i in range(n)]
    for i, (bb, h) in enumerate(ids):
        state_ref[bb, h] = s_olds[i] * cds[i] + _dot(kdecs[i], v16s[i])

    for i, (bb, h) in enumerate(ids):
        hs = slice(h * DN_HEAD_DIM, (h + 1) * DN_HEAD_DIM)
        o = outs[i]
        ms = jnp.mean(o * o, axis=-1, keepdims=True)
        o_ref[bb, :, hs] = (o * lax.rsqrt(ms + EPS) * gn_ref[...] * _silu(z_ref[bb, :, hs])).astype(BF16)


def _gdn(dn_in, ab, z, conv_w, neg_a, dt_bias, gnorm, tri, batch, seq, nb):
    C = GDN_CHUNK
    nc = seq // C
    blk3 = lambda w: pl.BlockSpec((nb, C, w), lambda bi, t: (bi, t, 0))
    const = lambda shape: pl.BlockSpec(shape, lambda bi, t: (0, 0))
    out = pl.pallas_call(
        _gdn_kernel,
        grid=(batch // nb, nc),
        in_specs=[blk3(3 * DN_WIDTH), blk3(2 * LANES), blk3(DN_WIDTH),
                  const((CONV_WIDTH, 3 * DN_WIDTH)), const((1, LANES)), const((1, LANES)),
                  const((1, DN_HEAD_DIM)), const((C, C))],
        out_specs=blk3(DN_WIDTH),
        out_shape=jax.ShapeDtypeStruct((batch, seq, DN_WIDTH), BF16),
        scratch_shapes=[pltpu.VMEM((nb, DN_HEADS, DN_HEAD_DIM, DN_HEAD_DIM), F32),
                        pltpu.VMEM((nb, SUBLANES, 3 * DN_WIDTH), F32)],
        compiler_params=pltpu.CompilerParams(dimension_semantics=("arbitrary", "arbitrary"),
                                             vmem_limit_bytes=VMEM_LIMIT_BYTES),
        name="gated_deltanet",
    )(dn_in.reshape(batch, seq, -1), ab.reshape(batch, seq, -1), z.reshape(batch, seq, -1),
      conv_w, neg_a, dt_bias, gnorm, tri)
    return out.reshape(batch * seq, DN_WIDTH)


def _outproj_kernel(x_ref, att_ref, dn_ref, wo_ref, g_ref, h_ref, xn_ref):
    h = x_ref[...] + (_dot(att_ref[...], wo_ref[0:ATT_WIDTH, :])
                      + _dot(dn_ref[...], wo_ref[ATT_WIDTH:ATT_WIDTH + DN_WIDTH, :]))
    h_ref[...] = h
    ms = jnp.mean(h * h, axis=-1, keepdims=True)
    xn_ref[...] = (h * lax.rsqrt(ms + EPS) * g_ref[...]).astype(BF16)


def _outproj(x2, att_n, dn_o, w_out, g, tm):
    T = x2.shape[0]
    return pl.pallas_call(
        _outproj_kernel,
        grid=(T // tm,),
        in_specs=[pl.BlockSpec((tm, D_MODEL), lambda i: (i, 0)),
                  pl.BlockSpec((tm, ATT_WIDTH), lambda i: (i, 0)),
                  pl.BlockSpec((tm, DN_WIDTH), lambda i: (i, 0)),
                  pl.BlockSpec((ATT_WIDTH + DN_WIDTH, D_MODEL), lambda i: (0, 0)),
                  pl.BlockSpec((1, D_MODEL), lambda i: (0, 0))],
        out_specs=[pl.BlockSpec((tm, D_MODEL), lambda i: (i, 0)),
                   pl.BlockSpec((tm, D_MODEL), lambda i: (i, 0))],
        out_shape=[jax.ShapeDtypeStruct((T, D_MODEL), F32),
                   jax.ShapeDtypeStruct((T, D_MODEL), BF16)],
        compiler_params=pltpu.CompilerParams(dimension_semantics=("arbitrary",),
                                             vmem_limit_bytes=VMEM_LIMIT_BYTES),
        name="outproj",
    )(x2, att_n, dn_o, w_out, g)


def _oddeven_merge(lo, hi, r):
    step = r * 2
    if step < hi - lo:
        yield from _oddeven_merge(lo, hi, step)
        yield from _oddeven_merge(lo + r, hi, step)
        for i in range(lo + r, hi - r, step):
            yield (i, i + r)
    else:
        yield (lo, lo + r)


def _oddeven_sort_net(lo, hi):
    if hi - lo >= 1:
        mid = lo + (hi - lo) // 2
        yield from _oddeven_sort_net(lo, mid)
        yield from _oddeven_sort_net(mid + 1, hi)
        yield from _oddeven_merge(lo, hi, 1)


_SORT16 = tuple(_oddeven_sort_net(0, PEER_TOPK - 1))
_BITONIC16 = tuple((i, i + d) for d in (8, 4, 2, 1) for i in range(PEER_TOPK) if (i & d) == 0)
_STAIR = tuple((r, q) for r in range(PEER_TOPK) for q in range(PEER_TOPK) if (r + 1) * (q + 1) <= PEER_TOPK)


def _ce_vi(a, b):
    va, ia = a
    vb, ib = b
    a_first = (va > vb) | ((va == vb) & (ia < ib))
    return ((jnp.where(a_first, va, vb), jnp.where(a_first, ia, ib)),
            (jnp.where(a_first, vb, va), jnp.where(a_first, ib, ia)))


def _first_vi(a, b):
    va, ia = a
    vb, ib = b
    a_first = (va > vb) | ((va == vb) & (ia < ib))
    return (jnp.where(a_first, va, vb), jnp.where(a_first, ia, ib))


def _sort16_vi(items):
    items = list(items)
    for i, j in _SORT16:
        items[i], items[j] = _ce_vi(items[i], items[j])
    return items


def _merge_top16_vi(a, b):
    n = PEER_TOPK
    items = [_first_vi(a[i], b[n - 1 - i]) for i in range(n)]
    for i, j in _BITONIC16:
        items[i], items[j] = _ce_vi(items[i], items[j])
    return items


def _sort16_v(vals):
    vals = list(vals)
    for i, j in _SORT16:
        hi = jnp.maximum(vals[i], vals[j])
        lo = jnp.minimum(vals[i], vals[j])
        vals[i], vals[j] = hi, lo
    return vals


def _merge_top16_v(a, b):
    n = PEER_TOPK
    vals = [jnp.maximum(a[i], b[n - 1 - i]) for i in range(n)]
    for i, j in _BITONIC16:
        hi = jnp.maximum(vals[i], vals[j])
        lo = jnp.minimum(vals[i], vals[j])
        vals[i], vals[j] = hi, lo
    return vals


def _top16_of_128(load_key):
    groups = []
    for gi in range(N_KEYS // PEER_TOPK):
        items = [(load_key(gi * PEER_TOPK + t), jnp.full((SUBLANES, LANES), float(gi * PEER_TOPK + t), F32))
                 for t in range(PEER_TOPK)]
        groups.append(_sort16_vi(items))
    while len(groups) > 1:
        groups = [_merge_top16_vi(groups[2 * t], groups[2 * t + 1]) for t in range(len(groups) // 2)]
    return groups[0]


def _pair16(x_f32):
    bits = pltpu.bitcast(x_f32, U32)
    return (bits & jnp.uint32(0xFFFF0000)) | (bits >> 16)


def _topk_select(load0, load1):
    zero = jnp.zeros((SUBLANES, LANES), F32)
    one = jnp.ones((SUBLANES, LANES), F32)
    a = _top16_of_128(load0)
    b = _top16_of_128(load1)
    cand = {(r, q): a[r][0] + b[q][0] for (r, q) in _STAIR}
    row0 = [cand[(0, q)] for q in range(PEER_TOPK)]
    rest = [cand[rq] for rq in _STAIR if rq[0] > 0]
    neg_inf = jnp.full((SUBLANES, LANES), -jnp.inf, F32)
    best = row0
    for s in range(0, len(rest), PEER_TOPK):
        grp = rest[s:s + PEER_TOPK]
        grp = grp + [neg_inf] * (PEER_TOPK - len(grp))
        best = _merge_top16_v(best, _sort16_v(grp))
    thr = best[PEER_TOPK - 1]
    zsum = one
    for jj in range(1, PEER_TOPK):
        zsum = zsum + jnp.exp(best[jj] - best[0])
    inv_z_half = 0.5 / zsum

    n_gt = zero
    for rq in _STAIR:
        n_gt = n_gt + jnp.where(cand[rq] > thr, one, zero)
    need = float(PEER_TOPK) - n_gt
    cnt = zero
    n_row = [zero] * PEER_TOPK
    for (r, q) in _STAIR:
        c = cand[(r, q)]
        eq = c == thr
        take = (c > thr) | (eq & (cnt < need))
        cnt = cnt + jnp.where(eq, one, zero)
        n_row[r] = n_row[r] + jnp.where(take, one, zero)
    return a, b, n_row, inv_z_half


def _count_of_key(k, a, n_row):
    n_k = jnp.zeros((SUBLANES, LANES), F32)
    for r in reversed(range(PEER_TOPK)):
        n_k = jnp.where(a[r][1] == float(k), n_row[r], n_k)
    return n_k


def _rank_of_keys(kv, idx_rows):
    rk = jnp.full(kv.shape, float(PEER_TOPK), F32)
    for q in reversed(range(PEER_TOPK)):
        rk = jnp.where(idx_rows[q] == kv, jnp.full_like(rk, float(q)), rk)
    return rk


def _peer_topk_kernel(xn_ref, wqt_ref, k0_ref, k1_ref, k1h_ref,
                      r1_ref, e1_ref, n2_ref, c2_ref,
                      s0_ref, s1_ref, s1h_ref):
    tb = xn_ref.shape[0]
    half_w = PEER_HEADS * PEER_HALF
    qt = _dot_nt(wqt_ref[...], xn_ref[...]).astype(BF16)
    s0_ref[...] = _dot(k0_ref[...], qt[0:half_w])
    s1_ref[...] = _dot(k1_ref[...], qt[half_w:2 * half_w])
    s1h_ref[...] = _dot(k1h_ref[...], qt[half_w:2 * half_w])

    sub_iota = lax.broadcasted_iota(jnp.int32, (_ROWS16, LANES), 0).astype(F32)

    def slab(gi, carry):
        ls = pl.ds(pl.multiple_of(gi * LANES, LANES), LANES)
        a, b, n_row, inv_z_half = _topk_select(
            lambda k: s0_ref[pl.ds(k * SUBLANES, SUBLANES), ls],
            lambda k: s1_ref[pl.ds(k * SUBLANES, SUBLANES), ls])

        for k in range(N_KEYS):
            rs = pl.ds(k * SUBLANES, SUBLANES)
            c_k = jnp.exp(s0_ref[rs, ls] - a[0][0]) * inv_z_half
            n2_ref[rs, ls] = _pair16(_count_of_key(k, a, n_row))
            c2_ref[rs, ls] = _pair16(c_k.astype(BF16).astype(F32))

        for h in range(PEER_HEADS):
            idx_rows = [jnp.broadcast_to(b[q][1][h:h + 1, :], (_ROWS16, LANES)) for q in range(PEER_TOPK)]
            b0h = jnp.broadcast_to(b[0][0][h:h + 1, :], (_ROWS16, LANES))
            for kb in range(N_KEYS // _ROWS16):
                rs = pl.ds(h * N_KEYS + kb * _ROWS16, _ROWS16)
                rk = _rank_of_keys(sub_iota + float(kb * _ROWS16), idx_rows)
                r1_ref[rs, ls] = rk.astype(BF16)
                e1_ref[rs, ls] = jnp.exp(s1h_ref[rs, ls] - b0h).astype(BF16)
        return carry

    lax.fori_loop(0, tb // LANES, slab, 0)


def _peer_topk(xn2, wqt, k0, k1, k1h, tb):
    T = xn2.shape[0]
    rows = PEER_HEADS * N_KEYS
    kspec = pl.BlockSpec((rows, PEER_HEADS * PEER_HALF), lambda i: (0, 0))
    ospec = pl.BlockSpec((rows, tb), lambda i: (0, i))
    return pl.pallas_call(
        _peer_topk_kernel,
        grid=(T // tb,),
        in_specs=[pl.BlockSpec((tb, D_MODEL), lambda i: (i, 0)),
                  pl.BlockSpec((D_MODEL, D_MODEL), lambda i: (0, 0)),
                  kspec, kspec, kspec],
        out_specs=[ospec, ospec, ospec, ospec],
        out_shape=[jax.ShapeDtypeStruct((rows, T), BF16),
                   jax.ShapeDtypeStruct((rows, T), BF16),
                   jax.ShapeDtypeStruct((rows, T), U32),
                   jax.ShapeDtypeStruct((rows, T), U32)],
        scratch_shapes=[pltpu.VMEM((rows, tb), F32),
                        pltpu.VMEM((rows, tb), F32),
                        pltpu.VMEM((rows, tb), F32)],
        compiler_params=pltpu.CompilerParams(dimension_semantics=("arbitrary",),
                                             vmem_limit_bytes=VMEM_LIMIT_BYTES),
        name="peer_topk",
    )(xn2, wqt, k0, k1, k1h)


_SUB_E = 512
_W_TILE = 256


def _peer_dense_kernel(xn_ref, h_ref, r1_ref, e1_ref, n2_ref, c2_ref, u_ref, vt_ref, o_ref,
                       acc_ref, hm_ref, at_ref):
    c = pl.program_id(1)
    tb = xn_ref.shape[0]
    ec = u_ref.shape[0]
    i0_per_sub = _SUB_E // N_KEYS

    @pl.when(c == 0)
    def _():
        acc_ref[...] = jnp.zeros_like(acc_ref)

    xn = xn_ref[...]
    tw = min(tb, _W_TILE)
    zero16 = jnp.zeros((_ROWS16, tw), BF16)
    n_j = N_KEYS // _ROWS16
    n_sub = ec // _SUB_E

    def expert_acts(sub):
        return _dot_nt(u_ref[sub * _SUB_E:(sub + 1) * _SUB_E, :], xn)

    at_ref[0] = expert_acts(0)
    for sub in range(n_sub):
        if sub + 1 < n_sub:
            at_ref[(sub + 1) % 2] = expert_acts(sub + 1)
        at = at_ref.at[sub % 2]
        for ii in range(i0_per_sub):
            i0 = c * (ec // N_KEYS) + sub * i0_per_sub + ii
            for lt in range(tb // tw):
                ls = slice(lt * tw, (lt + 1) * tw)
                wacc = [None] * n_j
                for h in range(PEER_HEADS):
                    rowi = pl.ds(i0 * PEER_HEADS + h, 1)
                    nb = pltpu.bitcast(jnp.broadcast_to(n2_ref[rowi, ls], (SUBLANES, tw)), BF16)
                    cb = pltpu.bitcast(jnp.broadcast_to(c2_ref[rowi, ls], (SUBLANES, tw)), BF16)
                    for j in range(n_j):
                        rs = slice(h * N_KEYS + j * _ROWS16, h * N_KEYS + (j + 1) * _ROWS16)
                        term = jnp.where(r1_ref[rs, ls] < nb, e1_ref[rs, ls] * cb, zero16)
                        wacc[j] = term if wacc[j] is None else wacc[j] + term
                for j in range(n_j):
                    er = slice(ii * N_KEYS + j * _ROWS16, ii * N_KEYS + (j + 1) * _ROWS16)
                    a16 = at[er, ls]
                    gl = a16 * (1.0 + lax.erf(a16 * 0.7071067811865476))
                    hm_ref[sub, er, ls] = gl.astype(BF16) * wacc[j]
        acc_ref[...] += _dot(vt_ref[:, sub * _SUB_E:(sub + 1) * _SUB_E], hm_ref[sub])

    @pl.when(c == pl.num_programs(1) - 1)
    def _():
        o_ref[...] = h_ref[...] + acc_ref[...].T


def _peer_dense(xn2, h, r1, e1, n2, c2, u16, vt16, tb, ec):
    T = xn2.shape[0]
    rows = PEER_HEADS * N_KEYS
    tspec = pl.BlockSpec((rows, tb), lambda i, c: (0, i))
    return pl.pallas_call(
        _peer_dense_kernel,
        grid=(T // tb, N_EXPERTS // ec),
        in_specs=[pl.BlockSpec((tb, D_MODEL), lambda i, c: (i, 0)),
                  pl.BlockSpec((tb, D_MODEL), lambda i, c: (i, 0)),
                  tspec, tspec, tspec, tspec,
                  pl.BlockSpec((ec, D_MODEL), lambda i, c: (c, 0)),
                  pl.BlockSpec((D_MODEL, ec), lambda i, c: (0, c))],
        out_specs=pl.BlockSpec((tb, D_MODEL), lambda i, c: (i, 0)),
        out_shape=jax.ShapeDtypeStruct((T, D_MODEL), F32),
        scratch_shapes=[pltpu.VMEM((D_MODEL, tb), F32),
                        pltpu.VMEM((ec // _SUB_E, _SUB_E, tb), BF16),
                        pltpu.VMEM((2, _SUB_E, tb), F32)],
        compiler_params=pltpu.CompilerParams(dimension_semantics=("arbitrary", "arbitrary"),
                                             vmem_limit_bytes=VMEM_LIMIT_BYTES),
        name="peer_dense",
    )(xn2, h, r1, e1, n2, c2, u16, vt16)


def _block_diag_ones(width, blk):
    idx = np.arange(width) // blk
    return jnp.asarray((idx[:, None] == idx[None, :]).astype(np.float32), dtype=BF16)


def _layer(x, norm_mix_g, w_in, att_q_norm_g, att_k_norm_g, att_sinks, att_out_norm_g, dn_conv_w,
           dn_a_log, dn_dt_bias, dn_out_norm_g, w_out, norm_ffn_g, peer_w_q, peer_sub_keys, peer_u, peer_v):
    B, S, _ = x.shape
    T = B * S
    x2 = x.reshape(T, D_MODEL)

    o_q, o_k, o_v = 0, ATT_WIDTH, ATT_WIDTH + ATT_KV_WIDTH
    o_dn = ATT_WIDTH + 2 * ATT_KV_WIDTH
    o_a = o_dn + 3 * DN_WIDTH
    o_b = o_a + DN_HEADS
    o_z = o_b + DN_HEADS
    pad = jnp.zeros((D_MODEL, LANES - DN_HEADS), w_in.dtype)
    w_all = jnp.concatenate([w_in[:, o_q:o_dn], w_in[:, o_dn:o_a], w_in[:, o_z:o_z + DN_WIDTH],
                             w_in[:, o_a:o_b], pad, w_in[:, o_b:o_z], pad], axis=1).astype(BF16)
    del o_k, o_v

    def lane_pad(vec):
        return jnp.concatenate([vec.astype(F32), jnp.zeros((LANES - vec.shape[0],), F32)]).reshape(1, LANES)

    att_in, dn_in, z_in, ab_in = _inproj(x2, norm_mix_g.reshape(1, D_MODEL), w_all, tm=512)

    att_n = _attention(
        att_in, att_sinks.astype(F32),
        jnp.tile(att_q_norm_g.astype(F32), ATT_HEADS).reshape(1, ATT_WIDTH),
        jnp.tile(att_k_norm_g.astype(F32), ATT_KV_HEADS).reshape(1, ATT_KV_WIDTH),
        att_out_norm_g.astype(F32).reshape(1, ATT_WIDTH),
        _block_diag_ones(ATT_WIDTH, ATT_HEAD_DIM), _block_diag_ones(ATT_KV_WIDTH, ATT_HEAD_DIM),
        seq=S, tq=512)

    tri = jnp.asarray(np.tril(np.ones((GDN_CHUNK, GDN_CHUNK), np.float32)))
    dn_o = _gdn(dn_in, ab_in, z_in, dn_conv_w.astype(F32),
                lane_pad(-jnp.exp(dn_a_log.astype(F32))), lane_pad(dn_dt_bias),
                dn_out_norm_g.astype(F32).reshape(1, DN_HEAD_DIM), tri, batch=B, seq=S, nb=2)

    h, xn2 = _outproj(x2, att_n, dn_o, w_out.astype(BF16), norm_ffn_g.reshape(1, D_MODEL), tm=512)

    wqt = (peer_w_q.T.reshape(PEER_HEADS, 2, PEER_HALF, D_MODEL).transpose(1, 0, 2, 3)
           .reshape(D_MODEL, D_MODEL).astype(BF16))
    eye = jnp.eye(PEER_HEADS, dtype=peer_sub_keys.dtype)
    kd = jnp.einsum('hpkc,hg->pkhgc', peer_sub_keys, eye)
    k_km = kd.reshape(2, N_KEYS * PEER_HEADS, PEER_HEADS * PEER_HALF).astype(BF16)
    k1_hm = kd[1].transpose(1, 0, 2, 3).reshape(PEER_HEADS * N_KEYS, PEER_HEADS * PEER_HALF).astype(BF16)
    r1, e1, n2, c2 = _peer_topk(xn2, wqt, k_km[0], k_km[1], k1_hm, tb=512)

    out = _peer_dense(xn2, h, r1, e1, n2, c2, peer_u.astype(BF16), peer_v.T.astype(BF16),
                      tb=512, ec=2048)
    return out.reshape(B, S, D_MODEL)


def kernel(x, norm_mix_g, w_in, att_q_norm_g, att_k_norm_g, att_sinks, att_out_norm_g, dn_conv_w, dn_a_log, dn_dt_bias, dn_out_norm_g, w_out, norm_ffn_g, peer_w_q, peer_sub_keys, peer_u, peer_v):
    h = x
    for l in range(norm_mix_g.shape[0]):
        h = _layer(h, norm_mix_g[l], w_in[l], att_q_norm_g[l], att_k_norm_g[l], att_sinks[l],
                   att_out_norm_g[l], dn_conv_w[l], dn_a_log[l], dn_dt_bias[l], dn_out_norm_g[l],
                   w_out[l], norm_ffn_g[l], peer_w_q[l], peer_sub_keys[l], peer_u[l], peer_v[l])
    return h
```

```python
import functools

import numpy as np
import jax
import jax.numpy as jnp
from jax import lax
from jax.experimental import pallas as pl
from jax.experimental.pallas import tpu as pltpu

F32 = jnp.float32
BF16 = jnp.bfloat16

D_MODEL = 1024
ATT_HEADS = 8
ATT_KV_HEADS = 2
ATT_HEAD_DIM = 64
ATT_WIDTH = 512
ATT_KV_WIDTH = 128
ATT_BLOCK = 128
DN_HEADS = 4
DN_HEAD_DIM = 128
DN_WIDTH = 512
CONV_WIDTH = 4
PEER_HEADS = 8
N_KEYS = 128
N_EXPERTS = N_KEYS * N_KEYS
PEER_HALF = 64
PEER_TOPK = 16
EPS = 1e-6

LANES = 128
SUBLANES = 8
VMEM_LIMIT_BYTES = 56 * 1024 * 1024

_C_ATT = 0
_C_DN = 768
_C_Z = _C_DN + 3 * DN_WIDTH
_C_AB = _C_Z + DN_WIDTH
_C_END = _C_AB + 2 * LANES

GDN_CHUNK = 128
_ROWS16 = 2 * SUBLANES


def _dot(a, b):
    return jnp.dot(a, b, preferred_element_type=F32)


def _dot_nt(a, b):
    return lax.dot_general(a, b, (((1,), (1,)), ((), ())), preferred_element_type=F32)


def _split2(a):
    hi = a.astype(BF16)
    lo = (a - hi.astype(F32)).astype(BF16)
    return hi, lo


def _dot3(a, b):
    ah, al = _split2(a)
    bh, bl = _split2(b)
    return _dot(ah, bh) + (_dot(ah, bl) + _dot(al, bh))


def _inproj_kernel(x_ref, g_ref, w_ref, att_ref, dn_ref, z_ref, ab_ref):
    x = x_ref[...]
    ms = jnp.mean(x * x, axis=-1, keepdims=True)
    xn = (x * lax.rsqrt(ms + EPS) * g_ref[...]).astype(BF16)
    att_ref[...] = _dot(xn, w_ref[:, _C_ATT:_C_DN])
    dn_ref[...] = _dot(xn, w_ref[:, _C_DN:_C_Z])
    z_ref[...] = _dot(xn, w_ref[:, _C_Z:_C_AB])
    ab_ref[...] = _dot(xn, w_ref[:, _C_AB:_C_END])


def _inproj(x2, g, w_all, tm):
    T = x2.shape[0]
    return pl.pallas_call(
        _inproj_kernel,
        grid=(T // tm,),
        in_specs=[pl.BlockSpec((tm, D_MODEL), lambda i: (i, 0)),
                  pl.BlockSpec((1, D_MODEL), lambda i: (0, 0)),
                  pl.BlockSpec((D_MODEL, _C_END), lambda i: (0, 0))],
        out_specs=[pl.BlockSpec((tm, 768), lambda i: (i, 0)),
                   pl.BlockSpec((tm, 3 * DN_WIDTH), lambda i: (i, 0)),
                   pl.BlockSpec((tm, DN_WIDTH), lambda i: (i, 0)),
                   pl.BlockSpec((tm, 2 * LANES), lambda i: (i, 0))],
        out_shape=[jax.ShapeDtypeStruct((T, 768), F32),
                   jax.ShapeDtypeStruct((T, 3 * DN_WIDTH), F32),
                   jax.ShapeDtypeStruct((T, DN_WIDTH), F32),
                   jax.ShapeDtypeStruct((T, 2 * LANES), F32)],
        compiler_params=pltpu.CompilerParams(dimension_semantics=("arbitrary",),
                                             vmem_limit_bytes=VMEM_LIMIT_BYTES),
        name="inproj",
    )(x2, g, w_all)


def _head_rms(t, bd, g):
    hi, lo = _split2(t * t)
    ss = _dot(hi, bd) + _dot(lo, bd)
    return t * lax.rsqrt(ss * (1.0 / ATT_HEAD_DIM) + EPS) * g


def _attn_kernel(sink_ref, cur_ref, prev_ref, gq_ref, gk_ref, go_ref, bdq_ref, bdk_ref, o_ref,
                 *, tiles_per_seq):
    i = pl.program_id(0)
    tq = cur_ref.shape[0]
    nblk = tq // ATT_BLOCK
    first = (i % tiles_per_seq) == 0

    q = cur_ref[:, 0:ATT_WIDTH]
    k = cur_ref[:, ATT_WIDTH:ATT_WIDTH + ATT_KV_WIDTH]
    v = cur_ref[:, ATT_WIDTH + ATT_KV_WIDTH:ATT_WIDTH + 2 * ATT_KV_WIDTH]
    kp = prev_ref[:, 0:ATT_KV_WIDTH]
    vp = prev_ref[:, ATT_KV_WIDTH:2 * ATT_KV_WIDTH]

    qn = (_head_rms(q, bdq_ref[...], gq_ref[...]) * (ATT_HEAD_DIM ** -0.5)).astype(BF16)
    kn = jnp.concatenate([_head_rms(kp, bdk_ref[...], gk_ref[...]),
                          _head_rms(k, bdk_ref[...], gk_ref[...])], axis=0)
    vf = jnp.concatenate([vp, v], axis=0)

    lane = lax.broadcasted_iota(jnp.int32, kn.shape, 1)
    left = lane < ATT_HEAD_DIM
    zero = jnp.zeros_like(kn)
    k0l = jnp.where(left, kn, zero)
    k1r = jnp.where(left, zero, kn)
    v0l = jnp.where(left, vf, zero)
    v1r = jnp.where(left, zero, vf)
    kl = (k0l.astype(BF16), pltpu.roll(k1r, ATT_HEAD_DIM, 1).astype(BF16))
    kr = (pltpu.roll(k0l, ATT_HEAD_DIM, 1).astype(BF16), k1r.astype(BF16))
    vl = (v0l.astype(BF16), pltpu.roll(v1r, ATT_HEAD_DIM, 1).astype(BF16))
    vr = (pltpu.roll(v0l, ATT_HEAD_DIM, 1).astype(BF16), v1r.astype(BF16))

    qi = lax.broadcasted_iota(jnp.int32, (ATT_BLOCK, 2 * ATT_BLOCK), 0)
    kj = lax.broadcasted_iota(jnp.int32, (ATT_BLOCK, 2 * ATT_BLOCK), 1)
    rel = qi + ATT_BLOCK - kj
    in_window = (rel >= 0) & (rel < ATT_BLOCK)
    first_key = jnp.where(first, ATT_BLOCK, 0)
    neg_inf = jnp.full((ATT_BLOCK, 2 * ATT_BLOCK), -jnp.inf, F32)

    def softmax_rows(s, sink):
        m = jnp.maximum(jnp.max(s, axis=-1, keepdims=True), sink)
        p = jnp.exp(s - m)
        den = jnp.sum(p, axis=-1, keepdims=True) + jnp.exp(sink - m)
        return (p * (1.0 / den)).astype(BF16)

    for j in range(nblk):
        rows = slice(j * ATT_BLOCK, (j + 1) * ATT_BLOCK)
        krows = slice(j * ATT_BLOCK, (j + 2) * ATT_BLOCK)
        if j == 0:
            mask = in_window & (kj >= first_key)
        else:
            mask = in_window
        pairs = []
        for c in range(ATT_KV_HEADS):
            qe = jnp.concatenate([qn[rows, (2 * c) * LANES:(2 * c + 1) * LANES],
                                  qn[rows, (2 * c + 1) * LANES:(2 * c + 2) * LANES]], axis=0)
            s_even = _dot_nt(qe, kl[c][krows])
            s_odd = _dot_nt(qe, kr[c][krows])
            for half in range(2):
                hr = slice(half * ATT_BLOCK, (half + 1) * ATT_BLOCK)
                h_even = 4 * c + 2 * half
                p_e = softmax_rows(jnp.where(mask, s_even[hr], neg_inf), sink_ref[h_even])
                p_o = softmax_rows(jnp.where(mask, s_odd[hr], neg_inf), sink_ref[h_even + 1])
                pairs.append(_dot(p_e, vl[c][krows]) + _dot(p_o, vr[c][krows]))
        att = jnp.concatenate(pairs, axis=1)
        ms = jnp.mean(att * att, axis=-1, keepdims=True)
        o_ref[rows, :] = (att * lax.rsqrt(ms + EPS) * go_ref[...]).astype(BF16)


def _attention(att_in, sinks, gq, gk, go, bdq, bdk, seq, tq):
    T = att_in.shape[0]
    tiles_per_seq = seq // tq
    blk_per_tile = tq // ATT_BLOCK
    kv_col_blk = ATT_WIDTH // (2 * ATT_KV_WIDTH)
    return pl.pallas_call(
        functools.partial(_attn_kernel, tiles_per_seq=tiles_per_seq),
        grid_spec=pltpu.PrefetchScalarGridSpec(
            num_scalar_prefetch=1,
            grid=(T // tq,),
            in_specs=[pl.BlockSpec((tq, 768), lambda i, s: (i, 0)),
                      pl.BlockSpec((ATT_BLOCK, 2 * ATT_KV_WIDTH),
                                   lambda i, s: (jnp.maximum(i * blk_per_tile - 1, 0), kv_col_blk)),
                      pl.BlockSpec((1, ATT_WIDTH), lambda i, s: (0, 0)),
                      pl.BlockSpec((1, ATT_KV_WIDTH), lambda i, s: (0, 0)),
                      pl.BlockSpec((1, ATT_WIDTH), lambda i, s: (0, 0)),
                      pl.BlockSpec((ATT_WIDTH, ATT_WIDTH), lambda i, s: (0, 0)),
                      pl.BlockSpec((ATT_KV_WIDTH, ATT_KV_WIDTH), lambda i, s: (0, 0))],
            out_specs=pl.BlockSpec((tq, ATT_WIDTH), lambda i, s: (i, 0)),
        ),
        out_shape=jax.ShapeDtypeStruct((T, ATT_WIDTH), BF16),
        compiler_params=pltpu.CompilerParams(dimension_semantics=("arbitrary",),
                                             vmem_limit_bytes=VMEM_LIMIT_BYTES),
        name="swa_attention",
    )(sinks, att_in, att_in, gq, gk, go, bdq, bdk)


def _unit_lower_inverses(m_lows, row, col):
    n = m_lows[0].shape[0]
    zero = jnp.zeros_like(m_lows[0])
    eye = jnp.where(row == col, jnp.ones_like(zero), zero)
    same = (row >> 3) == (col >> 3)
    nms = [jnp.where(same, -m, zero) for m in m_lows]
    ps = [eye + nm for nm in nms]
    n2s = [_dot3(nm, nm) for nm in nms]
    ps = [p + _dot3(p, n2) for p, n2 in zip(ps, n2s)]
    n4s = [_dot3(n2, n2) for n2 in n2s]
    ps = [p + _dot3(p, n4) for p, n4 in zip(ps, n4s)]
    shift = 3
    while (1 << shift) < n:
        same_next = (row >> (shift + 1)) == (col >> (shift + 1))
        lower_left = same_next & jnp.logical_not(same)
        ts = [_dot3(p, jnp.where(lower_left, m, zero)) for p, m in zip(ps, m_lows)]
        ps = [p - _dot3(t, p) for p, t in zip(ps, ts)]
        same = same_next
        shift += 1
    return ps


def _silu(v):
    return v * (1.0 / (1.0 + jnp.exp(-v)))


def _gdn_kernel(dn_ref, ab_ref, z_ref, cw_ref, nega_ref, dtb_ref, gn_ref, tri_ref, o_ref,
                state_ref, carry_ref):
    t = pl.program_id(1)
    C = GDN_CHUNK
    nb = dn_ref.shape[0]

    @pl.when(t == 0)
    def _():
        state_ref[...] = jnp.zeros_like(state_ref)
        carry_ref[...] = jnp.zeros_like(carry_ref)

    row = lax.broadcasted_iota(jnp.int32, (C, C), 0)
    col = lax.broadcasted_iota(jnp.int32, (C, C), 1)
    causal = row >= col
    strict = row > col
    zero_cc = jnp.zeros((C, C), F32)
    cw = cw_ref[...]

    ids, qs, ks, vbs, kbes, decs, egs, kdecs, cds = [], [], [], [], [], [], [], [], []
    for bb in range(nb):
        x = dn_ref[bb]
        xs = jnp.concatenate([carry_ref[bb], x], axis=0)
        carry_ref[bb] = x[C - SUBLANES:C, :]
        y = xs[5:5 + C] * cw[0:1, :]
        for jw in range(1, CONV_WIDTH):
            y = y + xs[5 + jw:5 + jw + C] * cw[jw:jw + 1, :]
        y = _silu(y)

        ab = ab_ref[bb]
        sp_in = ab[:, 0:LANES] + dtb_ref[...]
        softplus = jnp.maximum(sp_in, 0.0) + jnp.log1p(jnp.exp(-jnp.abs(sp_in)))
        g_all = nega_ref[...] * softplus
        beta_all = 1.0 / (1.0 + jnp.exp(-ab[:, LANES:2 * LANES]))
        gc_all = _dot3(tri_ref[...], g_all)
        gc_t = gc_all.T

        for h in range(DN_HEADS):
            qh = y[:, h * DN_HEAD_DIM:(h + 1) * DN_HEAD_DIM]
            kh = y[:, DN_WIDTH + h * DN_HEAD_DIM:DN_WIDTH + (h + 1) * DN_HEAD_DIM]
            vh = y[:, 2 * DN_WIDTH + h * DN_HEAD_DIM:2 * DN_WIDTH + (h + 1) * DN_HEAD_DIM]
            qh = qh * lax.rsqrt(jnp.sum(qh * qh, axis=-1, keepdims=True) + EPS) * (DN_HEAD_DIM ** -0.5)
            kh = kh * lax.rsqrt(jnp.sum(kh * kh, axis=-1, keepdims=True) + EPS)
            gcol = gc_all[:, h:h + 1]
            grow = gc_t[h:h + 1, :]
            bcol = beta_all[:, h:h + 1]
            glast = gc_all[C - 1:C, h:h + 1]
            eg = jnp.exp(gcol)
            kb = kh * bcol
            ids.append((bb, h))
            qs.append(qh)
            ks.append(kh.astype(BF16))
            vbs.append(jnp.concatenate([vh * bcol, kb * eg], axis=1).astype(BF16))
            kbes.append(kb.astype(BF16))
            decs.append(jnp.where(causal, jnp.exp(jnp.where(causal, gcol - grow, zero_cc)), zero_cc))
            egs.append(eg)
            kdecs.append((kh * jnp.exp(glast - gcol)).T.astype(BF16))
            cds.append(jnp.exp(glast))

    n = len(ids)
    m_lows = [jnp.where(strict, _dot_nt(kbes[i], ks[i]) * decs[i], zero_cc) for i in range(n)]
    qks = [jnp.where(causal, _dot_nt(qs[i].astype(BF16), ks[i]) * decs[i], zero_cc).astype(BF16)
           for i in range(n)]
    tinvs = _unit_lower_inverses(m_lows, row, col)
    uws = [_dot(tinvs[i].astype(BF16), vbs[i]) for i in range(n)]
    s_olds = [state_ref[bb, h] for (bb, h) in ids]
    s16s = [s.astype(BF16) for s in s_olds]
    v16s = [(uws[i][:, 0:DN_HEAD_DIM] - _dot(uws[i][:, DN_HEAD_DIM:].astype(BF16), s16s[i])).astype(BF16)
            for i in range(n)]
    outs = [_dot((qs[i] * egs[i]).astype(BF16), s16s[i]) + _dot(qks[i], v16s[i]) for i in range(n)]
    for i, (bb, h) in enumerate(ids):
        state_ref[bb, h] = s_olds[i] * cds[i] + _dot(kdecs[i], v16s[i])

    for i, (bb, h) in enumerate(ids):
        hs = slice(h * DN_HEAD_DIM, (h + 1) * DN_HEAD_DIM)
        o = outs[i]
        ms = jnp.mean(o * o, axis=-1, keepdims=True)
        o_ref[bb, :, hs] = (o * lax.rsqrt(ms + EPS) * gn_ref[...] * _silu(z_ref[bb, :, hs])).astype(BF16)


def _gdn(dn_in, ab, z, conv_w, neg_a, dt_bias, gnorm, tri, batch, seq, nb):
    C = GDN_CHUNK
    nc = seq // C
    blk3 = lambda w: pl.BlockSpec((nb, C, w), lambda bi, t: (bi, t, 0))
    const = lambda shape: pl.BlockSpec(shape, lambda bi, t: (0, 0))
    out = pl.pallas_call(
        _gdn_kernel,
        grid=(batch // nb, nc),
        in_specs=[blk3(3 * DN_WIDTH), blk3(2 * LANES), blk3(DN_WIDTH),
                  const((CONV_WIDTH, 3 * DN_WIDTH)), const((1, LANES)), const((1, LANES)),
                  const((1, DN_HEAD_DIM)), const((C, C))],
        out_specs=blk3(DN_WIDTH),
        out_shape=jax.ShapeDtypeStruct((batch, seq, DN_WIDTH), BF16),
        scratch_shapes=[pltpu.VMEM((nb, DN_HEADS, DN_HEAD_DIM, DN_HEAD_DIM), F32),
                        pltpu.VMEM((nb, SUBLANES, 3 * DN_WIDTH), F32)],
        compiler_params=pltpu.CompilerParams(dimension_semantics=("arbitrary", "arbitrary"),
                                             vmem_limit_bytes=VMEM_LIMIT_BYTES),
        name="gated_deltanet",
    )(dn_in.reshape(batch, seq, -1), ab.reshape(batch, seq, -1), z.reshape(batch, seq, -1),
      conv_w, neg_a, dt_bias, gnorm, tri)
    return out.reshape(batch * seq, DN_WIDTH)


def _outproj_kernel(x_ref, att_ref, dn_ref, wo_ref, g_ref, h_ref, xn_ref):
    h = x_ref[...] + (_dot(att_ref[...], wo_ref[0:ATT_WIDTH, :])
                      + _dot(dn_ref[...], wo_ref[ATT_WIDTH:ATT_WIDTH + DN_WIDTH, :]))
    h_ref[...] = h
    ms = jnp.mean(h * h, axis=-1, keepdims=True)
    xn_ref[...] = (h * lax.rsqrt(ms + EPS) * g_ref[...]).astype(BF16)


def _outproj(x2, att_n, dn_o, w_out, g, tm):
    T = x2.shape[0]
    return pl.pallas_call(
        _outproj_kernel,
        grid=(T // tm,),
        in_specs=[pl.BlockSpec((tm, D_MODEL), lambda i: (i, 0)),
                  pl.BlockSpec((tm, ATT_WIDTH), lambda i: (i, 0)),
                  pl.BlockSpec((tm, DN_WIDTH), lambda i: (i, 0)),
                  pl.BlockSpec((ATT_WIDTH + DN_WIDTH, D_MODEL), lambda i: (0, 0)),
                  pl.BlockSpec((1, D_MODEL), lambda i: (0, 0))],
        out_specs=[pl.BlockSpec((tm, D_MODEL), lambda i: (i, 0)),
                   pl.BlockSpec((tm, D_MODEL), lambda i: (i, 0))],
        out_shape=[jax.ShapeDtypeStruct((T, D_MODEL), F32),
                   jax.ShapeDtypeStruct((T, D_MODEL), BF16)],
        compiler_params=pltpu.CompilerParams(dimension_semantics=("arbitrary",),
                                             vmem_limit_bytes=VMEM_LIMIT_BYTES),
        name="outproj",
    )(x2, att_n, dn_o, w_out, g)


def _oddeven_merge(lo, hi, r):
    step = r * 2
    if step < hi - lo:
        yield from _oddeven_merge(lo, hi, step)
        yield from _oddeven_merge(lo + r, hi, step)
        for i in range(lo + r, hi - r, step):
            yield (i, i + r)
    else:
        yield (lo, lo + r)


def _oddeven_sort_net(lo, hi):
    if hi - lo >= 1:
        mid = lo + (hi - lo) // 2
        yield from _oddeven_sort_net(lo, mid)
        yield from _oddeven_sort_net(mid + 1, hi)
        yield from _oddeven_merge(lo, hi, 1)


_SORT16 = tuple(_oddeven_sort_net(0, PEER_TOPK - 1))
_BITONIC16 = tuple((i, i + d) for d in (8, 4, 2, 1) for i in range(PEER_TOPK) if (i & d) == 0)
_STAIR = tuple((r, q) for r in range(PEER_TOPK) for q in range(PEER_TOPK) if (r + 1) * (q + 1) <= PEER_TOPK)


def _ce_vi(a, b):
    va, ia = a
    vb, ib = b
    a_first = (va > vb) | ((va == vb) & (ia < ib))
    return ((jnp.where(a_first, va, vb), jnp.where(a_first, ia, ib)),
            (jnp.where(a_first, vb, va), jnp.where(a_first, ib, ia)))


def _first_vi(a, b):
    va, ia = a
    vb, ib = b
    a_first = (va > vb) | ((va == vb) & (ia < ib))
    return (jnp.where(a_first, va, vb), jnp.where(a_first, ia, ib))


def _sort16_vi(items):
    items = list(items)
    for i, j in _SORT16:
        items[i], items[j] = _ce_vi(items[i], items[j])
    return items


def _merge_top16_vi(a, b):
    n = PEER_TOPK
    items = [_first_vi(a[i], b[n - 1 - i]) for i in range(n)]
    for i, j in _BITONIC16:
        items[i], items[j] = _ce_vi(items[i], items[j])
    return items


def _sort16_v(vals):
    vals = list(vals)
    for i, j in _SORT16:
        hi = jnp.maximum(vals[i], vals[j])
        lo = jnp.minimum(vals[i], vals[j])
        vals[i], vals[j] = hi, lo
    return vals


def _merge_top16_v(a, b):
    n = PEER_TOPK
    vals = [jnp.maximum(a[i], b[n - 1 - i]) for i in range(n)]
    for i, j in _BITONIC16:
        hi = jnp.maximum(vals[i], vals[j])
        lo = jnp.minimum(vals[i], vals[j])
        vals[i], vals[j] = hi, lo
    return vals


def _top16_vi_of_128(load_key):
    groups = []
    for gi in range(N_KEYS // PEER_TOPK):
        items = [(load_key(gi * PEER_TOPK + t), jnp.full((SUBLANES, LANES), float(gi * PEER_TOPK + t), F32))
                 for t in range(PEER_TOPK)]
        groups.append(_sort16_vi(items))
    while len(groups) > 1:
        groups = [_merge_top16_vi(groups[2 * t], groups[2 * t + 1]) for t in range(len(groups) // 2)]
    return groups[0]


def _top16_v_of_128(load_key):
    groups = [_sort16_v([load_key(gi * PEER_TOPK + t) for t in range(PEER_TOPK)])
              for gi in range(N_KEYS // PEER_TOPK)]
    while len(groups) > 1:
        groups = [_merge_top16_v(groups[2 * t], groups[2 * t + 1]) for t in range(len(groups) // 2)]
    return groups[0]


def _top16_is_distinct(load_key, vals):
    zero = jnp.zeros((SUBLANES, LANES), F32)
    one = jnp.ones((SUBLANES, LANES), F32)
    n_ge = zero
    for k in range(N_KEYS):
        n_ge = n_ge + jnp.where(load_key(k) >= vals[PEER_TOPK - 1], one, zero)
    ok = n_ge == float(PEER_TOPK)
    for r in range(PEER_TOPK - 1):
        ok = ok & (vals[r] > vals[r + 1])
    return jnp.where(ok, one, zero)


def _select_pairs(av, bv):
    zero = jnp.zeros((SUBLANES, LANES), F32)
    one = jnp.ones((SUBLANES, LANES), F32)
    cand = {(r, q): av[r] + bv[q] for (r, q) in _STAIR}
    row0 = [cand[(0, q)] for q in range(PEER_TOPK)]
    rest = [cand[rq] for rq in _STAIR if rq[0] > 0]
    neg_inf = jnp.full((SUBLANES, LANES), -jnp.inf, F32)
    best = row0
    for s in range(0, len(rest), PEER_TOPK):
        grp = rest[s:s + PEER_TOPK]
        grp = grp + [neg_inf] * (PEER_TOPK - len(grp))
        best = _merge_top16_v(best, _sort16_v(grp))
    thr = best[PEER_TOPK - 1]
    zsum = one
    for jj in range(1, PEER_TOPK):
        zsum = zsum + jnp.exp(best[jj] - best[0])
    inv_z_half = 0.5 / zsum

    n_gt = zero
    for rq in _STAIR:
        n_gt = n_gt + jnp.where(cand[rq] > thr, one, zero)
    need = float(PEER_TOPK) - n_gt
    cnt = zero
    n_row = [zero] * PEER_TOPK
    for (r, q) in _STAIR:
        c = cand[(r, q)]
        eq = c == thr
        take = (c > thr) | (eq & (cnt < need))
        cnt = cnt + jnp.where(eq, one, zero)
        n_row[r] = n_row[r] + jnp.where(take, one, zero)
    return n_row, inv_z_half


def _match_count(keys, probe, n_row):
    n_k = jnp.zeros((SUBLANES, LANES), F32)
    for r in reversed(range(PEER_TOPK)):
        n_k = jnp.where(keys[r] == probe, n_row[r], n_k)
    return n_k


def _match_rank(probe, rows):
    rk = jnp.full(probe.shape, float(PEER_TOPK), F32)
    for q in reversed(range(PEER_TOPK)):
        rk = jnp.where(rows[q] == probe, jnp.full_like(rk, float(q)), rk)
    return rk


def _peer_topk_kernel(xn_ref, wqt_ref, k0_ref, k1_ref, r1_ref, e1_ref, n_ref, c_ref, s0_ref, s1_ref):
    tb = xn_ref.shape[0]
    half_w = PEER_HEADS * PEER_HALF
    qt = _dot_nt(wqt_ref[...], xn_ref[...]).astype(BF16)
    s0 = _dot(k0_ref[...], qt[0:half_w])
    s1 = _dot(k1_ref[...], qt[half_w:2 * half_w])
    for g in range(tb // LANES):
        s0_ref[g] = s0[:, g * LANES:(g + 1) * LANES]
        s1_ref[g] = s1[:, g * LANES:(g + 1) * LANES]

    sub_iota = lax.broadcasted_iota(jnp.int32, (_ROWS16, LANES), 0).astype(F32)

    def slab(gi, carry):
        ls = pl.ds(pl.multiple_of(gi * LANES, LANES), LANES)
        load0 = lambda k: s0_ref[gi, pl.ds(k * SUBLANES, SUBLANES), :]
        load1 = lambda k: s1_ref[gi, pl.ds(k * SUBLANES, SUBLANES), :]
        load1_head = lambda h, kb: s1_ref[gi, pl.ds(kb * _ROWS16 * SUBLANES + h, _ROWS16, stride=SUBLANES), :]

        av = _top16_v_of_128(load0)
        bv = _top16_v_of_128(load1)
        n_row, inv_z_half = _select_pairs(av, bv)
        distinct = _top16_is_distinct(load0, av) * _top16_is_distinct(load1, bv)
        has_tie = jnp.min(distinct) < 0.5

        def emit(keys0, probe0, keys1, probe1):
            for k in range(N_KEYS):
                rs = pl.ds(k * SUBLANES, SUBLANES)
                s0k = load0(k)
                n_ref[rs, ls] = _match_count(keys0, probe0(k, s0k), n_row)
                c_ref[rs, ls] = jnp.exp(s0k - av[0]) * inv_z_half
            for h in range(PEER_HEADS):
                rows = [jnp.broadcast_to(keys1[q][h:h + 1, :], (_ROWS16, LANES)) for q in range(PEER_TOPK)]
                b0h = jnp.broadcast_to(bv[0][h:h + 1, :], (_ROWS16, LANES))
                for kb in range(N_KEYS // _ROWS16):
                    rs = pl.ds(h * N_KEYS + kb * _ROWS16, _ROWS16)
                    tile = load1_head(h, kb)
                    r1_ref[rs, ls] = _match_rank(probe1(kb, tile), rows).astype(BF16)
                    e1_ref[rs, ls] = jnp.exp(tile - b0h).astype(BF16)

        @pl.when(jnp.logical_not(has_tie))
        def _():
            emit(av, lambda k, s0k: s0k, bv, lambda kb, tile: tile)

        @pl.when(has_tie)
        def _():
            a = _top16_vi_of_128(load0)
            b = _top16_vi_of_128(load1)
            emit([it[1] for it in a], lambda k, s0k: float(k),
                 [it[1] for it in b], lambda kb, tile: sub_iota + float(kb * _ROWS16))

        return carry

    lax.fori_loop(0, tb // LANES, slab, 0)


def _peer_topk(xn2, wqt, k0, k1, tb):
    T = xn2.shape[0]
    rows = PEER_HEADS * N_KEYS
    kspec = pl.BlockSpec((rows, PEER_HEADS * PEER_HALF), lambda i: (0, 0))
    ospec = pl.BlockSpec((rows, tb), lambda i: (0, i))
    return pl.pallas_call(
        _peer_topk_kernel,
        grid=(T // tb,),
        in_specs=[pl.BlockSpec((tb, D_MODEL), lambda i: (i, 0)),
                  pl.BlockSpec((D_MODEL, D_MODEL), lambda i: (0, 0)),
                  kspec, kspec],
        out_specs=[ospec, ospec, ospec, ospec],
        out_shape=[jax.ShapeDtypeStruct((rows, T), BF16),
                   jax.ShapeDtypeStruct((rows, T), BF16),
                   jax.ShapeDtypeStruct((rows, T), F32),
                   jax.ShapeDtypeStruct((rows, T), F32)],
        scratch_shapes=[pltpu.VMEM((tb // LANES, rows, LANES), F32),
                        pltpu.VMEM((tb // LANES, rows, LANES), F32)],
        compiler_params=pltpu.CompilerParams(dimension_semantics=("arbitrary",),
                                             vmem_limit_bytes=VMEM_LIMIT_BYTES),
        name="peer_topk",
    )(xn2, wqt, k0, k1)


_SUB_E = 512
_W_TILE = 256


def _peer_dense_kernel(xn_ref, h_ref, r1_ref, e1_ref, n_ref, c_ref, u_next_ref, u_first_ref,
                       vt_prev_ref, vt_last_ref, o_ref, acc_ref, hm_ref, at_ref):
    c = pl.program_id(1)
    tb = xn_ref.shape[0]
    n_sub = vt_prev_ref.shape[0]
    i0_per_sub = _SUB_E // N_KEYS
    xn = xn_ref[...]

    @pl.when(c == 0)
    def _():
        acc_ref[...] = jnp.zeros_like(acc_ref)
        at_ref[0] = _dot_nt(u_first_ref[...], xn)

    tw = min(tb, _W_TILE)
    zero16 = jnp.zeros((_ROWS16, tw), BF16)
    n_j = N_KEYS // _ROWS16

    def gated_acts(sub, slot):
        for ii in range(i0_per_sub):
            i0 = (c * n_sub + sub) * i0_per_sub + ii
            for lt in range(tb // tw):
                ls = slice(lt * tw, (lt + 1) * tw)
                wacc = [None] * n_j
                for h in range(PEER_HEADS):
                    rowi = pl.ds(i0 * PEER_HEADS + h, 1)
                    nb = jnp.broadcast_to(n_ref[rowi, ls], (_ROWS16, tw)).astype(BF16)
                    cb = jnp.broadcast_to(c_ref[rowi, ls], (_ROWS16, tw)).astype(BF16)
                    for j in range(n_j):
                        rs = slice(h * N_KEYS + j * _ROWS16, h * N_KEYS + (j + 1) * _ROWS16)
                        term = jnp.where(r1_ref[rs, ls] < nb, e1_ref[rs, ls] * cb, zero16)
                        wacc[j] = term if wacc[j] is None else wacc[j] + term
                for j in range(n_j):
                    er = slice(ii * N_KEYS + j * _ROWS16, ii * N_KEYS + (j + 1) * _ROWS16)
                    a16 = at_ref[slot, er, ls]
                    gl = a16 * (1.0 + lax.erf(a16 * 0.7071067811865476))
                    hm_ref[slot, er, ls] = gl.astype(BF16) * wacc[j]

    for sub in range(n_sub):
        par = sub % 2
        at_ref[1 - par] = _dot_nt(u_next_ref[sub * _SUB_E:(sub + 1) * _SUB_E, :], xn)
        if sub > 0:
            acc_ref[...] += _dot(vt_prev_ref[sub - 1], hm_ref[1 - par])
        gated_acts(sub, par)
    acc_ref[...] += _dot(vt_prev_ref[n_sub - 1], hm_ref[(n_sub - 1) % 2])

    @pl.when(c == pl.num_programs(1) - 1)
    def _():
        o_ref[...] = h_ref[...] + acc_ref[...].T


def _peer_dense(xn2, h, r1, e1, n_sel, c_gate, u16, vt_slabs, tb, ec):
    T = xn2.shape[0]
    rows = PEER_HEADS * N_KEYS
    n_sub = ec // _SUB_E
    assert n_sub & (n_sub - 1) == 0
    last_slab = N_EXPERTS // _SUB_E - 1
    tspec = pl.BlockSpec((rows, tb), lambda i, c: (0, i))
    u_next = jnp.roll(u16, -_SUB_E, axis=0)
    vt_prev = vt_slabs
    return pl.pallas_call(
        _peer_dense_kernel,
        grid=(T // tb, N_EXPERTS // ec),
        in_specs=[pl.BlockSpec((tb, D_MODEL), lambda i, c: (i, 0)),
                  pl.BlockSpec((tb, D_MODEL), lambda i, c: (i, 0)),
                  tspec, tspec, tspec, tspec,
                  pl.BlockSpec((ec, D_MODEL), lambda i, c: (c, 0)),
                  pl.BlockSpec((_SUB_E, D_MODEL), lambda i, c: (0, 0)),
                  pl.BlockSpec((n_sub, D_MODEL, _SUB_E), lambda i, c: (c, 0, 0)),
                  pl.BlockSpec((1, D_MODEL, _SUB_E), lambda i, c: (last_slab, 0, 0))],
        out_specs=pl.BlockSpec((tb, D_MODEL), lambda i, c: (i, 0)),
        out_shape=jax.ShapeDtypeStruct((T, D_MODEL), F32),
        scratch_shapes=[pltpu.VMEM((D_MODEL, tb), F32),
                        pltpu.VMEM((2, _SUB_E, tb), BF16),
                        pltpu.VMEM((2, _SUB_E, tb), F32)],
        compiler_params=pltpu.CompilerParams(dimension_semantics=("arbitrary", "arbitrary"),
                                             vmem_limit_bytes=VMEM_LIMIT_BYTES),
        name="peer_dense",
    )(xn2, h, r1, e1, n_sel, c_gate, u_next, u16, vt_prev, vt_slabs)


def _block_diag_ones(width, blk):
    idx = np.arange(width) // blk
    return jnp.asarray((idx[:, None] == idx[None, :]).astype(np.float32), dtype=BF16)


def _layer(x, norm_mix_g, w_in, att_q_norm_g, att_k_norm_g, att_sinks, att_out_norm_g, dn_conv_w,
           dn_a_log, dn_dt_bias, dn_out_norm_g, w_out, norm_ffn_g, peer_w_q, peer_sub_keys, peer_u, peer_v):
    B, S, _ = x.shape
    T = B * S
    x2 = x.reshape(T, D_MODEL)

    o_q, o_k, o_v = 0, ATT_WIDTH, ATT_WIDTH + ATT_KV_WIDTH
    o_dn = ATT_WIDTH + 2 * ATT_KV_WIDTH
    o_a = o_dn + 3 * DN_WIDTH
    o_b = o_a + DN_HEADS
    o_z = o_b + DN_HEADS
    pad = jnp.zeros((D_MODEL, LANES - DN_HEADS), w_in.dtype)
    w_all = jnp.concatenate([w_in[:, o_q:o_dn], w_in[:, o_dn:o_a], w_in[:, o_z:o_z + DN_WIDTH],
                             w_in[:, o_a:o_b], pad, w_in[:, o_b:o_z], pad], axis=1).astype(BF16)
    del o_k, o_v

    def lane_pad(vec):
        return jnp.concatenate([vec.astype(F32), jnp.zeros((LANES - vec.shape[0],), F32)]).reshape(1, LANES)

    att_in, dn_in, z_in, ab_in = _inproj(x2, norm_mix_g.reshape(1, D_MODEL), w_all, tm=512)

    att_n = _attention(
        att_in, att_sinks.astype(F32),
        jnp.tile(att_q_norm_g.astype(F32), ATT_HEADS).reshape(1, ATT_WIDTH),
        jnp.tile(att_k_norm_g.astype(F32), ATT_KV_HEADS).reshape(1, ATT_KV_WIDTH),
        att_out_norm_g.astype(F32).reshape(1, ATT_WIDTH),
        _block_diag_ones(ATT_WIDTH, ATT_HEAD_DIM), _block_diag_ones(ATT_KV_WIDTH, ATT_HEAD_DIM),
        seq=S, tq=512)

    tri = jnp.asarray(np.tril(np.ones((GDN_CHUNK, GDN_CHUNK), np.float32)))
    dn_o = _gdn(dn_in, ab_in, z_in, dn_conv_w.astype(F32),
                lane_pad(-jnp.exp(dn_a_log.astype(F32))), lane_pad(dn_dt_bias),
                dn_out_norm_g.astype(F32).reshape(1, DN_HEAD_DIM), tri, batch=B, seq=S, nb=2)

    h, xn2 = _outproj(x2, att_n, dn_o, w_out.astype(BF16), norm_ffn_g.reshape(1, D_MODEL), tm=512)

    wqt = (peer_w_q.T.reshape(PEER_HEADS, 2, PEER_HALF, D_MODEL).transpose(1, 0, 2, 3)
           .reshape(D_MODEL, D_MODEL).astype(BF16))
    eye = jnp.eye(PEER_HEADS, dtype=peer_sub_keys.dtype)
    kd = jnp.einsum('hpkc,hg->pkhgc', peer_sub_keys, eye)
    k_km = kd.reshape(2, N_KEYS * PEER_HEADS, PEER_HEADS * PEER_HALF).astype(BF16)
    r1, e1, n_sel, c_gate = _peer_topk(xn2, wqt, k_km[0], k_km[1], tb=512)

    vt_slabs = (peer_v.reshape(N_EXPERTS // _SUB_E, _SUB_E, D_MODEL).transpose(0, 2, 1).astype(BF16))
    out = _peer_dense(xn2, h, r1, e1, n_sel, c_gate, peer_u.astype(BF16), vt_slabs, tb=512, ec=2048)
    return out.reshape(B, S, D_MODEL)


def kernel(x, norm_mix_g, w_in, att_q_norm_g, att_k_norm_g, att_sinks, att_out_norm_g, dn_conv_w, dn_a_log, dn_dt_bias, dn_out_norm_g, w_out, norm_ffn_g, peer_w_q, peer_sub_keys, peer_u, peer_v):
    h = x
    for l in range(norm_mix_g.shape[0]):
        h = _layer(h, norm_mix_g[l], w_in[l], att_q_norm_g[l], att_k_norm_g[l], att_sinks[l],
                   att_out_norm_g[l], dn_conv_w[l], dn_a_log[l], dn_dt_bias[l], dn_out_norm_g[l],
                   w_out[l], norm_ffn_g[l], peer_w_q[l], peer_sub_keys[l], peer_u[l], peer_v[l])
    return h
```

```python
import functools

import numpy as np
import jax
import jax.numpy as jnp
from jax import lax
from jax.experimental import pallas as pl
from jax.experimental.pallas import tpu as pltpu

F32 = jnp.float32
BF16 = jnp.bfloat16

D_MODEL = 1024
ATT_HEADS = 8
ATT_KV_HEADS = 2
ATT_HEAD_DIM = 64
ATT_WIDTH = 512
ATT_KV_WIDTH = 128
ATT_BLOCK = 128
DN_HEADS = 4
DN_HEAD_DIM = 128
DN_WIDTH = 512
CONV_WIDTH = 4
PEER_HEADS = 8
N_KEYS = 128
N_EXPERTS = N_KEYS * N_KEYS
PEER_HALF = 64
PEER_TOPK = 16
EPS = 1e-6

LANES = 128
SUBLANES = 8
VMEM_LIMIT_BYTES = 56 * 1024 * 1024

_C_ATT = 0
_C_DN = 768
_C_Z = _C_DN + 3 * DN_WIDTH
_C_AB = _C_Z + DN_WIDTH
_C_END = _C_AB + 2 * LANES

GDN_CHUNK = 128
_ROWS16 = 2 * SUBLANES


def _dot(a, b):
    return jnp.dot(a, b, preferred_element_type=F32)


def _dot_nt(a, b):
    return lax.dot_general(a, b, (((1,), (1,)), ((), ())), preferred_element_type=F32)


def _split2(a):
    hi = a.astype(BF16)
    lo = (a - hi.astype(F32)).astype(BF16)
    return hi, lo


def _dot3s(a_split, b_split):
    ah, al = a_split
    bh, bl = b_split
    return _dot(ah, bh) + (_dot(ah, bl) + _dot(al, bh))


def _dot3(a, b):
    return _dot3s(_split2(a), _split2(b))


def _inproj_kernel(x_ref, g_ref, w_ref, att_ref, dn_ref, z_ref, ab_ref):
    x = x_ref[...]
    ms = jnp.mean(x * x, axis=-1, keepdims=True)
    xn = (x * lax.rsqrt(ms + EPS) * g_ref[...]).astype(BF16)
    att_ref[...] = _dot(xn, w_ref[:, _C_ATT:_C_DN])
    dn_ref[...] = _dot(xn, w_ref[:, _C_DN:_C_Z])
    z_ref[...] = _dot(xn, w_ref[:, _C_Z:_C_AB])
    ab_ref[...] = _dot(xn, w_ref[:, _C_AB:_C_END])


def _inproj(x2, g, w_all, tm):
    T = x2.shape[0]
    return pl.pallas_call(
        _inproj_kernel,
        grid=(T // tm,),
        in_specs=[pl.BlockSpec((tm, D_MODEL), lambda i: (i, 0)),
                  pl.BlockSpec((1, D_MODEL), lambda i: (0, 0)),
                  pl.BlockSpec((D_MODEL, _C_END), lambda i: (0, 0))],
        out_specs=[pl.BlockSpec((tm, 768), lambda i: (i, 0)),
                   pl.BlockSpec((tm, 3 * DN_WIDTH), lambda i: (i, 0)),
                   pl.BlockSpec((tm, DN_WIDTH), lambda i: (i, 0)),
                   pl.BlockSpec((tm, 2 * LANES), lambda i: (i, 0))],
        out_shape=[jax.ShapeDtypeStruct((T, 768), F32),
                   jax.ShapeDtypeStruct((T, 3 * DN_WIDTH), F32),
                   jax.ShapeDtypeStruct((T, DN_WIDTH), F32),
                   jax.ShapeDtypeStruct((T, 2 * LANES), F32)],
        compiler_params=pltpu.CompilerParams(dimension_semantics=("arbitrary",),
                                             vmem_limit_bytes=VMEM_LIMIT_BYTES),
        name="inproj",
    )(x2, g, w_all)


def _head_rms(t, bd, g):
    hi, lo = _split2(t * t)
    ss = _dot(hi, bd) + _dot(lo, bd)
    return t * lax.rsqrt(ss * (1.0 / ATT_HEAD_DIM) + EPS) * g


def _attn_kernel(sink_ref, cur_ref, prev_ref, gq_ref, gk_ref, go_ref, bdq_ref, bdk_ref, o_ref,
                 *, tiles_per_seq):
    i = pl.program_id(0)
    tq = cur_ref.shape[0]
    nblk = tq // ATT_BLOCK
    first = (i % tiles_per_seq) == 0

    q = cur_ref[:, 0:ATT_WIDTH]
    k = cur_ref[:, ATT_WIDTH:ATT_WIDTH + ATT_KV_WIDTH]
    v = cur_ref[:, ATT_WIDTH + ATT_KV_WIDTH:ATT_WIDTH + 2 * ATT_KV_WIDTH]
    kp = prev_ref[:, 0:ATT_KV_WIDTH]
    vp = prev_ref[:, ATT_KV_WIDTH:2 * ATT_KV_WIDTH]

    qn = (_head_rms(q, bdq_ref[...], gq_ref[...]) * (ATT_HEAD_DIM ** -0.5)).astype(BF16)
    kn = jnp.concatenate([_head_rms(kp, bdk_ref[...], gk_ref[...]),
                          _head_rms(k, bdk_ref[...], gk_ref[...])], axis=0)
    vf = jnp.concatenate([vp, v], axis=0)

    lane = lax.broadcasted_iota(jnp.int32, kn.shape, 1)
    left = lane < ATT_HEAD_DIM
    zero = jnp.zeros_like(kn)
    k0l = jnp.where(left, kn, zero)
    k1r = jnp.where(left, zero, kn)
    v0l = jnp.where(left, vf, zero)
    v1r = jnp.where(left, zero, vf)
    kl = (k0l.astype(BF16), pltpu.roll(k1r, ATT_HEAD_DIM, 1).astype(BF16))
    kr = (pltpu.roll(k0l, ATT_HEAD_DIM, 1).astype(BF16), k1r.astype(BF16))
    vl = (v0l.astype(BF16), pltpu.roll(v1r, ATT_HEAD_DIM, 1).astype(BF16))
    vr = (pltpu.roll(v0l, ATT_HEAD_DIM, 1).astype(BF16), v1r.astype(BF16))

    qi = lax.broadcasted_iota(jnp.int32, (ATT_BLOCK, 2 * ATT_BLOCK), 0)
    kj = lax.broadcasted_iota(jnp.int32, (ATT_BLOCK, 2 * ATT_BLOCK), 1)
    rel = qi + ATT_BLOCK - kj
    in_window = (rel >= 0) & (rel < ATT_BLOCK)
    first_key = jnp.where(first, ATT_BLOCK, 0)
    neg_inf = jnp.full((ATT_BLOCK, 2 * ATT_BLOCK), -jnp.inf, F32)

    def softmax_rows(s, sink):
        m = jnp.maximum(jnp.max(s, axis=-1, keepdims=True), sink)
        p = jnp.exp(s - m)
        den = jnp.sum(p, axis=-1, keepdims=True) + jnp.exp(sink - m)
        return (p * (1.0 / den)).astype(BF16)

    for j in range(nblk):
        rows = slice(j * ATT_BLOCK, (j + 1) * ATT_BLOCK)
        krows = slice(j * ATT_BLOCK, (j + 2) * ATT_BLOCK)
        if j == 0:
            mask = in_window & (kj >= first_key)
        else:
            mask = in_window
        pairs = []
        for c in range(ATT_KV_HEADS):
            qe = jnp.concatenate([qn[rows, (2 * c) * LANES:(2 * c + 1) * LANES],
                                  qn[rows, (2 * c + 1) * LANES:(2 * c + 2) * LANES]], axis=0)
            s_even = _dot_nt(qe, kl[c][krows])
            s_odd = _dot_nt(qe, kr[c][krows])
            for half in range(2):
                hr = slice(half * ATT_BLOCK, (half + 1) * ATT_BLOCK)
                h_even = 4 * c + 2 * half
                p_e = softmax_rows(jnp.where(mask, s_even[hr], neg_inf), sink_ref[h_even])
                p_o = softmax_rows(jnp.where(mask, s_odd[hr], neg_inf), sink_ref[h_even + 1])
                pairs.append(_dot(p_e, vl[c][krows]) + _dot(p_o, vr[c][krows]))
        att = jnp.concatenate(pairs, axis=1)
        ms = jnp.mean(att * att, axis=-1, keepdims=True)
        o_ref[rows, :] = (att * lax.rsqrt(ms + EPS) * go_ref[...]).astype(BF16)


def _attention(att_in, sinks, gq, gk, go, bdq, bdk, seq, tq):
    T = att_in.shape[0]
    tiles_per_seq = seq // tq
    blk_per_tile = tq // ATT_BLOCK
    kv_col_blk = ATT_WIDTH // (2 * ATT_KV_WIDTH)
    return pl.pallas_call(
        functools.partial(_attn_kernel, tiles_per_seq=tiles_per_seq),
        grid_spec=pltpu.PrefetchScalarGridSpec(
            num_scalar_prefetch=1,
            grid=(T // tq,),
            in_specs=[pl.BlockSpec((tq, 768), lambda i, s: (i, 0)),
                      pl.BlockSpec((ATT_BLOCK, 2 * ATT_KV_WIDTH),
                                   lambda i, s: (jnp.maximum(i * blk_per_tile - 1, 0), kv_col_blk)),
                      pl.BlockSpec((1, ATT_WIDTH), lambda i, s: (0, 0)),
                      pl.BlockSpec((1, ATT_KV_WIDTH), lambda i, s: (0, 0)),
                      pl.BlockSpec((1, ATT_WIDTH), lambda i, s: (0, 0)),
                      pl.BlockSpec((ATT_WIDTH, ATT_WIDTH), lambda i, s: (0, 0)),
                      pl.BlockSpec((ATT_KV_WIDTH, ATT_KV_WIDTH), lambda i, s: (0, 0))],
            out_specs=pl.BlockSpec((tq, ATT_WIDTH), lambda i, s: (i, 0)),
        ),
        out_shape=jax.ShapeDtypeStruct((T, ATT_WIDTH), BF16),
        compiler_params=pltpu.CompilerParams(dimension_semantics=("arbitrary",),
                                             vmem_limit_bytes=VMEM_LIMIT_BYTES),
        name="swa_attention",
    )(sinks, att_in, att_in, gq, gk, go, bdq, bdk)


def _unit_lower_inverses(m_lows, row, col):
    n = m_lows[0].shape[0]
    zero = jnp.zeros_like(m_lows[0])
    zero16 = jnp.zeros(zero.shape, BF16)
    eye = jnp.where(row == col, jnp.ones_like(zero), zero)
    same = (row >> 3) == (col >> 3)
    m_sp = [_split2(m) for m in m_lows]
    nm_sp = [(jnp.where(same, -mh, zero16), jnp.where(same, -ml, zero16)) for mh, ml in m_sp]
    ps = [eye + jnp.where(same, -m, zero) for m in m_lows]
    n2_sp = [_split2(_dot3s(nm, nm)) for nm in nm_sp]
    ps = [p + _dot3s(_split2(p), n2) for p, n2 in zip(ps, n2_sp)]
    n4_sp = [_split2(_dot3s(n2, n2)) for n2 in n2_sp]
    ps = [p + _dot3s(_split2(p), n4) for p, n4 in zip(ps, n4_sp)]
    shift = 3
    while (1 << shift) < n:
        same_next = (row >> (shift + 1)) == (col >> (shift + 1))
        lower_left = same_next & jnp.logical_not(same)
        p_sp = [_split2(p) for p in ps]
        ts = [_dot3s(psp, (jnp.where(lower_left, mh, zero16), jnp.where(lower_left, ml, zero16)))
              for psp, (mh, ml) in zip(p_sp, m_sp)]
        ps = [p - _dot3s(_split2(t), psp) for p, t, psp in zip(ps, ts, p_sp)]
        same = same_next
        shift += 1
    return ps


def _silu(v):
    return v * (1.0 / (1.0 + jnp.exp(-v)))


def _gdn_kernel(dn_ref, ab_ref, z_ref, cw_ref, nega_ref, dtb_ref, gn_ref, tri_ref, o_ref,
                state_ref, carry_ref):
    t = pl.program_id(1)
    C = GDN_CHUNK
    nb = dn_ref.shape[0]

    @pl.when(t == 0)
    def _():
        state_ref[...] = jnp.zeros_like(state_ref)
        carry_ref[...] = jnp.zeros_like(carry_ref)

    row = lax.broadcasted_iota(jnp.int32, (C, C), 0)
    col = lax.broadcasted_iota(jnp.int32, (C, C), 1)
    causal = row >= col
    strict = row > col
    zero_cc = jnp.zeros((C, C), F32)
    cw = cw_ref[...]

    ids, qs, ks, vbs, kbes, decs, egs, kdecs, cds = [], [], [], [], [], [], [], [], []
    for bb in range(nb):
        x = dn_ref[bb]
        xs = jnp.concatenate([carry_ref[bb], x], axis=0)
        carry_ref[bb] = x[C - SUBLANES:C, :]
        y = xs[5:5 + C] * cw[0:1, :]
        for jw in range(1, CONV_WIDTH):
            y = y + xs[5 + jw:5 + jw + C] * cw[jw:jw + 1, :]
        y = _silu(y)

        ab = ab_ref[bb]
        sp_in = ab[:, 0:LANES] + dtb_ref[...]
        softplus = jnp.maximum(sp_in, 0.0) + jnp.log1p(jnp.exp(-jnp.abs(sp_in)))
        g_all = nega_ref[...] * softplus
        beta_all = 1.0 / (1.0 + jnp.exp(-ab[:, LANES:2 * LANES]))
        gc_all = _dot3(tri_ref[...], g_all)
        gc_t = gc_all.T

        for h in range(DN_HEADS):
            qh = y[:, h * DN_HEAD_DIM:(h + 1) * DN_HEAD_DIM]
            kh = y[:, DN_WIDTH + h * DN_HEAD_DIM:DN_WIDTH + (h + 1) * DN_HEAD_DIM]
            vh = y[:, 2 * DN_WIDTH + h * DN_HEAD_DIM:2 * DN_WIDTH + (h + 1) * DN_HEAD_DIM]
            qh = qh * lax.rsqrt(jnp.sum(qh * qh, axis=-1, keepdims=True) + EPS) * (DN_HEAD_DIM ** -0.5)
            kh = kh * lax.rsqrt(jnp.sum(kh * kh, axis=-1, keepdims=True) + EPS)
            gcol = gc_all[:, h:h + 1]
            grow = gc_t[h:h + 1, :]
            bcol = beta_all[:, h:h + 1]
            glast = gc_all[C - 1:C, h:h + 1]
            eg = jnp.exp(gcol)
            kb = kh * bcol
            ids.append((bb, h))
            qs.append(qh)
            ks.append(kh.astype(BF16))
            vbs.append(jnp.concatenate([vh * bcol, kb * eg], axis=1).astype(BF16))
            kbes.append(kb.astype(BF16))
            decs.append(jnp.where(causal, jnp.exp(jnp.where(causal, gcol - grow, zero_cc)), zero_cc))
            egs.append(eg)
            kdecs.append((kh * jnp.exp(glast - gcol)).T.astype(BF16))
            cds.append(jnp.exp(glast))

    n = len(ids)
    m_lows = [jnp.where(strict, _dot_nt(kbes[i], ks[i]) * decs[i], zero_cc) for i in range(n)]
    qks = [jnp.where(causal, _dot_nt(qs[i].astype(BF16), ks[i]) * decs[i], zero_cc).astype(BF16)
           for i in range(n)]
    tinvs = _unit_lower_inverses(m_lows, row, col)
    uws = [_dot(tinvs[i].astype(BF16), vbs[i]) for i in range(n)]
    s_olds = [state_ref[bb, h] for (bb, h) in ids]
    s16s = [s.astype(BF16) for s in s_olds]
    v16s = [(uws[i][:, 0:DN_HEAD_DIM] - _dot(uws[i][:, DN_HEAD_DIM:].astype(BF16), s16s[i])).astype(BF16)
            for i in range(n)]
    outs = [_dot((qs[i] * egs[i]).astype(BF16), s16s[i]) + _dot(qks[i], v16s[i]) for i in range(n)]
    for i, (bb, h) in enumerate(ids):
        state_ref[bb, h] = s_olds[i] * cds[i] + _dot(kdecs[i], v16s[i])

    for i, (bb, h) in enumerate(ids):
        hs = slice(h * DN_HEAD_DIM, (h + 1) * DN_HEAD_DIM)
        o = outs[i]
        ms = jnp.mean(o * o, axis=-1, keepdims=True)
        o_ref[bb, :, hs] = (o * lax.rsqrt(ms + EPS) * gn_ref[...] * _silu(z_ref[bb, :, hs])).astype(BF16)


def _gdn(dn_in, ab, z, conv_w, neg_a, dt_bias, gnorm, tri, batch, seq, nb):
    C = GDN_CHUNK
    nc = seq // C
    blk3 = lambda w: pl.BlockSpec((nb, C, w), lambda bi, t: (bi, t, 0))
    const = lambda shape: pl.BlockSpec(shape, lambda bi, t: (0, 0))
    out = pl.pallas_call(
        _gdn_kernel,
        grid=(batch // nb, nc),
        in_specs=[blk3(3 * DN_WIDTH), blk3(2 * LANES), blk3(DN_WIDTH),
                  const((CONV_WIDTH, 3 * DN_WIDTH)), const((1, LANES)), const((1, LANES)),
                  const((1, DN_HEAD_DIM)), const((C, C))],
        out_specs=blk3(DN_WIDTH),
        out_shape=jax.ShapeDtypeStruct((batch, seq, DN_WIDTH), BF16),
        scratch_shapes=[pltpu.VMEM((nb, DN_HEADS, DN_HEAD_DIM, DN_HEAD_DIM), F32),
                        pltpu.VMEM((nb, SUBLANES, 3 * DN_WIDTH), F32)],
        compiler_params=pltpu.CompilerParams(dimension_semantics=("arbitrary", "arbitrary"),
                                             vmem_limit_bytes=VMEM_LIMIT_BYTES),
        name="gated_deltanet",
    )(dn_in.reshape(batch, seq, -1), ab.reshape(batch, seq, -1), z.reshape(batch, seq, -1),
      conv_w, neg_a, dt_bias, gnorm, tri)
    return out.reshape(batch * seq, DN_WIDTH)


def _outproj_kernel(x_ref, att_ref, dn_ref, wo_ref, g_ref, h_ref, xn_ref):
    h = x_ref[...] + (_dot(att_ref[...], wo_ref[0:ATT_WIDTH, :])
                      + _dot(dn_ref[...], wo_ref[ATT_WIDTH:ATT_WIDTH + DN_WIDTH, :]))
    h_ref[...] = h
    ms = jnp.mean(h * h, axis=-1, keepdims=True)
    xn_ref[...] = (h * lax.rsqrt(ms + EPS) * g_ref[...]).astype(BF16)


def _outproj(x2, att_n, dn_o, w_out, g, tm):
    T = x2.shape[0]
    return pl.pallas_call(
        _outproj_kernel,
        grid=(T // tm,),
        in_specs=[pl.BlockSpec((tm, D_MODEL), lambda i: (i, 0)),
                  pl.BlockSpec((tm, ATT_WIDTH), lambda i: (i, 0)),
                  pl.BlockSpec((tm, DN_WIDTH), lambda i: (i, 0)),
                  pl.BlockSpec((ATT_WIDTH + DN_WIDTH, D_MODEL), lambda i: (0, 0)),
                  pl.BlockSpec((1, D_MODEL), lambda i: (0, 0))],
        out_specs=[pl.BlockSpec((tm, D_MODEL), lambda i: (i, 0)),
                   pl.BlockSpec((tm, D_MODEL), lambda i: (i, 0))],
        out_shape=[jax.ShapeDtypeStruct((T, D_MODEL), F32),
                   jax.ShapeDtypeStruct((T, D_MODEL), BF16)],
        compiler_params=pltpu.CompilerParams(dimension_semantics=("arbitrary",),
                                             vmem_limit_bytes=VMEM_LIMIT_BYTES),
        name="outproj",
    )(x2, att_n, dn_o, w_out, g)


def _oddeven_merge(lo, hi, r):
    step = r * 2
    if step < hi - lo:
        yield from _oddeven_merge(lo, hi, step)
        yield from _oddeven_merge(lo + r, hi, step)
        for i in range(lo + r, hi - r, step):
            yield (i, i + r)
    else:
        yield (lo, lo + r)


def _oddeven_sort_net(lo, hi):
    if hi - lo >= 1:
        mid = lo + (hi - lo) // 2
        yield from _oddeven_sort_net(lo, mid)
        yield from _oddeven_sort_net(mid + 1, hi)
        yield from _oddeven_merge(lo, hi, 1)


_SORT16 = tuple(_oddeven_sort_net(0, PEER_TOPK - 1))
_BITONIC16 = tuple((i, i + d) for d in (8, 4, 2, 1) for i in range(PEER_TOPK) if (i & d) == 0)
_STAIR = tuple((r, q) for r in range(PEER_TOPK) for q in range(PEER_TOPK) if (r + 1) * (q + 1) <= PEER_TOPK)


def _ce_vi(a, b):
    va, ia = a
    vb, ib = b
    a_first = (va > vb) | ((va == vb) & (ia < ib))
    return ((jnp.where(a_first, va, vb), jnp.where(a_first, ia, ib)),
            (jnp.where(a_first, vb, va), jnp.where(a_first, ib, ia)))


def _first_vi(a, b):
    va, ia = a
    vb, ib = b
    a_first = (va > vb) | ((va == vb) & (ia < ib))
    return (jnp.where(a_first, va, vb), jnp.where(a_first, ia, ib))


def _sort16_vi(items):
    items = list(items)
    for i, j in _SORT16:
        items[i], items[j] = _ce_vi(items[i], items[j])
    return items


def _merge_top16_vi(a, b):
    n = PEER_TOPK
    items = [_first_vi(a[i], b[n - 1 - i]) for i in range(n)]
    for i, j in _BITONIC16:
        items[i], items[j] = _ce_vi(items[i], items[j])
    return items


def _sort16_v(vals):
    vals = list(vals)
    for i, j in _SORT16:
        hi = jnp.maximum(vals[i], vals[j])
        lo = jnp.minimum(vals[i], vals[j])
        vals[i], vals[j] = hi, lo
    return vals


def _merge_top16_v(a, b):
    n = PEER_TOPK
    vals = [jnp.maximum(a[i], b[n - 1 - i]) for i in range(n)]
    for i, j in _BITONIC16:
        hi = jnp.maximum(vals[i], vals[j])
        lo = jnp.minimum(vals[i], vals[j])
        vals[i], vals[j] = hi, lo
    return vals


def _top16_vi_of_128(load_key):
    groups = []
    for gi in range(N_KEYS // PEER_TOPK):
        items = [(load_key(gi * PEER_TOPK + t), jnp.full((SUBLANES, LANES), float(gi * PEER_TOPK + t), F32))
                 for t in range(PEER_TOPK)]
        groups.append(_sort16_vi(items))
    while len(groups) > 1:
        groups = [_merge_top16_vi(groups[2 * t], groups[2 * t + 1]) for t in range(len(groups) // 2)]
    return groups[0]


def _top16_v_of_128(load_key):
    groups = [_sort16_v([load_key(gi * PEER_TOPK + t) for t in range(PEER_TOPK)])
              for gi in range(N_KEYS // PEER_TOPK)]
    while len(groups) > 1:
        groups = [_merge_top16_v(groups[2 * t], groups[2 * t + 1]) for t in range(len(groups) // 2)]
    return groups[0]


def _top16_is_distinct(load_key, vals):
    zero = jnp.zeros((SUBLANES, LANES), F32)
    one = jnp.ones((SUBLANES, LANES), F32)
    n_ge = zero
    for k in range(N_KEYS):
        n_ge = n_ge + jnp.where(load_key(k) >= vals[PEER_TOPK - 1], one, zero)
    ok = n_ge == float(PEER_TOPK)
    for r in range(PEER_TOPK - 1):
        ok = ok & (vals[r] > vals[r + 1])
    return jnp.where(ok, one, zero)


def _select_pairs(av, bv):
    zero = jnp.zeros((SUBLANES, LANES), F32)
    one = jnp.ones((SUBLANES, LANES), F32)
    cand = {(r, q): av[r] + bv[q] for (r, q) in _STAIR}
    row0 = [cand[(0, q)] for q in range(PEER_TOPK)]
    rest = [cand[rq] for rq in _STAIR if rq[0] > 0]
    neg_inf = jnp.full((SUBLANES, LANES), -jnp.inf, F32)
    best = row0
    for s in range(0, len(rest), PEER_TOPK):
        grp = rest[s:s + PEER_TOPK]
        grp = grp + [neg_inf] * (PEER_TOPK - len(grp))
        best = _merge_top16_v(best, _sort16_v(grp))
    thr = best[PEER_TOPK - 1]
    zsum = one
    for jj in range(1, PEER_TOPK):
        zsum = zsum + jnp.exp(best[jj] - best[0])
    inv_z_half = 0.5 / zsum

    n_gt = zero
    for rq in _STAIR:
        n_gt = n_gt + jnp.where(cand[rq] > thr, one, zero)
    need = float(PEER_TOPK) - n_gt
    cnt = zero
    n_row = [zero] * PEER_TOPK
    for (r, q) in _STAIR:
        c = cand[(r, q)]
        eq = c == thr
        take = (c > thr) | (eq & (cnt < need))
        cnt = cnt + jnp.where(eq, one, zero)
        n_row[r] = n_row[r] + jnp.where(take, one, zero)
    return n_row, inv_z_half


def _match_count(keys, probe, n_row):
    n_k = jnp.zeros((SUBLANES, LANES), F32)
    for r in reversed(range(PEER_TOPK)):
        n_k = jnp.where(keys[r] == probe, n_row[r], n_k)
    return n_k


def _match_rank(probe, rows):
    rk = jnp.full(probe.shape, float(PEER_TOPK), F32)
    for q in reversed(range(PEER_TOPK)):
        rk = jnp.where(rows[q] == probe, jnp.full_like(rk, float(q)), rk)
    return rk


def _peer_topk_kernel(xn_ref, wqt_ref, k0_ref, k1_ref, r1_ref, e1_ref, n_ref, c_ref, s0_ref, s1_ref):
    tb = xn_ref.shape[0]
    half_w = PEER_HEADS * PEER_HALF
    qt = _dot_nt(wqt_ref[...], xn_ref[...]).astype(BF16)
    s0 = _dot(k0_ref[...], qt[0:half_w])
    s1 = _dot(k1_ref[...], qt[half_w:2 * half_w])
    for g in range(tb // LANES):
        s0_ref[g] = s0[:, g * LANES:(g + 1) * LANES]
        s1_ref[g] = s1[:, g * LANES:(g + 1) * LANES]

    sub_iota = lax.broadcasted_iota(jnp.int32, (_ROWS16, LANES), 0).astype(F32)

    def slab(gi, carry):
        ls = pl.ds(pl.multiple_of(gi * LANES, LANES), LANES)
        load0 = lambda k: s0_ref[gi, pl.ds(k * SUBLANES, SUBLANES), :]
        load1 = lambda k: s1_ref[gi, pl.ds(k * SUBLANES, SUBLANES), :]
        load1_head = lambda h, kb: s1_ref[gi, pl.ds(kb * _ROWS16 * SUBLANES + h, _ROWS16, stride=SUBLANES), :]

        av = _top16_v_of_128(load0)
        bv = _top16_v_of_128(load1)
        n_row, inv_z_half = _select_pairs(av, bv)
        distinct = _top16_is_distinct(load0, av) * _top16_is_distinct(load1, bv)
        has_tie = jnp.min(distinct) < 0.5

        def emit(keys0, probe0, keys1, probe1):
            for k in range(N_KEYS):
                rs = pl.ds(k * SUBLANES, SUBLANES)
                s0k = load0(k)
                n_ref[rs, ls] = _match_count(keys0, probe0(k, s0k), n_row)
                c_ref[rs, ls] = jnp.exp(s0k - av[0]) * inv_z_half
            for h in range(PEER_HEADS):
                rows = [jnp.broadcast_to(keys1[q][h:h + 1, :], (_ROWS16, LANES)) for q in range(PEER_TOPK)]
                b0h = jnp.broadcast_to(bv[0][h:h + 1, :], (_ROWS16, LANES))
                for kb in range(N_KEYS // _ROWS16):
                    rs = pl.ds(h * N_KEYS + kb * _ROWS16, _ROWS16)
                    tile = load1_head(h, kb)
                    r1_ref[rs, ls] = _match_rank(probe1(kb, tile), rows).astype(BF16)
                    e1_ref[rs, ls] = jnp.exp(tile - b0h).astype(BF16)

        @pl.when(jnp.logical_not(has_tie))
        def _():
            emit(av, lambda k, s0k: s0k, bv, lambda kb, tile: tile)

        @pl.when(has_tie)
        def _():
            a = _top16_vi_of_128(load0)
            b = _top16_vi_of_128(load1)
            emit([it[1] for it in a], lambda k, s0k: float(k),
                 [it[1] for it in b], lambda kb, tile: sub_iota + float(kb * _ROWS16))

        return carry

    lax.fori_loop(0, tb // LANES, slab, 0)


def _peer_topk(xn2, wqt, k0, k1, tb):
    T = xn2.shape[0]
    rows = PEER_HEADS * N_KEYS
    kspec = pl.BlockSpec((rows, PEER_HEADS * PEER_HALF), lambda i: (0, 0))
    ospec = pl.BlockSpec((rows, tb), lambda i: (0, i))
    return pl.pallas_call(
        _peer_topk_kernel,
        grid=(T // tb,),
        in_specs=[pl.BlockSpec((tb, D_MODEL), lambda i: (i, 0)),
                  pl.BlockSpec((D_MODEL, D_MODEL), lambda i: (0, 0)),
                  kspec, kspec],
        out_specs=[ospec, ospec, ospec, ospec],
        out_shape=[jax.ShapeDtypeStruct((rows, T), BF16),
                   jax.ShapeDtypeStruct((rows, T), BF16),
                   jax.ShapeDtypeStruct((rows, T), F32),
                   jax.ShapeDtypeStruct((rows, T), F32)],
        scratch_shapes=[pltpu.VMEM((tb // LANES, rows, LANES), F32),
                        pltpu.VMEM((tb // LANES, rows, LANES), F32)],
        compiler_params=pltpu.CompilerParams(dimension_semantics=("arbitrary",),
                                             vmem_limit_bytes=VMEM_LIMIT_BYTES),
        name="peer_topk",
    )(xn2, wqt, k0, k1)


_SUB_E = 1024
_W_TILE = 256


def _peer_dense_kernel(xn_ref, h_ref, r1_ref, e1_ref, n_ref, c_ref, u_next_ref, u_first_ref,
                       vt_ref, o_ref, acc_ref, hm_ref, at_ref):
    c = pl.program_id(1)
    tb = xn_ref.shape[0]
    n_sub = vt_ref.shape[0]
    i0_per_sub = _SUB_E // N_KEYS
    xn = xn_ref[...]

    @pl.when(c == 0)
    def _():
        acc_ref[...] = jnp.zeros_like(acc_ref)
        at_ref[0] = _dot_nt(u_first_ref[...], xn)

    tw = min(tb, _W_TILE)
    zero16 = jnp.zeros((_ROWS16, tw), BF16)
    n_j = N_KEYS // _ROWS16

    def gated_acts(sub, slot, out_ref):
        for ii in range(i0_per_sub):
            i0 = (c * n_sub + sub) * i0_per_sub + ii
            for lt in range(tb // tw):
                ls = slice(lt * tw, (lt + 1) * tw)
                wacc = [None] * n_j
                for h in range(PEER_HEADS):
                    rowi = pl.ds(i0 * PEER_HEADS + h, 1)
                    nb = jnp.broadcast_to(n_ref[rowi, ls], (_ROWS16, tw)).astype(BF16)
                    cb = jnp.broadcast_to(c_ref[rowi, ls], (_ROWS16, tw)).astype(BF16)
                    for j in range(n_j):
                        rs = slice(h * N_KEYS + j * _ROWS16, h * N_KEYS + (j + 1) * _ROWS16)
                        term = jnp.where(r1_ref[rs, ls] < nb, e1_ref[rs, ls] * cb, zero16)
                        wacc[j] = term if wacc[j] is None else wacc[j] + term
                for j in range(n_j):
                    er = slice(ii * N_KEYS + j * _ROWS16, ii * N_KEYS + (j + 1) * _ROWS16)
                    a16 = at_ref[slot, er, ls]
                    gl = a16 * (1.0 + lax.erf(a16 * 0.7071067811865476))
                    out_ref[er, ls] = gl.astype(BF16) * wacc[j]

    for sub in range(n_sub):
        par = sub % 2
        at_ref[1 - par] = _dot_nt(u_next_ref[sub * _SUB_E:(sub + 1) * _SUB_E, :], xn)
        if sub > 0:
            acc_ref[...] += _dot(vt_ref[sub - 1], hm_ref[1 - par])
        gated_acts(sub, par, hm_ref.at[par])
    acc_ref[...] += _dot(vt_ref[n_sub - 1], hm_ref[(n_sub - 1) % 2])

    @pl.when(c == pl.num_programs(1) - 1)
    def _():
        o_ref[...] = h_ref[...] + acc_ref[...].T


def _peer_dense(xn2, h, r1, e1, n_sel, c_gate, u16, vt_slabs, tb, ec):
    T = xn2.shape[0]
    rows = PEER_HEADS * N_KEYS
    n_sub = ec // _SUB_E
    assert n_sub >= 2
    tspec = pl.BlockSpec((rows, tb), lambda i, c: (0, i))
    u_next = jnp.roll(u16, -_SUB_E, axis=0)
    return pl.pallas_call(
        _peer_dense_kernel,
        grid=(T // tb, N_EXPERTS // ec),
        in_specs=[pl.BlockSpec((tb, D_MODEL), lambda i, c: (i, 0)),
                  pl.BlockSpec((tb, D_MODEL), lambda i, c: (i, 0)),
                  tspec, tspec, tspec, tspec,
                  pl.BlockSpec((ec, D_MODEL), lambda i, c: (c, 0)),
                  pl.BlockSpec((_SUB_E, D_MODEL), lambda i, c: (0, 0)),
                  pl.BlockSpec((n_sub, D_MODEL, _SUB_E), lambda i, c: (c, 0, 0))],
        out_specs=pl.BlockSpec((tb, D_MODEL), lambda i, c: (i, 0)),
        out_shape=jax.ShapeDtypeStruct((T, D_MODEL), F32),
        scratch_shapes=[pltpu.VMEM((D_MODEL, tb), F32),
                        pltpu.VMEM((2, _SUB_E, tb), BF16),
                        pltpu.VMEM((2, _SUB_E, tb), F32)],
        compiler_params=pltpu.CompilerParams(dimension_semantics=("arbitrary", "arbitrary"),
                                             vmem_limit_bytes=VMEM_LIMIT_BYTES),
        name="peer_dense",
    )(xn2, h, r1, e1, n_sel, c_gate, u_next, u16, vt_slabs)


def _block_diag_ones(width, blk):
    idx = np.arange(width) // blk
    return jnp.asarray((idx[:, None] == idx[None, :]).astype(np.float32), dtype=BF16)


def _layer(x, norm_mix_g, w_in, att_q_norm_g, att_k_norm_g, att_sinks, att_out_norm_g, dn_conv_w,
           dn_a_log, dn_dt_bias, dn_out_norm_g, w_out, norm_ffn_g, peer_w_q, peer_sub_keys, peer_u, peer_v):
    B, S, _ = x.shape
    T = B * S
    x2 = x.reshape(T, D_MODEL)

    o_q, o_k, o_v = 0, ATT_WIDTH, ATT_WIDTH + ATT_KV_WIDTH
    o_dn = ATT_WIDTH + 2 * ATT_KV_WIDTH
    o_a = o_dn + 3 * DN_WIDTH
    o_b = o_a + DN_HEADS
    o_z = o_b + DN_HEADS
    pad = jnp.zeros((D_MODEL, LANES - DN_HEADS), w_in.dtype)
    w_all = jnp.concatenate([w_in[:, o_q:o_dn], w_in[:, o_dn:o_a], w_in[:, o_z:o_z + DN_WIDTH],
                             w_in[:, o_a:o_b], pad, w_in[:, o_b:o_z], pad], axis=1).astype(BF16)
    del o_k, o_v

    def lane_pad(vec):
        return jnp.concatenate([vec.astype(F32), jnp.zeros((LANES - vec.shape[0],), F32)]).reshape(1, LANES)

    att_in, dn_in, z_in, ab_in = _inproj(x2, norm_mix_g.reshape(1, D_MODEL), w_all, tm=512)

    att_n = _attention(
        att_in, att_sinks.astype(F32),
        jnp.tile(att_q_norm_g.astype(F32), ATT_HEADS).reshape(1, ATT_WIDTH),
        jnp.tile(att_k_norm_g.astype(F32), ATT_KV_HEADS).reshape(1, ATT_KV_WIDTH),
        att_out_norm_g.astype(F32).reshape(1, ATT_WIDTH),
        _block_diag_ones(ATT_WIDTH, ATT_HEAD_DIM), _block_diag_ones(ATT_KV_WIDTH, ATT_HEAD_DIM),
        seq=S, tq=512)

    tri = jnp.asarray(np.tril(np.ones((GDN_CHUNK, GDN_CHUNK), np.float32)))
    dn_o = _gdn(dn_in, ab_in, z_in, dn_conv_w.astype(F32),
                lane_pad(-jnp.exp(dn_a_log.astype(F32))), lane_pad(dn_dt_bias),
                dn_out_norm_g.astype(F32).reshape(1, DN_HEAD_DIM), tri, batch=B, seq=S, nb=4)

    h, xn2 = _outproj(x2, att_n, dn_o, w_out.astype(BF16), norm_ffn_g.reshape(1, D_MODEL), tm=512)

    wqt = (peer_w_q.T.reshape(PEER_HEADS, 2, PEER_HALF, D_MODEL).transpose(1, 0, 2, 3)
           .reshape(D_MODEL, D_MODEL).astype(BF16))
    eye = jnp.eye(PEER_HEADS, dtype=peer_sub_keys.dtype)
    kd = jnp.einsum('hpkc,hg->pkhgc', peer_sub_keys, eye)
    k_km = kd.reshape(2, N_KEYS * PEER_HEADS, PEER_HEADS * PEER_HALF).astype(BF16)
    r1, e1, n_sel, c_gate = _peer_topk(xn2, wqt, k_km[0], k_km[1], tb=512)

    vt_slabs = (peer_v.reshape(N_EXPERTS // _SUB_E, _SUB_E, D_MODEL).transpose(0, 2, 1).astype(BF16))
    out = _peer_dense(xn2, h, r1, e1, n_sel, c_gate, peer_u.astype(BF16), vt_slabs, tb=512, ec=2048)
    return out.reshape(B, S, D_MODEL)


def kernel(x, norm_mix_g, w_in, att_q_norm_g, att_k_norm_g, att_sinks, att_out_norm_g, dn_conv_w, dn_a_log, dn_dt_bias, dn_out_norm_g, w_out, norm_ffn_g, peer_w_q, peer_sub_keys, peer_u, peer_v):
    h = x
    for l in range(norm_mix_g.shape[0]):
        h = _layer(h, norm_mix_g[l], w_in[l], att_q_norm_g[l], att_k_norm_g[l], att_sinks[l],
                   att_out_norm_g[l], dn_conv_w[l], dn_a_log[l], dn_dt_bias[l], dn_out_norm_g[l],
                   w_out[l], norm_ffn_g[l], peer_w_q[l], peer_sub_keys[l], peer_u[l], peer_v[l])
    return h
```

```python
import functools

import numpy as np
import jax
import jax.numpy as jnp
from jax import lax
from jax.experimental import pallas as pl
from jax.experimental.pallas import tpu as pltpu

F32 = jnp.float32
BF16 = jnp.bfloat16

D_MODEL = 1024
ATT_HEADS = 8
ATT_KV_HEADS = 2
ATT_HEAD_DIM = 64
ATT_WIDTH = 512
ATT_KV_WIDTH = 128
ATT_BLOCK = 128
DN_HEADS = 4
DN_HEAD_DIM = 128
DN_WIDTH = 512
CONV_WIDTH = 4
PEER_HEADS = 8
N_KEYS = 128
N_EXPERTS = N_KEYS * N_KEYS
PEER_HALF = 64
PEER_TOPK = 16
EPS = 1e-6

LANES = 128
SUBLANES = 8
VMEM_LIMIT_BYTES = 56 * 1024 * 1024

_C_ATT = 0
_C_DN = 768
_C_Z = _C_DN + 3 * DN_WIDTH
_C_AB = _C_Z + DN_WIDTH
_C_END = _C_AB + 2 * LANES

GDN_CHUNK = 128
_ROWS16 = 2 * SUBLANES


def _dot(a, b):
    return jnp.dot(a, b, preferred_element_type=F32)


def _dot_nt(a, b):
    return lax.dot_general(a, b, (((1,), (1,)), ((), ())), preferred_element_type=F32)


def _split2(a):
    hi = a.astype(BF16)
    lo = (a - hi.astype(F32)).astype(BF16)
    return hi, lo


def _dot3s(a_split, b_split):
    ah, al = a_split
    bh, bl = b_split
    return _dot(ah, bh) + (_dot(ah, bl) + _dot(al, bh))


def _dot3(a, b):
    return _dot3s(_split2(a), _split2(b))


def _inproj_kernel(x_ref, g_ref, w_ref, att_ref, dn_ref, z_ref, ab_ref):
    x = x_ref[...]
    ms = jnp.mean(x * x, axis=-1, keepdims=True)
    xn = (x * lax.rsqrt(ms + EPS) * g_ref[...]).astype(BF16)
    att_ref[...] = _dot(xn, w_ref[:, _C_ATT:_C_DN])
    dn_ref[...] = _dot(xn, w_ref[:, _C_DN:_C_Z])
    z_ref[...] = _dot(xn, w_ref[:, _C_Z:_C_AB])
    ab_ref[...] = _dot(xn, w_ref[:, _C_AB:_C_END])


def _inproj(x2, g, w_all, tm):
    T = x2.shape[0]
    return pl.pallas_call(
        _inproj_kernel,
        grid=(T // tm,),
        in_specs=[pl.BlockSpec((tm, D_MODEL), lambda i: (i, 0)),
                  pl.BlockSpec((1, D_MODEL), lambda i: (0, 0)),
                  pl.BlockSpec((D_MODEL, _C_END), lambda i: (0, 0))],
        out_specs=[pl.BlockSpec((tm, 768), lambda i: (i, 0)),
                   pl.BlockSpec((tm, 3 * DN_WIDTH), lambda i: (i, 0)),
                   pl.BlockSpec((tm, DN_WIDTH), lambda i: (i, 0)),
                   pl.BlockSpec((tm, 2 * LANES), lambda i: (i, 0))],
        out_shape=[jax.ShapeDtypeStruct((T, 768), F32),
                   jax.ShapeDtypeStruct((T, 3 * DN_WIDTH), F32),
                   jax.ShapeDtypeStruct((T, DN_WIDTH), F32),
                   jax.ShapeDtypeStruct((T, 2 * LANES), F32)],
        compiler_params=pltpu.CompilerParams(dimension_semantics=("arbitrary",),
                                             vmem_limit_bytes=VMEM_LIMIT_BYTES),
        name="inproj",
    )(x2, g, w_all)


def _head_rms(t, bd, g):
    hi, lo = _split2(t * t)
    ss = _dot(hi, bd) + _dot(lo, bd)
    return t * lax.rsqrt(ss * (1.0 / ATT_HEAD_DIM) + EPS) * g


def _attn_kernel(sink_ref, cur_ref, prev_ref, gq_ref, gk_ref, go_ref, bdq_ref, bdk_ref, o_ref,
                 *, tiles_per_seq):
    i = pl.program_id(0)
    tq = cur_ref.shape[0]
    nblk = tq // ATT_BLOCK
    first = (i % tiles_per_seq) == 0

    q = cur_ref[:, 0:ATT_WIDTH]
    k = cur_ref[:, ATT_WIDTH:ATT_WIDTH + ATT_KV_WIDTH]
    v = cur_ref[:, ATT_WIDTH + ATT_KV_WIDTH:ATT_WIDTH + 2 * ATT_KV_WIDTH]
    kp = prev_ref[:, 0:ATT_KV_WIDTH]
    vp = prev_ref[:, ATT_KV_WIDTH:2 * ATT_KV_WIDTH]

    qn = (_head_rms(q, bdq_ref[...], gq_ref[...]) * (ATT_HEAD_DIM ** -0.5)).astype(BF16)
    kn = jnp.concatenate([_head_rms(kp, bdk_ref[...], gk_ref[...]),
                          _head_rms(k, bdk_ref[...], gk_ref[...])], axis=0)
    vf = jnp.concatenate([vp, v], axis=0)

    lane = lax.broadcasted_iota(jnp.int32, kn.shape, 1)
    left = lane < ATT_HEAD_DIM
    zero = jnp.zeros_like(kn)
    k0l = jnp.where(left, kn, zero)
    k1r = jnp.where(left, zero, kn)
    v0l = jnp.where(left, vf, zero)
    v1r = jnp.where(left, zero, vf)
    kl = (k0l.astype(BF16), pltpu.roll(k1r, ATT_HEAD_DIM, 1).astype(BF16))
    kr = (pltpu.roll(k0l, ATT_HEAD_DIM, 1).astype(BF16), k1r.astype(BF16))
    vl = (v0l.astype(BF16), pltpu.roll(v1r, ATT_HEAD_DIM, 1).astype(BF16))
    vr = (pltpu.roll(v0l, ATT_HEAD_DIM, 1).astype(BF16), v1r.astype(BF16))

    qi = lax.broadcasted_iota(jnp.int32, (ATT_BLOCK, 2 * ATT_BLOCK), 0)
    kj = lax.broadcasted_iota(jnp.int32, (ATT_BLOCK, 2 * ATT_BLOCK), 1)
    rel = qi + ATT_BLOCK - kj
    in_window = (rel >= 0) & (rel < ATT_BLOCK)
    first_key = jnp.where(first, ATT_BLOCK, 0)
    neg_inf = jnp.full((ATT_BLOCK, 2 * ATT_BLOCK), -jnp.inf, F32)

    def softmax_rows(s, sink):
        m = jnp.maximum(jnp.max(s, axis=-1, keepdims=True), sink)
        p = jnp.exp(s - m)
        den = jnp.sum(p, axis=-1, keepdims=True) + jnp.exp(sink - m)
        return (p * (1.0 / den)).astype(BF16)

    for j in range(nblk):
        rows = slice(j * ATT_BLOCK, (j + 1) * ATT_BLOCK)
        krows = slice(j * ATT_BLOCK, (j + 2) * ATT_BLOCK)
        if j == 0:
            mask = in_window & (kj >= first_key)
        else:
            mask = in_window
        pairs = []
        for c in range(ATT_KV_HEADS):
            qe = jnp.concatenate([qn[rows, (2 * c) * LANES:(2 * c + 1) * LANES],
                                  qn[rows, (2 * c + 1) * LANES:(2 * c + 2) * LANES]], axis=0)
            s_even = _dot_nt(qe, kl[c][krows])
            s_odd = _dot_nt(qe, kr[c][krows])
            for half in range(2):
                hr = slice(half * ATT_BLOCK, (half + 1) * ATT_BLOCK)
                h_even = 4 * c + 2 * half
                p_e = softmax_rows(jnp.where(mask, s_even[hr], neg_inf), sink_ref[h_even])
                p_o = softmax_rows(jnp.where(mask, s_odd[hr], neg_inf), sink_ref[h_even + 1])
                pairs.append(_dot(p_e, vl[c][krows]) + _dot(p_o, vr[c][krows]))
        att = jnp.concatenate(pairs, axis=1)
        ms = jnp.mean(att * att, axis=-1, keepdims=True)
        o_ref[rows, :] = (att * lax.rsqrt(ms + EPS) * go_ref[...]).astype(BF16)


def _attention(att_in, sinks, gq, gk, go, bdq, bdk, seq, tq):
    T = att_in.shape[0]
    tiles_per_seq = seq // tq
    blk_per_tile = tq // ATT_BLOCK
    kv_col_blk = ATT_WIDTH // (2 * ATT_KV_WIDTH)
    return pl.pallas_call(
        functools.partial(_attn_kernel, tiles_per_seq=tiles_per_seq),
        grid_spec=pltpu.PrefetchScalarGridSpec(
            num_scalar_prefetch=1,
            grid=(T // tq,),
            in_specs=[pl.BlockSpec((tq, 768), lambda i, s: (i, 0)),
                      pl.BlockSpec((ATT_BLOCK, 2 * ATT_KV_WIDTH),
                                   lambda i, s: (jnp.maximum(i * blk_per_tile - 1, 0), kv_col_blk)),
                      pl.BlockSpec((1, ATT_WIDTH), lambda i, s: (0, 0)),
                      pl.BlockSpec((1, ATT_KV_WIDTH), lambda i, s: (0, 0)),
                      pl.BlockSpec((1, ATT_WIDTH), lambda i, s: (0, 0)),
                      pl.BlockSpec((ATT_WIDTH, ATT_WIDTH), lambda i, s: (0, 0)),
                      pl.BlockSpec((ATT_KV_WIDTH, ATT_KV_WIDTH), lambda i, s: (0, 0))],
            out_specs=pl.BlockSpec((tq, ATT_WIDTH), lambda i, s: (i, 0)),
        ),
        out_shape=jax.ShapeDtypeStruct((T, ATT_WIDTH), BF16),
        compiler_params=pltpu.CompilerParams(dimension_semantics=("arbitrary",),
                                             vmem_limit_bytes=VMEM_LIMIT_BYTES),
        name="swa_attention",
    )(sinks, att_in, att_in, gq, gk, go, bdq, bdk)


def _unit_lower_inverses(m_lows, row, col):
    n = m_lows[0].shape[0]
    zero = jnp.zeros_like(m_lows[0])
    zero16 = jnp.zeros(zero.shape, BF16)
    eye = jnp.where(row == col, jnp.ones_like(zero), zero)
    same = (row >> 3) == (col >> 3)
    m_sp = [_split2(m) for m in m_lows]
    nm_sp = [(jnp.where(same, -mh, zero16), jnp.where(same, -ml, zero16)) for mh, ml in m_sp]
    ps = [eye + jnp.where(same, -m, zero) for m in m_lows]
    n2_sp = [_split2(_dot3s(nm, nm)) for nm in nm_sp]
    ps = [p + _dot3s(_split2(p), n2) for p, n2 in zip(ps, n2_sp)]
    n4_sp = [_split2(_dot3s(n2, n2)) for n2 in n2_sp]
    ps = [p + _dot3s(_split2(p), n4) for p, n4 in zip(ps, n4_sp)]
    shift = 3
    while (1 << shift) < n:
        same_next = (row >> (shift + 1)) == (col >> (shift + 1))
        lower_left = same_next & jnp.logical_not(same)
        p_sp = [_split2(p) for p in ps]
        ts = [_dot3s(psp, (jnp.where(lower_left, mh, zero16), jnp.where(lower_left, ml, zero16)))
              for psp, (mh, ml) in zip(p_sp, m_sp)]
        ps = [p - _dot3s(_split2(t), psp) for p, t, psp in zip(ps, ts, p_sp)]
        same = same_next
        shift += 1
    return ps


def _silu(v):
    return v * (1.0 / (1.0 + jnp.exp(-v)))


def _gdn_kernel(dn_ref, ab_ref, z_ref, cw_ref, nega_ref, dtb_ref, gn_ref, tri_ref, o_ref,
                state_ref, carry_ref):
    t = pl.program_id(1)
    C = GDN_CHUNK
    nb = dn_ref.shape[0]

    @pl.when(t == 0)
    def _():
        state_ref[...] = jnp.zeros_like(state_ref)
        carry_ref[...] = jnp.zeros_like(carry_ref)

    row = lax.broadcasted_iota(jnp.int32, (C, C), 0)
    col = lax.broadcasted_iota(jnp.int32, (C, C), 1)
    causal = row >= col
    strict = row > col
    zero_cc = jnp.zeros((C, C), F32)
    cw = cw_ref[...]

    ids, qs, ks, vbs, kbes, decs, egs, kdecs, cds = [], [], [], [], [], [], [], [], []
    for bb in range(nb):
        x = dn_ref[bb]
        xs = jnp.concatenate([carry_ref[bb], x], axis=0)
        carry_ref[bb] = x[C - SUBLANES:C, :]
        y = xs[5:5 + C] * cw[0:1, :]
        for jw in range(1, CONV_WIDTH):
            y = y + xs[5 + jw:5 + jw + C] * cw[jw:jw + 1, :]
        y = _silu(y)

        ab = ab_ref[bb]
        sp_in = ab[:, 0:LANES] + dtb_ref[...]
        softplus = jnp.maximum(sp_in, 0.0) + jnp.log1p(jnp.exp(-jnp.abs(sp_in)))
        g_all = nega_ref[...] * softplus
        beta_all = 1.0 / (1.0 + jnp.exp(-ab[:, LANES:2 * LANES]))
        gc_all = _dot3(tri_ref[...], g_all)
        gc_t = gc_all.T

        for h in range(DN_HEADS):
            qh = y[:, h * DN_HEAD_DIM:(h + 1) * DN_HEAD_DIM]
            kh = y[:, DN_WIDTH + h * DN_HEAD_DIM:DN_WIDTH + (h + 1) * DN_HEAD_DIM]
            vh = y[:, 2 * DN_WIDTH + h * DN_HEAD_DIM:2 * DN_WIDTH + (h + 1) * DN_HEAD_DIM]
            qh = qh * lax.rsqrt(jnp.sum(qh * qh, axis=-1, keepdims=True) + EPS) * (DN_HEAD_DIM ** -0.5)
            kh = kh * lax.rsqrt(jnp.sum(kh * kh, axis=-1, keepdims=True) + EPS)
            gcol = gc_all[:, h:h + 1]
            grow = gc_t[h:h + 1, :]
            bcol = beta_all[:, h:h + 1]
            glast = gc_all[C - 1:C, h:h + 1]
            eg = jnp.exp(gcol)
            kb = kh * bcol
            ids.append((bb, h))
            qs.append(qh)
            ks.append(kh.astype(BF16))
            vbs.append(jnp.concatenate([vh * bcol, kb * eg], axis=1).astype(BF16))
            kbes.append(kb.astype(BF16))
            decs.append(jnp.where(causal, jnp.exp(jnp.where(causal, gcol - grow, zero_cc)), zero_cc))
            egs.append(eg)
            kdecs.append((kh * jnp.exp(glast - gcol)).T.astype(BF16))
            cds.append(jnp.exp(glast))

    n = len(ids)
    m_lows = [jnp.where(strict, _dot_nt(kbes[i], ks[i]) * decs[i], zero_cc) for i in range(n)]
    qks = [jnp.where(causal, _dot_nt(qs[i].astype(BF16), ks[i]) * decs[i], zero_cc).astype(BF16)
           for i in range(n)]
    tinvs = _unit_lower_inverses(m_lows, row, col)
    uws = [_dot(tinvs[i].astype(BF16), vbs[i]) for i in range(n)]
    s_olds = [state_ref[bb, h] for (bb, h) in ids]
    s16s = [s.astype(BF16) for s in s_olds]
    v16s = [(uws[i][:, 0:DN_HEAD_DIM] - _dot(uws[i][:, DN_HEAD_DIM:].astype(BF16), s16s[i])).astype(BF16)
            for i in range(n)]
    outs = [_dot((qs[i] * egs[i]).astype(BF16), s16s[i]) + _dot(qks[i], v16s[i]) for i in range(n)]
    for i, (bb, h) in enumerate(ids):
        state_ref[bb, h] = s_olds[i] * cds[i] + _dot(kdecs[i], v16s[i])

    for i, (bb, h) in enumerate(ids):
        hs = slice(h * DN_HEAD_DIM, (h + 1) * DN_HEAD_DIM)
        o = outs[i]
        ms = jnp.mean(o * o, axis=-1, keepdims=True)
        o_ref[bb, :, hs] = (o * lax.rsqrt(ms + EPS) * gn_ref[...] * _silu(z_ref[bb, :, hs])).astype(BF16)


def _gdn(dn_in, ab, z, conv_w, neg_a, dt_bias, gnorm, tri, batch, seq, nb):
    C = GDN_CHUNK
    nc = seq // C
    blk3 = lambda w: pl.BlockSpec((nb, C, w), lambda bi, t: (bi, t, 0))
    const = lambda shape: pl.BlockSpec(shape, lambda bi, t: (0, 0))
    out = pl.pallas_call(
        _gdn_kernel,
        grid=(batch // nb, nc),
        in_specs=[blk3(3 * DN_WIDTH), blk3(2 * LANES), blk3(DN_WIDTH),
                  const((CONV_WIDTH, 3 * DN_WIDTH)), const((1, LANES)), const((1, LANES)),
                  const((1, DN_HEAD_DIM)), const((C, C))],
        out_specs=blk3(DN_WIDTH),
        out_shape=jax.ShapeDtypeStruct((batch, seq, DN_WIDTH), BF16),
        scratch_shapes=[pltpu.VMEM((nb, DN_HEADS, DN_HEAD_DIM, DN_HEAD_DIM), F32),
                        pltpu.VMEM((nb, SUBLANES, 3 * DN_WIDTH), F32)],
        compiler_params=pltpu.CompilerParams(dimension_semantics=("arbitrary", "arbitrary"),
                                             vmem_limit_bytes=VMEM_LIMIT_BYTES),
        name="gated_deltanet",
    )(dn_in.reshape(batch, seq, -1), ab.reshape(batch, seq, -1), z.reshape(batch, seq, -1),
      conv_w, neg_a, dt_bias, gnorm, tri)
    return out.reshape(batch * seq, DN_WIDTH)


def _outproj_kernel(x_ref, att_ref, dn_ref, wo_ref, g_ref, h_ref, xn_ref):
    h = x_ref[...] + (_dot(att_ref[...], wo_ref[0:ATT_WIDTH, :])
                      + _dot(dn_ref[...], wo_ref[ATT_WIDTH:ATT_WIDTH + DN_WIDTH, :]))
    h_ref[...] = h
    ms = jnp.mean(h * h, axis=-1, keepdims=True)
    xn_ref[...] = (h * lax.rsqrt(ms + EPS) * g_ref[...]).astype(BF16)


def _outproj(x2, att_n, dn_o, w_out, g, tm):
    T = x2.shape[0]
    return pl.pallas_call(
        _outproj_kernel,
        grid=(T // tm,),
        in_specs=[pl.BlockSpec((tm, D_MODEL), lambda i: (i, 0)),
                  pl.BlockSpec((tm, ATT_WIDTH), lambda i: (i, 0)),
                  pl.BlockSpec((tm, DN_WIDTH), lambda i: (i, 0)),
                  pl.BlockSpec((ATT_WIDTH + DN_WIDTH, D_MODEL), lambda i: (0, 0)),
                  pl.BlockSpec((1, D_MODEL), lambda i: (0, 0))],
        out_specs=[pl.BlockSpec((tm, D_MODEL), lambda i: (i, 0)),
                   pl.BlockSpec((tm, D_MODEL), lambda i: (i, 0))],
        out_shape=[jax.ShapeDtypeStruct((T, D_MODEL), F32),
                   jax.ShapeDtypeStruct((T, D_MODEL), BF16)],
        compiler_params=pltpu.CompilerParams(dimension_semantics=("arbitrary",),
                                             vmem_limit_bytes=VMEM_LIMIT_BYTES),
        name="outproj",
    )(x2, att_n, dn_o, w_out, g)


def _oddeven_merge(lo, hi, r):
    step = r * 2
    if step < hi - lo:
        yield from _oddeven_merge(lo, hi, step)
        yield from _oddeven_merge(lo + r, hi, step)
        for i in range(lo + r, hi - r, step):
            yield (i, i + r)
    else:
        yield (lo, lo + r)


def _oddeven_sort_net(lo, hi):
    if hi - lo >= 1:
        mid = lo + (hi - lo) // 2
        yield from _oddeven_sort_net(lo, mid)
        yield from _oddeven_sort_net(mid + 1, hi)
        yield from _oddeven_merge(lo, hi, 1)


_SORT16 = tuple(_oddeven_sort_net(0, PEER_TOPK - 1))
_BITONIC16 = tuple((i, i + d) for d in (8, 4, 2, 1) for i in range(PEER_TOPK) if (i & d) == 0)
_STAIR = tuple((r, q) for r in range(PEER_TOPK) for q in range(PEER_TOPK) if (r + 1) * (q + 1) <= PEER_TOPK)


def _ce_vi(a, b):
    va, ia = a
    vb, ib = b
    a_first = (va > vb) | ((va == vb) & (ia < ib))
    return ((jnp.where(a_first, va, vb), jnp.where(a_first, ia, ib)),
            (jnp.where(a_first, vb, va), jnp.where(a_first, ib, ia)))


def _first_vi(a, b):
    va, ia = a
    vb, ib = b
    a_first = (va > vb) | ((va == vb) & (ia < ib))
    return (jnp.where(a_first, va, vb), jnp.where(a_first, ia, ib))


def _sort16_vi(items):
    items = list(items)
    for i, j in _SORT16:
        items[i], items[j] = _ce_vi(items[i], items[j])
    return items


def _merge_top16_vi(a, b):
    n = PEER_TOPK
    items = [_first_vi(a[i], b[n - 1 - i]) for i in range(n)]
    for i, j in _BITONIC16:
        items[i], items[j] = _ce_vi(items[i], items[j])
    return items


def _sort16_v(vals):
    vals = list(vals)
    for i, j in _SORT16:
        hi = jnp.maximum(vals[i], vals[j])
        lo = jnp.minimum(vals[i], vals[j])
        vals[i], vals[j] = hi, lo
    return vals


def _merge_top16_v(a, b):
    n = PEER_TOPK
    vals = [jnp.maximum(a[i], b[n - 1 - i]) for i in range(n)]
    for i, j in _BITONIC16:
        hi = jnp.maximum(vals[i], vals[j])
        lo = jnp.minimum(vals[i], vals[j])
        vals[i], vals[j] = hi, lo
    return vals


def _top16_vi_of_128(load_key):
    groups = []
    for gi in range(N_KEYS // PEER_TOPK):
        items = [(load_key(gi * PEER_TOPK + t), jnp.full((SUBLANES, LANES), float(gi * PEER_TOPK + t), F32))
                 for t in range(PEER_TOPK)]
        groups.append(_sort16_vi(items))
    while len(groups) > 1:
        groups = [_merge_top16_vi(groups[2 * t], groups[2 * t + 1]) for t in range(len(groups) // 2)]
    return groups[0]


def _top16_v_of_128(load_key):
    groups = [_sort16_v([load_key(gi * PEER_TOPK + t) for t in range(PEER_TOPK)])
              for gi in range(N_KEYS // PEER_TOPK)]
    while len(groups) > 1:
        groups = [_merge_top16_v(groups[2 * t], groups[2 * t + 1]) for t in range(len(groups) // 2)]
    return groups[0]


def _top16_is_distinct(load_key, vals):
    zero = jnp.zeros((SUBLANES, LANES), F32)
    one = jnp.ones((SUBLANES, LANES), F32)
    n_ge = zero
    for k in range(N_KEYS):
        n_ge = n_ge + jnp.where(load_key(k) >= vals[PEER_TOPK - 1], one, zero)
    ok = n_ge == float(PEER_TOPK)
    for r in range(PEER_TOPK - 1):
        ok = ok & (vals[r] > vals[r + 1])
    return jnp.where(ok, one, zero)


def _select_pairs(av, bv):
    zero = jnp.zeros((SUBLANES, LANES), F32)
    one = jnp.ones((SUBLANES, LANES), F32)
    cand = {(r, q): av[r] + bv[q] for (r, q) in _STAIR}
    row0 = [cand[(0, q)] for q in range(PEER_TOPK)]
    rest = [cand[rq] for rq in _STAIR if rq[0] > 0]
    neg_inf = jnp.full((SUBLANES, LANES), -jnp.inf, F32)
    best = row0
    for s in range(0, len(rest), PEER_TOPK):
        grp = rest[s:s + PEER_TOPK]
        grp = grp + [neg_inf] * (PEER_TOPK - len(grp))
        best = _merge_top16_v(best, _sort16_v(grp))
    thr = best[PEER_TOPK - 1]
    zsum = one
    for jj in range(1, PEER_TOPK):
        zsum = zsum + jnp.exp(best[jj] - best[0])
    inv_z_half = 0.5 / zsum

    n_gt = zero
    for rq in _STAIR:
        n_gt = n_gt + jnp.where(cand[rq] > thr, one, zero)
    need = float(PEER_TOPK) - n_gt
    cnt = zero
    n_row = [zero] * PEER_TOPK
    for (r, q) in _STAIR:
        c = cand[(r, q)]
        eq = c == thr
        take = (c > thr) | (eq & (cnt < need))
        cnt = cnt + jnp.where(eq, one, zero)
        n_row[r] = n_row[r] + jnp.where(take, one, zero)
    return n_row, inv_z_half


def _match_count(keys, probe, n_row):
    n_k = jnp.zeros((SUBLANES, LANES), F32)
    for r in reversed(range(PEER_TOPK)):
        n_k = jnp.where(keys[r] == probe, n_row[r], n_k)
    return n_k


def _match_rank(probe, rows):
    rk = jnp.full(probe.shape, float(PEER_TOPK), F32)
    for q in reversed(range(PEER_TOPK)):
        rk = jnp.where(rows[q] == probe, jnp.full_like(rk, float(q)), rk)
    return rk


def _peer_topk_kernel(xn_ref, wqt_ref, k0_ref, k1_ref, r1_ref, e1_ref, n_ref, c_ref, s0_ref, s1_ref):
    tb = xn_ref.shape[0]
    half_w = PEER_HEADS * PEER_HALF
    qt = _dot_nt(wqt_ref[...], xn_ref[...]).astype(BF16)
    s0 = _dot(k0_ref[...], qt[0:half_w])
    s1 = _dot(k1_ref[...], qt[half_w:2 * half_w])
    for g in range(tb // LANES):
        s0_ref[g] = s0[:, g * LANES:(g + 1) * LANES]
        s1_ref[g] = s1[:, g * LANES:(g + 1) * LANES]

    sub_iota = lax.broadcasted_iota(jnp.int32, (_ROWS16, LANES), 0).astype(F32)

    def slab(gi, carry):
        ls = pl.ds(pl.multiple_of(gi * LANES, LANES), LANES)
        load0 = lambda k: s0_ref[gi, pl.ds(k * SUBLANES, SUBLANES), :]
        load1 = lambda k: s1_ref[gi, pl.ds(k * SUBLANES, SUBLANES), :]
        load1_head = lambda h, kb: s1_ref[gi, pl.ds(kb * _ROWS16 * SUBLANES + h, _ROWS16, stride=SUBLANES), :]

        av = _top16_v_of_128(load0)
        bv = _top16_v_of_128(load1)
        n_row, inv_z_half = _select_pairs(av, bv)
        distinct = _top16_is_distinct(load0, av) * _top16_is_distinct(load1, bv)
        has_tie = jnp.min(distinct) < 0.5

        def emit(keys0, probe0, keys1, probe1):
            for k in range(N_KEYS):
                rs = pl.ds(k * SUBLANES, SUBLANES)
                s0k = load0(k)
                n_ref[gi, rs, :] = _match_count(keys0, probe0(k, s0k), n_row)
                c_ref[gi, rs, :] = jnp.exp(s0k - av[0]) * inv_z_half
            for h in range(PEER_HEADS):
                rows = [jnp.broadcast_to(keys1[q][h:h + 1, :], (_ROWS16, LANES)) for q in range(PEER_TOPK)]
                b0h = jnp.broadcast_to(bv[0][h:h + 1, :], (_ROWS16, LANES))
                for kb in range(N_KEYS // _ROWS16):
                    rs = pl.ds(h * N_KEYS + kb * _ROWS16, _ROWS16)
                    tile = load1_head(h, kb)
                    r1_ref[rs, ls] = _match_rank(probe1(kb, tile), rows).astype(BF16)
                    e1_ref[rs, ls] = jnp.exp(tile - b0h).astype(BF16)

        @pl.when(jnp.logical_not(has_tie))
        def _():
            emit(av, lambda k, s0k: s0k, bv, lambda kb, tile: tile)

        @pl.when(has_tie)
        def _():
            a = _top16_vi_of_128(load0)
            b = _top16_vi_of_128(load1)
            emit([it[1] for it in a], lambda k, s0k: float(k),
                 [it[1] for it in b], lambda kb, tile: sub_iota + float(kb * _ROWS16))

        return carry

    lax.fori_loop(0, tb // LANES, slab, 0)


def _peer_topk(xn2, wqt, k0, k1, tb):
    T = xn2.shape[0]
    rows = PEER_HEADS * N_KEYS
    kspec = pl.BlockSpec((rows, PEER_HEADS * PEER_HALF), lambda i: (0, 0))
    ospec = pl.BlockSpec((rows, tb), lambda i: (0, i))
    slab_spec = pl.BlockSpec((tb // LANES, rows, LANES), lambda i: (i, 0, 0))
    return pl.pallas_call(
        _peer_topk_kernel,
        grid=(T // tb,),
        in_specs=[pl.BlockSpec((tb, D_MODEL), lambda i: (i, 0)),
                  pl.BlockSpec((D_MODEL, D_MODEL), lambda i: (0, 0)),
                  kspec, kspec],
        out_specs=[ospec, ospec, slab_spec, slab_spec],
        out_shape=[jax.ShapeDtypeStruct((rows, T), BF16),
                   jax.ShapeDtypeStruct((rows, T), BF16),
                   jax.ShapeDtypeStruct((T // LANES, rows, LANES), F32),
                   jax.ShapeDtypeStruct((T // LANES, rows, LANES), F32)],
        scratch_shapes=[pltpu.VMEM((tb // LANES, rows, LANES), F32),
                        pltpu.VMEM((tb // LANES, rows, LANES), F32)],
        compiler_params=pltpu.CompilerParams(dimension_semantics=("arbitrary",),
                                             vmem_limit_bytes=VMEM_LIMIT_BYTES),
        name="peer_topk",
    )(xn2, wqt, k0, k1)


_SUB_E = 512
_W_TILE = 256


def _peer_dense_kernel(xn_ref, h_ref, r1_ref, e1_ref, n_ref, c_ref, u_next_ref, u_first_ref,
                       vt_ref, o_ref, acc_ref, hm_ref, at_ref):
    c = pl.program_id(1)
    tb = xn_ref.shape[0]
    n_sub = vt_ref.shape[0]
    i0_per_sub = _SUB_E // N_KEYS
    xn = xn_ref[...]

    @pl.when(c == 0)
    def _():
        acc_ref[...] = jnp.zeros_like(acc_ref)
        at_ref[0] = _dot_nt(u_first_ref[...], xn)

    tw = min(tb, _W_TILE)
    zero16 = jnp.zeros((_ROWS16, tw), BF16)
    n_j = N_KEYS // _ROWS16

    def gated_acts(sub, slot, out_ref):
        for ii in range(i0_per_sub):
            i0 = (c * n_sub + sub) * i0_per_sub + ii
            for lt in range(tb // tw):
                ls = slice(lt * tw, (lt + 1) * tw)
                wacc = [None] * n_j
                for h in range(PEER_HEADS):
                    rowi = pl.ds(i0 * PEER_HEADS + h, SUBLANES, stride=0)
                    slabs = range(lt * tw // LANES, (lt + 1) * tw // LANES)
                    n8 = jnp.concatenate([n_ref[g, rowi, :] for g in slabs], axis=1)
                    c8 = jnp.concatenate([c_ref[g, rowi, :] for g in slabs], axis=1)
                    nb = jnp.concatenate([n8, n8], axis=0).astype(BF16)
                    cb = jnp.concatenate([c8, c8], axis=0).astype(BF16)
                    for j in range(n_j):
                        rs = slice(h * N_KEYS + j * _ROWS16, h * N_KEYS + (j + 1) * _ROWS16)
                        term = jnp.where(r1_ref[rs, ls] < nb, e1_ref[rs, ls] * cb, zero16)
                        wacc[j] = term if wacc[j] is None else wacc[j] + term
                for j in range(n_j):
                    er = slice(ii * N_KEYS + j * _ROWS16, ii * N_KEYS + (j + 1) * _ROWS16)
                    a16 = at_ref[slot, er, ls]
                    gl = a16 * (1.0 + lax.erf(a16 * 0.7071067811865476))
                    out_ref[er, ls] = gl.astype(BF16) * wacc[j]

    for sub in range(n_sub):
        par = sub % 2
        at_ref[1 - par] = _dot_nt(u_next_ref[sub * _SUB_E:(sub + 1) * _SUB_E, :], xn)
        if sub > 0:
            acc_ref[...] += _dot(vt_ref[sub - 1], hm_ref[1 - par])
        gated_acts(sub, par, hm_ref.at[par])
    acc_ref[...] += _dot(vt_ref[n_sub - 1], hm_ref[(n_sub - 1) % 2])

    @pl.when(c == pl.num_programs(1) - 1)
    def _():
        o_ref[...] = h_ref[...] + acc_ref[...].T


def _peer_dense(xn2, h, r1, e1, n_sel, c_gate, u16, vt_slabs, tb, ec):
    T = xn2.shape[0]
    rows = PEER_HEADS * N_KEYS
    n_sub = ec // _SUB_E
    assert n_sub >= 2
    tspec = pl.BlockSpec((rows, tb), lambda i, c: (0, i))
    slab_spec = pl.BlockSpec((tb // LANES, rows, LANES), lambda i, c: (i, 0, 0))
    u_next = jnp.roll(u16, -_SUB_E, axis=0)
    return pl.pallas_call(
        _peer_dense_kernel,
        grid=(T // tb, N_EXPERTS // ec),
        in_specs=[pl.BlockSpec((tb, D_MODEL), lambda i, c: (i, 0)),
                  pl.BlockSpec((tb, D_MODEL), lambda i, c: (i, 0)),
                  tspec, tspec, slab_spec, slab_spec,
                  pl.BlockSpec((ec, D_MODEL), lambda i, c: (c, 0)),
                  pl.BlockSpec((_SUB_E, D_MODEL), lambda i, c: (0, 0)),
                  pl.BlockSpec((n_sub, D_MODEL, _SUB_E), lambda i, c: (c, 0, 0))],
        out_specs=pl.BlockSpec((tb, D_MODEL), lambda i, c: (i, 0)),
        out_shape=jax.ShapeDtypeStruct((T, D_MODEL), F32),
        scratch_shapes=[pltpu.VMEM((D_MODEL, tb), F32),
                        pltpu.VMEM((2, _SUB_E, tb), BF16),
                        pltpu.VMEM((2, _SUB_E, tb), F32)],
        compiler_params=pltpu.CompilerParams(dimension_semantics=("arbitrary", "arbitrary"),
                                             vmem_limit_bytes=VMEM_LIMIT_BYTES),
        name="peer_dense",
    )(xn2, h, r1, e1, n_sel, c_gate, u_next, u16, vt_slabs)


def _block_diag_ones(width, blk):
    idx = np.arange(width) // blk
    return jnp.asarray((idx[:, None] == idx[None, :]).astype(np.float32), dtype=BF16)


def _layer(x, norm_mix_g, w_in, att_q_norm_g, att_k_norm_g, att_sinks, att_out_norm_g, dn_conv_w,
           dn_a_log, dn_dt_bias, dn_out_norm_g, w_out, norm_ffn_g, peer_w_q, peer_sub_keys, peer_u, peer_v):
    B, S, _ = x.shape
    T = B * S
    x2 = x.reshape(T, D_MODEL)

    o_q, o_k, o_v = 0, ATT_WIDTH, ATT_WIDTH + ATT_KV_WIDTH
    o_dn = ATT_WIDTH + 2 * ATT_KV_WIDTH
    o_a = o_dn + 3 * DN_WIDTH
    o_b = o_a + DN_HEADS
    o_z = o_b + DN_HEADS
    pad = jnp.zeros((D_MODEL, LANES - DN_HEADS), w_in.dtype)
    w_all = jnp.concatenate([w_in[:, o_q:o_dn], w_in[:, o_dn:o_a], w_in[:, o_z:o_z + DN_WIDTH],
                             w_in[:, o_a:o_b], pad, w_in[:, o_b:o_z], pad], axis=1).astype(BF16)
    del o_k, o_v

    def lane_pad(vec):
        return jnp.concatenate([vec.astype(F32), jnp.zeros((LANES - vec.shape[0],), F32)]).reshape(1, LANES)

    att_in, dn_in, z_in, ab_in = _inproj(x2, norm_mix_g.reshape(1, D_MODEL), w_all, tm=512)

    att_n = _attention(
        att_in, att_sinks.astype(F32),
        jnp.tile(att_q_norm_g.astype(F32), ATT_HEADS).reshape(1, ATT_WIDTH),
        jnp.tile(att_k_norm_g.astype(F32), ATT_KV_HEADS).reshape(1, ATT_KV_WIDTH),
        att_out_norm_g.astype(F32).reshape(1, ATT_WIDTH),
        _block_diag_ones(ATT_WIDTH, ATT_HEAD_DIM), _block_diag_ones(ATT_KV_WIDTH, ATT_HEAD_DIM),
        seq=S, tq=1024)

    tri = jnp.asarray(np.tril(np.ones((GDN_CHUNK, GDN_CHUNK), np.float32)))
    dn_o = _gdn(dn_in, ab_in, z_in, dn_conv_w.astype(F32),
                lane_pad(-jnp.exp(dn_a_log.astype(F32))), lane_pad(dn_dt_bias),
                dn_out_norm_g.astype(F32).reshape(1, DN_HEAD_DIM), tri, batch=B, seq=S, nb=4)

    h, xn2 = _outproj(x2, att_n, dn_o, w_out.astype(BF16), norm_ffn_g.reshape(1, D_MODEL), tm=512)

    wqt = (peer_w_q.T.reshape(PEER_HEADS, 2, PEER_HALF, D_MODEL).transpose(1, 0, 2, 3)
           .reshape(D_MODEL, D_MODEL).astype(BF16))
    eye = jnp.eye(PEER_HEADS, dtype=peer_sub_keys.dtype)
    kd = jnp.einsum('hpkc,hg->pkhgc', peer_sub_keys, eye)
    k_km = kd.reshape(2, N_KEYS * PEER_HEADS, PEER_HEADS * PEER_HALF).astype(BF16)
    r1, e1, n_sel, c_gate = _peer_topk(xn2, wqt, k_km[0], k_km[1], tb=512)

    vt_slabs = (peer_v.reshape(N_EXPERTS // _SUB_E, _SUB_E, D_MODEL).transpose(0, 2, 1).astype(BF16))
    out = _peer_dense(xn2, h, r1, e1, n_sel, c_gate, peer_u.astype(BF16), vt_slabs, tb=512, ec=2048)
    return out.reshape(B, S, D_MODEL)


def kernel(x, norm_mix_g, w_in, att_q_norm_g, att_k_norm_g, att_sinks, att_out_norm_g, dn_conv_w, dn_a_log, dn_dt_bias, dn_out_norm_g, w_out, norm_ffn_g, peer_w_q, peer_sub_keys, peer_u, peer_v):
    h = x
    for l in range(norm_mix_g.shape[0]):
        h = _layer(h, norm_mix_g[l], w_in[l], att_q_norm_g[l], att_k_norm_g[l], att_sinks[l],
                   att_out_norm_g[l], dn_conv_w[l], dn_a_log[l], dn_dt_bias[l], dn_out_norm_g[l],
                   w_out[l], norm_ffn_g[l], peer_w_q[l], peer_sub_keys[l], peer_u[l], peer_v[l])
    return h
```

```python
import functools

import numpy as np
import jax
import jax.numpy as jnp
from jax import lax
from jax.experimental import pallas as pl
from jax.experimental.pallas import tpu as pltpu

F32 = jnp.float32
BF16 = jnp.bfloat16

D_MODEL = 1024
ATT_HEADS = 8
ATT_KV_HEADS = 2
ATT_HEAD_DIM = 64
ATT_WIDTH = 512
ATT_KV_WIDTH = 128
ATT_BLOCK = 128
DN_HEADS = 4
DN_HEAD_DIM = 128
DN_WIDTH = 512
CONV_WIDTH = 4
PEER_HEADS = 8
N_KEYS = 128
N_EXPERTS = N_KEYS * N_KEYS
PEER_HALF = 64
PEER_TOPK = 16
EPS = 1e-6

LANES = 128
SUBLANES = 8
VMEM_LIMIT_BYTES = 56 * 1024 * 1024

_C_ATT = 0
_C_DN = 768
_C_Z = _C_DN + 3 * DN_WIDTH
_C_AB = _C_Z + DN_WIDTH
_C_END = _C_AB + 2 * LANES

GDN_CHUNK = 128
_ROWS16 = 2 * SUBLANES


def _dot(a, b):
    return jnp.dot(a, b, preferred_element_type=F32)


def _dot_nt(a, b):
    return lax.dot_general(a, b, (((1,), (1,)), ((), ())), preferred_element_type=F32)


def _split2(a):
    hi = a.astype(BF16)
    lo = (a - hi.astype(F32)).astype(BF16)
    return hi, lo


def _dot3s(a_split, b_split):
    ah, al = a_split
    bh, bl = b_split
    return _dot(ah, bh) + (_dot(ah, bl) + _dot(al, bh))


def _dot3(a, b):
    return _dot3s(_split2(a), _split2(b))


def _inproj_kernel(x_ref, g_ref, w_ref, att_ref, dn_ref, z_ref, ab_ref):
    x = x_ref[...]
    ms = jnp.mean(x * x, axis=-1, keepdims=True)
    xn = (x * lax.rsqrt(ms + EPS) * g_ref[...]).astype(BF16)
    att_ref[...] = _dot(xn, w_ref[:, _C_ATT:_C_DN])
    dn_ref[...] = _dot(xn, w_ref[:, _C_DN:_C_Z])
    z_ref[...] = _dot(xn, w_ref[:, _C_Z:_C_AB])
    ab_ref[...] = _dot(xn, w_ref[:, _C_AB:_C_END])


def _inproj(x2, g, w_all, tm):
    T = x2.shape[0]
    return pl.pallas_call(
        _inproj_kernel,
        grid=(T // tm,),
        in_specs=[pl.BlockSpec((tm, D_MODEL), lambda i: (i, 0)),
                  pl.BlockSpec((1, D_MODEL), lambda i: (0, 0)),
                  pl.BlockSpec((D_MODEL, _C_END), lambda i: (0, 0))],
        out_specs=[pl.BlockSpec((tm, 768), lambda i: (i, 0)),
                   pl.BlockSpec((tm, 3 * DN_WIDTH), lambda i: (i, 0)),
                   pl.BlockSpec((tm, DN_WIDTH), lambda i: (i, 0)),
                   pl.BlockSpec((tm, 2 * LANES), lambda i: (i, 0))],
        out_shape=[jax.ShapeDtypeStruct((T, 768), F32),
                   jax.ShapeDtypeStruct((T, 3 * DN_WIDTH), F32),
                   jax.ShapeDtypeStruct((T, DN_WIDTH), F32),
                   jax.ShapeDtypeStruct((T, 2 * LANES), F32)],
        compiler_params=pltpu.CompilerParams(dimension_semantics=("arbitrary",),
                                             vmem_limit_bytes=VMEM_LIMIT_BYTES),
        name="inproj",
    )(x2, g, w_all)


def _head_rms(t, bd, g):
    hi, lo = _split2(t * t)
    ss = _dot(hi, bd) + _dot(lo, bd)
    return t * lax.rsqrt(ss * (1.0 / ATT_HEAD_DIM) + EPS) * g


def _attn_kernel(sink_ref, cur_ref, prev_ref, gq_ref, gk_ref, go_ref, bdq_ref, bdk_ref, o_ref,
                 *, tiles_per_seq):
    i = pl.program_id(0)
    tq = cur_ref.shape[0]
    nblk = tq // ATT_BLOCK
    first = (i % tiles_per_seq) == 0

    q = cur_ref[:, 0:ATT_WIDTH]
    k = cur_ref[:, ATT_WIDTH:ATT_WIDTH + ATT_KV_WIDTH]
    v = cur_ref[:, ATT_WIDTH + ATT_KV_WIDTH:ATT_WIDTH + 2 * ATT_KV_WIDTH]
    kp = prev_ref[:, 0:ATT_KV_WIDTH]
    vp = prev_ref[:, ATT_KV_WIDTH:2 * ATT_KV_WIDTH]

    qn = (_head_rms(q, bdq_ref[...], gq_ref[...]) * (ATT_HEAD_DIM ** -0.5)).astype(BF16)
    kn = jnp.concatenate([_head_rms(kp, bdk_ref[...], gk_ref[...]),
                          _head_rms(k, bdk_ref[...], gk_ref[...])], axis=0)
    vf = jnp.concatenate([vp, v], axis=0)

    lane = lax.broadcasted_iota(jnp.int32, kn.shape, 1)
    left = lane < ATT_HEAD_DIM
    zero = jnp.zeros_like(kn)
    k0l = jnp.where(left, kn, zero)
    k1r = jnp.where(left, zero, kn)
    v0l = jnp.where(left, vf, zero)
    v1r = jnp.where(left, zero, vf)
    kl = (k0l.astype(BF16), pltpu.roll(k1r, ATT_HEAD_DIM, 1).astype(BF16))
    kr = (pltpu.roll(k0l, ATT_HEAD_DIM, 1).astype(BF16), k1r.astype(BF16))
    vl = (v0l.astype(BF16), pltpu.roll(v1r, ATT_HEAD_DIM, 1).astype(BF16))
    vr = (pltpu.roll(v0l, ATT_HEAD_DIM, 1).astype(BF16), v1r.astype(BF16))

    qi = lax.broadcasted_iota(jnp.int32, (ATT_BLOCK, 2 * ATT_BLOCK), 0)
    kj = lax.broadcasted_iota(jnp.int32, (ATT_BLOCK, 2 * ATT_BLOCK), 1)
    rel = qi + ATT_BLOCK - kj
    in_window = (rel >= 0) & (rel < ATT_BLOCK)
    first_key = jnp.where(first, ATT_BLOCK, 0)
    neg_inf = jnp.full((ATT_BLOCK, 2 * ATT_BLOCK), -jnp.inf, F32)

    def softmax_rows(s, sink):
        m = jnp.maximum(jnp.max(s, axis=-1, keepdims=True), sink)
        p = jnp.exp(s - m)
        den = jnp.sum(p, axis=-1, keepdims=True) + jnp.exp(sink - m)
        return (p * (1.0 / den)).astype(BF16)

    for j in range(nblk):
        rows = slice(j * ATT_BLOCK, (j + 1) * ATT_BLOCK)
        krows = slice(j * ATT_BLOCK, (j + 2) * ATT_BLOCK)
        if j == 0:
            mask = in_window & (kj >= first_key)
        else:
            mask = in_window
        pairs = []
        for c in range(ATT_KV_HEADS):
            qe = jnp.concatenate([qn[rows, (2 * c) * LANES:(2 * c + 1) * LANES],
                                  qn[rows, (2 * c + 1) * LANES:(2 * c + 2) * LANES]], axis=0)
            s_even = _dot_nt(qe, kl[c][krows])
            s_odd = _dot_nt(qe, kr[c][krows])
            for half in range(2):
                hr = slice(half * ATT_BLOCK, (half + 1) * ATT_BLOCK)
                h_even = 4 * c + 2 * half
                p_e = softmax_rows(jnp.where(mask, s_even[hr], neg_inf), sink_ref[h_even])
                p_o = softmax_rows(jnp.where(mask, s_odd[hr], neg_inf), sink_ref[h_even + 1])
                pairs.append(_dot(p_e, vl[c][krows]) + _dot(p_o, vr[c][krows]))
        att = jnp.concatenate(pairs, axis=1)
        ms = jnp.mean(att * att, axis=-1, keepdims=True)
        o_ref[rows, :] = (att * lax.rsqrt(ms + EPS) * go_ref[...]).astype(BF16)


def _attention(att_in, sinks, gq, gk, go, bdq, bdk, seq, tq):
    T = att_in.shape[0]
    tiles_per_seq = seq // tq
    blk_per_tile = tq // ATT_BLOCK
    kv_col_blk = ATT_WIDTH // (2 * ATT_KV_WIDTH)
    return pl.pallas_call(
        functools.partial(_attn_kernel, tiles_per_seq=tiles_per_seq),
        grid_spec=pltpu.PrefetchScalarGridSpec(
            num_scalar_prefetch=1,
            grid=(T // tq,),
            in_specs=[pl.BlockSpec((tq, 768), lambda i, s: (i, 0)),
                      pl.BlockSpec((ATT_BLOCK, 2 * ATT_KV_WIDTH),
                                   lambda i, s: (jnp.maximum(i * blk_per_tile - 1, 0), kv_col_blk)),
                      pl.BlockSpec((1, ATT_WIDTH), lambda i, s: (0, 0)),
                      pl.BlockSpec((1, ATT_KV_WIDTH), lambda i, s: (0, 0)),
                      pl.BlockSpec((1, ATT_WIDTH), lambda i, s: (0, 0)),
                      pl.BlockSpec((ATT_WIDTH, ATT_WIDTH), lambda i, s: (0, 0)),
                      pl.BlockSpec((ATT_KV_WIDTH, ATT_KV_WIDTH), lambda i, s: (0, 0))],
            out_specs=pl.BlockSpec((tq, ATT_WIDTH), lambda i, s: (i, 0)),
        ),
        out_shape=jax.ShapeDtypeStruct((T, ATT_WIDTH), BF16),
        compiler_params=pltpu.CompilerParams(dimension_semantics=("arbitrary",),
                                             vmem_limit_bytes=VMEM_LIMIT_BYTES),
        name="swa_attention",
    )(sinks, att_in, att_in, gq, gk, go, bdq, bdk)


def _unit_lower_inverses(m_lows, row, col):
    n = m_lows[0].shape[0]
    zero = jnp.zeros_like(m_lows[0])
    zero16 = jnp.zeros(zero.shape, BF16)
    eye = jnp.where(row == col, jnp.ones_like(zero), zero)
    same = (row >> 3) == (col >> 3)
    m_sp = [_split2(m) for m in m_lows]
    nm_sp = [(jnp.where(same, -mh, zero16), jnp.where(same, -ml, zero16)) for mh, ml in m_sp]
    ps = [eye + jnp.where(same, -m, zero) for m in m_lows]
    n2_sp = [_split2(_dot3s(nm, nm)) for nm in nm_sp]
    ps = [p + _dot3s(_split2(p), n2) for p, n2 in zip(ps, n2_sp)]
    n4_sp = [_split2(_dot3s(n2, n2)) for n2 in n2_sp]
    ps = [p + _dot3s(_split2(p), n4) for p, n4 in zip(ps, n4_sp)]
    shift = 3
    while (1 << shift) < n:
        same_next = (row >> (shift + 1)) == (col >> (shift + 1))
        lower_left = same_next & jnp.logical_not(same)
        p_sp = [_split2(p) for p in ps]
        ts = [_dot3s(psp, (jnp.where(lower_left, mh, zero16), jnp.where(lower_left, ml, zero16)))
              for psp, (mh, ml) in zip(p_sp, m_sp)]
        ps = [p - _dot3s(_split2(t), psp) for p, t, psp in zip(ps, ts, p_sp)]
        same = same_next
        shift += 1
    return ps


def _silu(v):
    return v * (1.0 / (1.0 + jnp.exp(-v)))


def _gdn_kernel(dn_ref, ab_ref, z_ref, cw_ref, nega_ref, dtb_ref, gn_ref, tri_ref, o_ref,
                state_ref, carry_ref):
    t = pl.program_id(1)
    C = GDN_CHUNK
    nb = dn_ref.shape[0]

    @pl.when(t == 0)
    def _():
        state_ref[...] = jnp.zeros_like(state_ref)
        carry_ref[...] = jnp.zeros_like(carry_ref)

    row = lax.broadcasted_iota(jnp.int32, (C, C), 0)
    col = lax.broadcasted_iota(jnp.int32, (C, C), 1)
    causal = row >= col
    strict = row > col
    zero_cc = jnp.zeros((C, C), F32)
    cw = cw_ref[...]
    row8 = lax.broadcasted_iota(jnp.int32, (SUBLANES, 3 * DN_WIDTH), 0)

    ids, qs, ks, vbs, kbes, decs, egs, kdecs, cds = [], [], [], [], [], [], [], [], []
    for bb in range(nb):
        x = dn_ref[bb]
        tail = carry_ref[bb]
        carry_ref[bb] = x[C - SUBLANES:C, :]
        y = x * cw[CONV_WIDTH - 1:CONV_WIDTH, :]
        for back in range(1, CONV_WIDTH):
            xr = pltpu.roll(x, back, 0)
            top = jnp.where(row8 < back, pltpu.roll(tail, back, 0), xr[0:SUBLANES])
            shifted = jnp.concatenate([top, xr[SUBLANES:]], axis=0)
            y = y + shifted * cw[CONV_WIDTH - 1 - back:CONV_WIDTH - back, :]
        y = _silu(y)

        ab = ab_ref[bb]
        sp_in = ab[:, 0:LANES] + dtb_ref[...]
        softplus = jnp.maximum(sp_in, 0.0) + jnp.log1p(jnp.exp(-jnp.abs(sp_in)))
        g_all = nega_ref[...] * softplus
        beta_all = 1.0 / (1.0 + jnp.exp(-ab[:, LANES:2 * LANES]))
        gc_all = _dot3(tri_ref[...], g_all)
        gc_t = gc_all.T

        for h in range(DN_HEADS):
            qh = y[:, h * DN_HEAD_DIM:(h + 1) * DN_HEAD_DIM]
            kh = y[:, DN_WIDTH + h * DN_HEAD_DIM:DN_WIDTH + (h + 1) * DN_HEAD_DIM]
            vh = y[:, 2 * DN_WIDTH + h * DN_HEAD_DIM:2 * DN_WIDTH + (h + 1) * DN_HEAD_DIM]
            qh = qh * lax.rsqrt(jnp.sum(qh * qh, axis=-1, keepdims=True) + EPS) * (DN_HEAD_DIM ** -0.5)
            kh = kh * lax.rsqrt(jnp.sum(kh * kh, axis=-1, keepdims=True) + EPS)
            gcol = gc_all[:, h:h + 1]
            grow = gc_t[h:h + 1, :]
            bcol = beta_all[:, h:h + 1]
            glast = gc_all[C - 1:C, h:h + 1]
            eg = jnp.exp(gcol)
            kb = kh * bcol
            ids.append((bb, h))
            qs.append(qh)
            ks.append(kh.astype(BF16))
            vbs.append(jnp.concatenate([vh * bcol, kb * eg], axis=1).astype(BF16))
            kbes.append(kb.astype(BF16))
            decs.append(jnp.where(causal, jnp.exp(jnp.where(causal, gcol - grow, zero_cc)), zero_cc))
            egs.append(eg)
            kdecs.append((kh * jnp.exp(glast - gcol)).T.astype(BF16))
            cds.append(jnp.exp(glast))

    n = len(ids)
    m_lows = [jnp.where(strict, _dot_nt(kbes[i], ks[i]) * decs[i], zero_cc) for i in range(n)]
    qks = [jnp.where(causal, _dot_nt(qs[i].astype(BF16), ks[i]) * decs[i], zero_cc).astype(BF16)
           for i in range(n)]
    tinvs = _unit_lower_inverses(m_lows, row, col)
    uws = [_dot(tinvs[i].astype(BF16), vbs[i]) for i in range(n)]
    s_olds = [state_ref[bb, h] for (bb, h) in ids]
    s16s = [s.astype(BF16) for s in s_olds]
    v16s = [(uws[i][:, 0:DN_HEAD_DIM] - _dot(uws[i][:, DN_HEAD_DIM:].astype(BF16), s16s[i])).astype(BF16)
            for i in range(n)]
    outs = [_dot((qs[i] * egs[i]).astype(BF16), s16s[i]) + _dot(qks[i], v16s[i]) for i in range(n)]
    for i, (bb, h) in enumerate(ids):
        state_ref[bb, h] = s_olds[i] * cds[i] + _dot(kdecs[i], v16s[i])

    for i, (bb, h) in enumerate(ids):
        hs = slice(h * DN_HEAD_DIM, (h + 1) * DN_HEAD_DIM)
        o = outs[i]
        ms = jnp.mean(o * o, axis=-1, keepdims=True)
        o_ref[bb, :, hs] = (o * lax.rsqrt(ms + EPS) * gn_ref[...] * _silu(z_ref[bb, :, hs])).astype(BF16)


def _gdn(dn_in, ab, z, conv_w, neg_a, dt_bias, gnorm, tri, batch, seq, nb):
    C = GDN_CHUNK
    nc = seq // C
    blk3 = lambda w: pl.BlockSpec((nb, C, w), lambda bi, t: (bi, t, 0))
    const = lambda shape: pl.BlockSpec(shape, lambda bi, t: (0, 0))
    out = pl.pallas_call(
        _gdn_kernel,
        grid=(batch // nb, nc),
        in_specs=[blk3(3 * DN_WIDTH), blk3(2 * LANES), blk3(DN_WIDTH),
                  const((CONV_WIDTH, 3 * DN_WIDTH)), const((1, LANES)), const((1, LANES)),
                  const((1, DN_HEAD_DIM)), const((C, C))],
        out_specs=blk3(DN_WIDTH),
        out_shape=jax.ShapeDtypeStruct((batch, seq, DN_WIDTH), BF16),
        scratch_shapes=[pltpu.VMEM((nb, DN_HEADS, DN_HEAD_DIM, DN_HEAD_DIM), F32),
                        pltpu.VMEM((nb, SUBLANES, 3 * DN_WIDTH), F32)],
        compiler_params=pltpu.CompilerParams(dimension_semantics=("arbitrary", "arbitrary"),
                                             vmem_limit_bytes=VMEM_LIMIT_BYTES),
        name="gated_deltanet",
    )(dn_in.reshape(batch, seq, -1), ab.reshape(batch, seq, -1), z.reshape(batch, seq, -1),
      conv_w, neg_a, dt_bias, gnorm, tri)
    return out.reshape(batch * seq, DN_WIDTH)


def _outproj_kernel(x_ref, att_ref, dn_ref, wo_ref, g_ref, h_ref, xn_ref):
    h = x_ref[...] + (_dot(att_ref[...], wo_ref[0:ATT_WIDTH, :])
                      + _dot(dn_ref[...], wo_ref[ATT_WIDTH:ATT_WIDTH + DN_WIDTH, :]))
    h_ref[...] = h
    ms = jnp.mean(h * h, axis=-1, keepdims=True)
    xn_ref[...] = (h * lax.rsqrt(ms + EPS) * g_ref[...]).astype(BF16)


def _outproj(x2, att_n, dn_o, w_out, g, tm):
    T = x2.shape[0]
    return pl.pallas_call(
        _outproj_kernel,
        grid=(T // tm,),
        in_specs=[pl.BlockSpec((tm, D_MODEL), lambda i: (i, 0)),
                  pl.BlockSpec((tm, ATT_WIDTH), lambda i: (i, 0)),
                  pl.BlockSpec((tm, DN_WIDTH), lambda i: (i, 0)),
                  pl.BlockSpec((ATT_WIDTH + DN_WIDTH, D_MODEL), lambda i: (0, 0)),
                  pl.BlockSpec((1, D_MODEL), lambda i: (0, 0))],
        out_specs=[pl.BlockSpec((tm, D_MODEL), lambda i: (i, 0)),
                   pl.BlockSpec((tm, D_MODEL), lambda i: (i, 0))],
        out_shape=[jax.ShapeDtypeStruct((T, D_MODEL), F32),
                   jax.ShapeDtypeStruct((T, D_MODEL), BF16)],
        compiler_params=pltpu.CompilerParams(dimension_semantics=("arbitrary",),
                                             vmem_limit_bytes=VMEM_LIMIT_BYTES),
        name="outproj",
    )(x2, att_n, dn_o, w_out, g)


def _oddeven_merge(lo, hi, r):
    step = r * 2
    if step < hi - lo:
        yield from _oddeven_merge(lo, hi, step)
        yield from _oddeven_merge(lo + r, hi, step)
        for i in range(lo + r, hi - r, step):
            yield (i, i + r)
    else:
        yield (lo, lo + r)


def _oddeven_sort_net(lo, hi):
    if hi - lo >= 1:
        mid = lo + (hi - lo) // 2
        yield from _oddeven_sort_net(lo, mid)
        yield from _oddeven_sort_net(mid + 1, hi)
        yield from _oddeven_merge(lo, hi, 1)


_SORT16 = tuple(_oddeven_sort_net(0, PEER_TOPK - 1))
_BITONIC16 = tuple((i, i + d) for d in (8, 4, 2, 1) for i in range(PEER_TOPK) if (i & d) == 0)
_STAIR = tuple((r, q) for r in range(PEER_TOPK) for q in range(PEER_TOPK) if (r + 1) * (q + 1) <= PEER_TOPK)


def _ce_vi(a, b):
    va, ia = a
    vb, ib = b
    a_first = (va > vb) | ((va == vb) & (ia < ib))
    return ((jnp.where(a_first, va, vb), jnp.where(a_first, ia, ib)),
            (jnp.where(a_first, vb, va), jnp.where(a_first, ib, ia)))


def _first_vi(a, b):
    va, ia = a
    vb, ib = b
    a_first = (va > vb) | ((va == vb) & (ia < ib))
    return (jnp.where(a_first, va, vb), jnp.where(a_first, ia, ib))


def _sort16_vi(items):
    items = list(items)
    for i, j in _SORT16:
        items[i], items[j] = _ce_vi(items[i], items[j])
    return items


def _merge_top16_vi(a, b):
    n = PEER_TOPK
    items = [_first_vi(a[i], b[n - 1 - i]) for i in range(n)]
    for i, j in _BITONIC16:
        items[i], items[j] = _ce_vi(items[i], items[j])
    return items


def _sort16_v(vals):
    vals = list(vals)
    for i, j in _SORT16:
        hi = jnp.maximum(vals[i], vals[j])
        lo = jnp.minimum(vals[i], vals[j])
        vals[i], vals[j] = hi, lo
    return vals


def _merge_top16_v(a, b):
    n = PEER_TOPK
    vals = [jnp.maximum(a[i], b[n - 1 - i]) for i in range(n)]
    for i, j in _BITONIC16:
        hi = jnp.maximum(vals[i], vals[j])
        lo = jnp.minimum(vals[i], vals[j])
        vals[i], vals[j] = hi, lo
    return vals


def _top16_vi_of_128(load_key):
    groups = []
    for gi in range(N_KEYS // PEER_TOPK):
        items = [(load_key(gi * PEER_TOPK + t), jnp.full((SUBLANES, LANES), float(gi * PEER_TOPK + t), F32))
                 for t in range(PEER_TOPK)]
        groups.append(_sort16_vi(items))
    while len(groups) > 1:
        groups = [_merge_top16_vi(groups[2 * t], groups[2 * t + 1]) for t in range(len(groups) // 2)]
    return groups[0]


def _top16_v_of_128(load_key):
    groups = [_sort16_v([load_key(gi * PEER_TOPK + t) for t in range(PEER_TOPK)])
              for gi in range(N_KEYS // PEER_TOPK)]
    while len(groups) > 1:
        groups = [_merge_top16_v(groups[2 * t], groups[2 * t + 1]) for t in range(len(groups) // 2)]
    return groups[0]


def _top16_is_distinct(load_key, vals):
    zero = jnp.zeros((SUBLANES, LANES), F32)
    one = jnp.ones((SUBLANES, LANES), F32)
    n_ge = zero
    for k in range(N_KEYS):
        n_ge = n_ge + jnp.where(load_key(k) >= vals[PEER_TOPK - 1], one, zero)
    ok = n_ge == float(PEER_TOPK)
    for r in range(PEER_TOPK - 1):
        ok = ok & (vals[r] > vals[r + 1])
    return jnp.where(ok, one, zero)


def _select_pairs(av, bv):
    zero = jnp.zeros((SUBLANES, LANES), F32)
    one = jnp.ones((SUBLANES, LANES), F32)
    cand = {(r, q): av[r] + bv[q] for (r, q) in _STAIR}
    row0 = [cand[(0, q)] for q in range(PEER_TOPK)]
    rest = [cand[rq] for rq in _STAIR if rq[0] > 0]
    neg_inf = jnp.full((SUBLANES, LANES), -jnp.inf, F32)
    best = row0
    for s in range(0, len(rest), PEER_TOPK):
        grp = rest[s:s + PEER_TOPK]
        grp = grp + [neg_inf] * (PEER_TOPK - len(grp))
        best = _merge_top16_v(best, _sort16_v(grp))
    thr = best[PEER_TOPK - 1]
    zsum = one
    for jj in range(1, PEER_TOPK):
        zsum = zsum + jnp.exp(best[jj] - best[0])
    inv_z_half = 0.5 / zsum

    n_gt = zero
    for rq in _STAIR:
        n_gt = n_gt + jnp.where(cand[rq] > thr, one, zero)
    need = float(PEER_TOPK) - n_gt
    cnt = zero
    n_row = [zero] * PEER_TOPK
    for (r, q) in _STAIR:
        c = cand[(r, q)]
        eq = c == thr
        take = (c > thr) | (eq & (cnt < need))
        cnt = cnt + jnp.where(eq, one, zero)
        n_row[r] = n_row[r] + jnp.where(take, one, zero)
    return n_row, inv_z_half


def _match_count(keys, probe, n_row):
    n_k = jnp.zeros((SUBLANES, LANES), F32)
    for r in reversed(range(PEER_TOPK)):
        n_k = jnp.where(keys[r] == probe, n_row[r], n_k)
    return n_k


def _match_rank(probe, rows):
    rk = jnp.full(probe.shape, float(PEER_TOPK), F32)
    for q in reversed(range(PEER_TOPK)):
        rk = jnp.where(rows[q] == probe, jnp.full_like(rk, float(q)), rk)
    return rk


def _peer_topk_kernel(xn_ref, wqt_ref, k0_ref, k1_ref, r1_ref, e1_ref, n_ref, c_ref, s0_ref, s1_ref):
    tb = xn_ref.shape[0]
    half_w = PEER_HEADS * PEER_HALF
    qt = _dot_nt(wqt_ref[...], xn_ref[...]).astype(BF16)
    s0 = _dot(k0_ref[...], qt[0:half_w])
    s1 = _dot(k1_ref[...], qt[half_w:2 * half_w])
    for g in range(tb // LANES):
        s0_ref[g] = s0[:, g * LANES:(g + 1) * LANES]
        s1_ref[g] = s1[:, g * LANES:(g + 1) * LANES]

    sub_iota = lax.broadcasted_iota(jnp.int32, (_ROWS16, LANES), 0).astype(F32)

    def slab(gi, carry):
        ls = pl.ds(pl.multiple_of(gi * LANES, LANES), LANES)
        load0 = lambda k: s0_ref[gi, pl.ds(k * SUBLANES, SUBLANES), :]
        load1 = lambda k: s1_ref[gi, pl.ds(k * SUBLANES, SUBLANES), :]
        load1_head = lambda h, kb: s1_ref[gi, pl.ds(kb * _ROWS16 * SUBLANES + h, _ROWS16, stride=SUBLANES), :]

        av = _top16_v_of_128(load0)
        bv = _top16_v_of_128(load1)
        n_row, inv_z_half = _select_pairs(av, bv)
        distinct = _top16_is_distinct(load0, av) * _top16_is_distinct(load1, bv)
        has_tie = jnp.min(distinct) < 0.5

        def emit(keys0, probe0, keys1, probe1):
            for k in range(N_KEYS):
                rs = pl.ds(k * SUBLANES, SUBLANES)
                s0k = load0(k)
                n_ref[gi, rs, :] = _match_count(keys0, probe0(k, s0k), n_row)
                c_ref[gi, rs, :] = jnp.exp(s0k - av[0]) * inv_z_half
            for h in range(PEER_HEADS):
                rows = [jnp.broadcast_to(keys1[q][h:h + 1, :], (_ROWS16, LANES)) for q in range(PEER_TOPK)]
                b0h = jnp.broadcast_to(bv[0][h:h + 1, :], (_ROWS16, LANES))
                for kb in range(N_KEYS // _ROWS16):
                    rs = pl.ds(h * N_KEYS + kb * _ROWS16, _ROWS16)
                    tile = load1_head(h, kb)
                    r1_ref[rs, ls] = _match_rank(probe1(kb, tile), rows).astype(BF16)
                    e1_ref[rs, ls] = jnp.exp(tile - b0h).astype(BF16)

        @pl.when(jnp.logical_not(has_tie))
        def _():
            emit(av, lambda k, s0k: s0k, bv, lambda kb, tile: tile)

        @pl.when(has_tie)
        def _():
            a = _top16_vi_of_128(load0)
            b = _top16_vi_of_128(load1)
            emit([it[1] for it in a], lambda k, s0k: float(k),
                 [it[1] for it in b], lambda kb, tile: sub_iota + float(kb * _ROWS16))

        return carry

    lax.fori_loop(0, tb // LANES, slab, 0)


def _peer_topk(xn2, wqt, k0, k1, tb):
    T = xn2.shape[0]
    rows = PEER_HEADS * N_KEYS
    kspec = pl.BlockSpec((rows, PEER_HEADS * PEER_HALF), lambda i: (0, 0))
    ospec = pl.BlockSpec((rows, tb), lambda i: (0, i))
    slab_spec = pl.BlockSpec((tb // LANES, rows, LANES), lambda i: (i, 0, 0))
    return pl.pallas_call(
        _peer_topk_kernel,
        grid=(T // tb,),
        in_specs=[pl.BlockSpec((tb, D_MODEL), lambda i: (i, 0)),
                  pl.BlockSpec((D_MODEL, D_MODEL), lambda i: (0, 0)),
                  kspec, kspec],
        out_specs=[ospec, ospec, slab_spec, slab_spec],
        out_shape=[jax.ShapeDtypeStruct((rows, T), BF16),
                   jax.ShapeDtypeStruct((rows, T), BF16),
                   jax.ShapeDtypeStruct((T // LANES, rows, LANES), F32),
                   jax.ShapeDtypeStruct((T // LANES, rows, LANES), F32)],
        scratch_shapes=[pltpu.VMEM((tb // LANES, rows, LANES), F32),
                        pltpu.VMEM((tb // LANES, rows, LANES), F32)],
        compiler_params=pltpu.CompilerParams(dimension_semantics=("arbitrary",),
                                             vmem_limit_bytes=VMEM_LIMIT_BYTES),
        name="peer_topk",
    )(xn2, wqt, k0, k1)


_SUB_E = 512
_W_TILE = 256


def _peer_dense_kernel(xn_ref, h_ref, r1_ref, e1_ref, n_ref, c_ref, u_next_ref, u_first_ref,
                       vt_ref, o_ref, acc_ref, hm_ref, at_ref):
    c = pl.program_id(1)
    tb = xn_ref.shape[0]
    n_sub = vt_ref.shape[0]
    i0_per_sub = _SUB_E // N_KEYS
    xn = xn_ref[...]

    def gelu2(a):
        return (a * (1.0 + lax.erf(a * 0.7071067811865476))).astype(BF16)

    @pl.when(c == 0)
    def _():
        acc_ref[...] = jnp.zeros_like(acc_ref)
        at_ref[0] = gelu2(_dot_nt(u_first_ref[...], xn))

    tw = min(tb, _W_TILE)
    zero16 = jnp.zeros((_ROWS16, tw), BF16)
    n_j = N_KEYS // _ROWS16

    def gated_acts(sub, slot, out_ref):
        for ii in range(i0_per_sub):
            i0 = (c * n_sub + sub) * i0_per_sub + ii
            for lt in range(tb // tw):
                ls = slice(lt * tw, (lt + 1) * tw)
                wacc = [None] * n_j
                for h in range(PEER_HEADS):
                    rowi = pl.ds(i0 * PEER_HEADS + h, SUBLANES, stride=0)
                    slabs = range(lt * tw // LANES, (lt + 1) * tw // LANES)
                    n8 = jnp.concatenate([n_ref[g, rowi, :] for g in slabs], axis=1)
                    c8 = jnp.concatenate([c_ref[g, rowi, :] for g in slabs], axis=1)
                    nb = jnp.concatenate([n8, n8], axis=0).astype(BF16)
                    cb = jnp.concatenate([c8, c8], axis=0).astype(BF16)
                    for j in range(n_j):
                        rs = slice(h * N_KEYS + j * _ROWS16, h * N_KEYS + (j + 1) * _ROWS16)
                        term = jnp.where(r1_ref[rs, ls] < nb, e1_ref[rs, ls] * cb, zero16)
                        wacc[j] = term if wacc[j] is None else wacc[j] + term
                for j in range(n_j):
                    er = slice(ii * N_KEYS + j * _ROWS16, ii * N_KEYS + (j + 1) * _ROWS16)
                    out_ref[er, ls] = at_ref[slot, er, ls] * wacc[j]

    for sub in range(n_sub):
        par = sub % 2
        at_ref[1 - par] = gelu2(_dot_nt(u_next_ref[sub * _SUB_E:(sub + 1) * _SUB_E, :], xn))
        if sub > 0:
            acc_ref[...] += _dot(vt_ref[sub - 1], hm_ref[1 - par])
        gated_acts(sub, par, hm_ref.at[par])
    acc_ref[...] += _dot(vt_ref[n_sub - 1], hm_ref[(n_sub - 1) % 2])

    @pl.when(c == pl.num_programs(1) - 1)
    def _():
        o_ref[...] = h_ref[...] + acc_ref[...].T


def _peer_dense(xn2, h, r1, e1, n_sel, c_gate, u16, vt_slabs, tb, ec):
    T = xn2.shape[0]
    rows = PEER_HEADS * N_KEYS
    n_sub = ec // _SUB_E
    assert n_sub >= 2
    tspec = pl.BlockSpec((rows, tb), lambda i, c: (0, i))
    slab_spec = pl.BlockSpec((tb // LANES, rows, LANES), lambda i, c: (i, 0, 0))
    u_next = jnp.roll(u16, -_SUB_E, axis=0)
    return pl.pallas_call(
        _peer_dense_kernel,
        grid=(T // tb, N_EXPERTS // ec),
        in_specs=[pl.BlockSpec((tb, D_MODEL), lambda i, c: (i, 0)),
                  pl.BlockSpec((tb, D_MODEL), lambda i, c: (i, 0)),
                  tspec, tspec, slab_spec, slab_spec,
                  pl.BlockSpec((ec, D_MODEL), lambda i, c: (c, 0)),
                  pl.BlockSpec((_SUB_E, D_MODEL), lambda i, c: (0, 0)),
                  pl.BlockSpec((n_sub, D_MODEL, _SUB_E), lambda i, c: (c, 0, 0))],
        out_specs=pl.BlockSpec((tb, D_MODEL), lambda i, c: (i, 0)),
        out_shape=jax.ShapeDtypeStruct((T, D_MODEL), F32),
        scratch_shapes=[pltpu.VMEM((D_MODEL, tb), F32),
                        pltpu.VMEM((2, _SUB_E, tb), BF16),
                        pltpu.VMEM((2, _SUB_E, tb), BF16)],
        compiler_params=pltpu.CompilerParams(dimension_semantics=("arbitrary", "arbitrary"),
                                             vmem_limit_bytes=VMEM_LIMIT_BYTES),
        name="peer_dense",
    )(xn2, h, r1, e1, n_sel, c_gate, u_next, u16, vt_slabs)


def _block_diag_ones(width, blk):
    idx = np.arange(width) // blk
    return jnp.asarray((idx[:, None] == idx[None, :]).astype(np.float32), dtype=BF16)


def _layer(x, norm_mix_g, w_in, att_q_norm_g, att_k_norm_g, att_sinks, att_out_norm_g, dn_conv_w,
           dn_a_log, dn_dt_bias, dn_out_norm_g, w_out, norm_ffn_g, peer_w_q, peer_sub_keys, peer_u, peer_v):
    B, S, _ = x.shape
    T = B * S
    x2 = x.reshape(T, D_MODEL)

    o_q, o_k, o_v = 0, ATT_WIDTH, ATT_WIDTH + ATT_KV_WIDTH
    o_dn = ATT_WIDTH + 2 * ATT_KV_WIDTH
    o_a = o_dn + 3 * DN_WIDTH
    o_b = o_a + DN_HEADS
    o_z = o_b + DN_HEADS
    pad = jnp.zeros((D_MODEL, LANES - DN_HEADS), w_in.dtype)
    w_all = jnp.concatenate([w_in[:, o_q:o_dn], w_in[:, o_dn:o_a], w_in[:, o_z:o_z + DN_WIDTH],
                             w_in[:, o_a:o_b], pad, w_in[:, o_b:o_z], pad], axis=1).astype(BF16)
    del o_k, o_v

    def lane_pad(vec):
        return jnp.concatenate([vec.astype(F32), jnp.zeros((LANES - vec.shape[0],), F32)]).reshape(1, LANES)

    att_in, dn_in, z_in, ab_in = _inproj(x2, norm_mix_g.reshape(1, D_MODEL), w_all, tm=512)

    att_n = _attention(
        att_in, att_sinks.astype(F32),
        jnp.tile(att_q_norm_g.astype(F32), ATT_HEADS).reshape(1, ATT_WIDTH),
        jnp.tile(att_k_norm_g.astype(F32), ATT_KV_HEADS).reshape(1, ATT_KV_WIDTH),
        att_out_norm_g.astype(F32).reshape(1, ATT_WIDTH),
        _block_diag_ones(ATT_WIDTH, ATT_HEAD_DIM), _block_diag_ones(ATT_KV_WIDTH, ATT_HEAD_DIM),
        seq=S, tq=1024)

    tri = jnp.asarray(np.tril(np.ones((GDN_CHUNK, GDN_CHUNK), np.float32)))
    dn_o = _gdn(dn_in, ab_in, z_in, dn_conv_w.astype(F32),
                lane_pad(-jnp.exp(dn_a_log.astype(F32))), lane_pad(dn_dt_bias),
                dn_out_norm_g.astype(F32).reshape(1, DN_HEAD_DIM), tri, batch=B, seq=S, nb=4)

    h, xn2 = _outproj(x2, att_n, dn_o, w_out.astype(BF16), norm_ffn_g.reshape(1, D_MODEL), tm=512)

    wqt = (peer_w_q.T.reshape(PEER_HEADS, 2, PEER_HALF, D_MODEL).transpose(1, 0, 2, 3)
           .reshape(D_MODEL, D_MODEL).astype(BF16))
    eye = jnp.eye(PEER_HEADS, dtype=peer_sub_keys.dtype)
    kd = jnp.einsum('hpkc,hg->pkhgc', peer_sub_keys, eye)
    k_km = kd.reshape(2, N_KEYS * PEER_HEADS, PEER_HEADS * PEER_HALF).astype(BF16)
    r1, e1, n_sel, c_gate = _peer_topk(xn2, wqt, k_km[0], k_km[1], tb=512)

    vt_slabs = (peer_v.reshape(N_EXPERTS // _SUB_E, _SUB_E, D_MODEL).transpose(0, 2, 1).astype(BF16))
    out = _peer_dense(xn2, h, r1, e1, n_sel, c_gate, peer_u.astype(BF16), vt_slabs, tb=512, ec=2048)
    return out.reshape(B, S, D_MODEL)


def kernel(x, norm_mix_g, w_in, att_q_norm_g, att_k_norm_g, att_sinks, att_out_norm_g, dn_conv_w, dn_a_log, dn_dt_bias, dn_out_norm_g, w_out, norm_ffn_g, peer_w_q, peer_sub_keys, peer_u, peer_v):
    h = x
    for l in range(norm_mix_g.shape[0]):
        h = _layer(h, norm_mix_g[l], w_in[l], att_q_norm_g[l], att_k_norm_g[l], att_sinks[l],
                   att_out_norm_g[l], dn_conv_w[l], dn_a_log[l], dn_dt_bias[l], dn_out_norm_g[l],
                   w_out[l], norm_ffn_g[l], peer_w_q[l], peer_sub_keys[l], peer_u[l], peer_v[l])
    return h
```

```python
import functools

import numpy as np
import jax
import jax.numpy as jnp
from jax import lax
from jax.experimental import pallas as pl
from jax.experimental.pallas import tpu as pltpu

F32 = jnp.float32
BF16 = jnp.bfloat16

D_MODEL = 1024
ATT_HEADS = 8
ATT_KV_HEADS = 2
ATT_HEAD_DIM = 64
ATT_WIDTH = 512
ATT_KV_WIDTH = 128
ATT_BLOCK = 128
DN_HEADS = 4
DN_HEAD_DIM = 128
DN_WIDTH = 512
CONV_WIDTH = 4
PEER_HEADS = 8
N_KEYS = 128
N_EXPERTS = N_KEYS * N_KEYS
PEER_HALF = 64
PEER_TOPK = 16
EPS = 1e-6

LANES = 128
SUBLANES = 8
VMEM_LIMIT_BYTES = 56 * 1024 * 1024

_C_ATT = 0
_C_DN = 768
_C_Z = _C_DN + 3 * DN_WIDTH
_C_AB = _C_Z + DN_WIDTH
_C_END = _C_AB + 2 * LANES

GDN_CHUNK = 128
_ROWS16 = 2 * SUBLANES


def _dot(a, b):
    return jnp.dot(a, b, preferred_element_type=F32)


def _dot_nt(a, b):
    return lax.dot_general(a, b, (((1,), (1,)), ((), ())), preferred_element_type=F32)


def _split2(a):
    hi = a.astype(BF16)
    lo = (a - hi.astype(F32)).astype(BF16)
    return hi, lo


def _dot3s(a_split, b_split):
    ah, al = a_split
    bh, bl = b_split
    return _dot(ah, bh) + (_dot(ah, bl) + _dot(al, bh))


def _dot3(a, b):
    return _dot3s(_split2(a), _split2(b))


def _inproj_kernel(x_ref, g_ref, w_ref, att_ref, dn_ref, z_ref, ab_ref):
    x = x_ref[...]
    ms = jnp.mean(x * x, axis=-1, keepdims=True)
    xn = (x * lax.rsqrt(ms + EPS) * g_ref[...]).astype(BF16)
    att_ref[...] = _dot(xn, w_ref[:, _C_ATT:_C_DN])
    dn_ref[...] = _dot(xn, w_ref[:, _C_DN:_C_Z])
    z_ref[...] = _dot(xn, w_ref[:, _C_Z:_C_AB])
    ab_ref[...] = _dot(xn, w_ref[:, _C_AB:_C_END])


def _inproj(x2, g, w_all, tm):
    T = x2.shape[0]
    return pl.pallas_call(
        _inproj_kernel,
        grid=(T // tm,),
        in_specs=[pl.BlockSpec((tm, D_MODEL), lambda i: (i, 0)),
                  pl.BlockSpec((1, D_MODEL), lambda i: (0, 0)),
                  pl.BlockSpec((D_MODEL, _C_END), lambda i: (0, 0))],
        out_specs=[pl.BlockSpec((tm, 768), lambda i: (i, 0)),
                   pl.BlockSpec((tm, 3 * DN_WIDTH), lambda i: (i, 0)),
                   pl.BlockSpec((tm, DN_WIDTH), lambda i: (i, 0)),
                   pl.BlockSpec((tm, 2 * LANES), lambda i: (i, 0))],
        out_shape=[jax.ShapeDtypeStruct((T, 768), F32),
                   jax.ShapeDtypeStruct((T, 3 * DN_WIDTH), F32),
                   jax.ShapeDtypeStruct((T, DN_WIDTH), F32),
                   jax.ShapeDtypeStruct((T, 2 * LANES), F32)],
        compiler_params=pltpu.CompilerParams(dimension_semantics=("arbitrary",),
                                             vmem_limit_bytes=VMEM_LIMIT_BYTES),
        name="inproj",
    )(x2, g, w_all)


def _head_rms(t, bd, g):
    hi, lo = _split2(t * t)
    ss = _dot(hi, bd) + _dot(lo, bd)
    return t * lax.rsqrt(ss * (1.0 / ATT_HEAD_DIM) + EPS) * g


def _attn_kernel(sink_ref, cur_ref, prev_ref, gq_ref, gk_ref, go_ref, bdq_ref, bdk_ref, o_ref,
                 *, tiles_per_seq):
    i = pl.program_id(0)
    tq = cur_ref.shape[0]
    nblk = tq // ATT_BLOCK
    first = (i % tiles_per_seq) == 0

    q = cur_ref[:, 0:ATT_WIDTH]
    k = cur_ref[:, ATT_WIDTH:ATT_WIDTH + ATT_KV_WIDTH]
    v = cur_ref[:, ATT_WIDTH + ATT_KV_WIDTH:ATT_WIDTH + 2 * ATT_KV_WIDTH]
    kp = prev_ref[:, 0:ATT_KV_WIDTH]
    vp = prev_ref[:, ATT_KV_WIDTH:2 * ATT_KV_WIDTH]

    qn = (_head_rms(q, bdq_ref[...], gq_ref[...]) * (ATT_HEAD_DIM ** -0.5)).astype(BF16)
    kn = jnp.concatenate([_head_rms(kp, bdk_ref[...], gk_ref[...]),
                          _head_rms(k, bdk_ref[...], gk_ref[...])], axis=0)
    vf = jnp.concatenate([vp, v], axis=0)

    lane = lax.broadcasted_iota(jnp.int32, kn.shape, 1)
    left = lane < ATT_HEAD_DIM
    zero = jnp.zeros_like(kn)
    k0l = jnp.where(left, kn, zero)
    k1r = jnp.where(left, zero, kn)
    v0l = jnp.where(left, vf, zero)
    v1r = jnp.where(left, zero, vf)
    kl = (k0l.astype(BF16), pltpu.roll(k1r, ATT_HEAD_DIM, 1).astype(BF16))
    kr = (pltpu.roll(k0l, ATT_HEAD_DIM, 1).astype(BF16), k1r.astype(BF16))
    vl = (v0l.astype(BF16), pltpu.roll(v1r, ATT_HEAD_DIM, 1).astype(BF16))
    vr = (pltpu.roll(v0l, ATT_HEAD_DIM, 1).astype(BF16), v1r.astype(BF16))

    qi = lax.broadcasted_iota(jnp.int32, (ATT_BLOCK, 2 * ATT_BLOCK), 0)
    kj = lax.broadcasted_iota(jnp.int32, (ATT_BLOCK, 2 * ATT_BLOCK), 1)
    rel = qi + ATT_BLOCK - kj
    in_window = (rel >= 0) & (rel < ATT_BLOCK)
    first_key = jnp.where(first, ATT_BLOCK, 0)
    neg_inf = jnp.full((ATT_BLOCK, 2 * ATT_BLOCK), -jnp.inf, F32)

    def softmax_rows(s, sink):
        m = jnp.maximum(jnp.max(s, axis=-1, keepdims=True), sink)
        p = jnp.exp(s - m)
        den = jnp.sum(p, axis=-1, keepdims=True) + jnp.exp(sink - m)
        return (p * (1.0 / den)).astype(BF16)

    for j in range(nblk):
        rows = slice(j * ATT_BLOCK, (j + 1) * ATT_BLOCK)
        krows = slice(j * ATT_BLOCK, (j + 2) * ATT_BLOCK)
        if j == 0:
            mask = in_window & (kj >= first_key)
        else:
            mask = in_window
        pairs = []
        for c in range(ATT_KV_HEADS):
            qe = jnp.concatenate([qn[rows, (2 * c) * LANES:(2 * c + 1) * LANES],
                                  qn[rows, (2 * c + 1) * LANES:(2 * c + 2) * LANES]], axis=0)
            s_even = _dot_nt(qe, kl[c][krows])
            s_odd = _dot_nt(qe, kr[c][krows])
            for half in range(2):
                hr = slice(half * ATT_BLOCK, (half + 1) * ATT_BLOCK)
                h_even = 4 * c + 2 * half
                p_e = softmax_rows(jnp.where(mask, s_even[hr], neg_inf), sink_ref[h_even])
                p_o = softmax_rows(jnp.where(mask, s_odd[hr], neg_inf), sink_ref[h_even + 1])
                pairs.append(_dot(p_e, vl[c][krows]) + _dot(p_o, vr[c][krows]))
        att = jnp.concatenate(pairs, axis=1)
        ms = jnp.mean(att * att, axis=-1, keepdims=True)
        o_ref[rows, :] = (att * lax.rsqrt(ms + EPS) * go_ref[...]).astype(BF16)


def _attention(att_in, sinks, gq, gk, go, bdq, bdk, seq, tq):
    T = att_in.shape[0]
    tiles_per_seq = seq // tq
    blk_per_tile = tq // ATT_BLOCK
    kv_col_blk = ATT_WIDTH // (2 * ATT_KV_WIDTH)
    return pl.pallas_call(
        functools.partial(_attn_kernel, tiles_per_seq=tiles_per_seq),
        grid_spec=pltpu.PrefetchScalarGridSpec(
            num_scalar_prefetch=1,
            grid=(T // tq,),
            in_specs=[pl.BlockSpec((tq, 768), lambda i, s: (i, 0)),
                      pl.BlockSpec((ATT_BLOCK, 2 * ATT_KV_WIDTH),
                                   lambda i, s: (jnp.maximum(i * blk_per_tile - 1, 0), kv_col_blk)),
                      pl.BlockSpec((1, ATT_WIDTH), lambda i, s: (0, 0)),
                      pl.BlockSpec((1, ATT_KV_WIDTH), lambda i, s: (0, 0)),
                      pl.BlockSpec((1, ATT_WIDTH), lambda i, s: (0, 0)),
                      pl.BlockSpec((ATT_WIDTH, ATT_WIDTH), lambda i, s: (0, 0)),
                      pl.BlockSpec((ATT_KV_WIDTH, ATT_KV_WIDTH), lambda i, s: (0, 0))],
            out_specs=pl.BlockSpec((tq, ATT_WIDTH), lambda i, s: (i, 0)),
        ),
        out_shape=jax.ShapeDtypeStruct((T, ATT_WIDTH), BF16),
        compiler_params=pltpu.CompilerParams(dimension_semantics=("arbitrary",),
                                             vmem_limit_bytes=VMEM_LIMIT_BYTES),
        name="swa_attention",
    )(sinks, att_in, att_in, gq, gk, go, bdq, bdk)


def _unit_lower_inverses(m_lows, row, col):
    n = m_lows[0].shape[0]
    zero = jnp.zeros_like(m_lows[0])
    zero16 = jnp.zeros(zero.shape, BF16)
    eye = jnp.where(row == col, jnp.ones_like(zero), zero)
    same = (row >> 3) == (col >> 3)
    m_sp = [_split2(m) for m in m_lows]
    nm_sp = [(jnp.where(same, -mh, zero16), jnp.where(same, -ml, zero16)) for mh, ml in m_sp]
    ps = [eye + jnp.where(same, -m, zero) for m in m_lows]
    n2_sp = [_split2(_dot3s(nm, nm)) for nm in nm_sp]
    ps = [p + _dot3s(_split2(p), n2) for p, n2 in zip(ps, n2_sp)]
    n4_sp = [_split2(_dot3s(n2, n2)) for n2 in n2_sp]
    ps = [p + _dot3s(_split2(p), n4) for p, n4 in zip(ps, n4_sp)]
    shift = 3
    while (1 << shift) < n:
        same_next = (row >> (shift + 1)) == (col >> (shift + 1))
        lower_left = same_next & jnp.logical_not(same)
        p_sp = [_split2(p) for p in ps]
        ts = [_dot3s(psp, (jnp.where(lower_left, mh, zero16), jnp.where(lower_left, ml, zero16)))
              for psp, (mh, ml) in zip(p_sp, m_sp)]
        ps = [p - _dot3s(_split2(t), psp) for p, t, psp in zip(ps, ts, p_sp)]
        same = same_next
        shift += 1
    return ps


def _silu(v):
    return v * (1.0 / (1.0 + jnp.exp(-v)))


def _gdn_kernel(dn_ref, ab_ref, z_ref, cw_ref, nega_ref, dtb_ref, gn_ref, tri_ref, o_ref,
                state_ref, carry_ref):
    t = pl.program_id(1)
    C = GDN_CHUNK
    nb = dn_ref.shape[0]

    @pl.when(t == 0)
    def _():
        state_ref[...] = jnp.zeros_like(state_ref)
        carry_ref[...] = jnp.zeros_like(carry_ref)

    row = lax.broadcasted_iota(jnp.int32, (C, C), 0)
    col = lax.broadcasted_iota(jnp.int32, (C, C), 1)
    causal = row >= col
    strict = row > col
    zero_cc = jnp.zeros((C, C), F32)
    cw = cw_ref[...]
    row8 = lax.broadcasted_iota(jnp.int32, (SUBLANES, 3 * DN_WIDTH), 0)

    ids, qs, ks, vbs, kbes, decs, egs, kdecs, cds = [], [], [], [], [], [], [], [], []
    for bb in range(nb):
        x = dn_ref[bb]
        tail = carry_ref[bb]
        carry_ref[bb] = x[C - SUBLANES:C, :]
        y = x * cw[CONV_WIDTH - 1:CONV_WIDTH, :]
        for back in range(1, CONV_WIDTH):
            xr = pltpu.roll(x, back, 0)
            top = jnp.where(row8 < back, pltpu.roll(tail, back, 0), xr[0:SUBLANES])
            shifted = jnp.concatenate([top, xr[SUBLANES:]], axis=0)
            y = y + shifted * cw[CONV_WIDTH - 1 - back:CONV_WIDTH - back, :]
        y = _silu(y)

        ab = ab_ref[bb]
        sp_in = ab[:, 0:LANES] + dtb_ref[...]
        softplus = jnp.maximum(sp_in, 0.0) + jnp.log1p(jnp.exp(-jnp.abs(sp_in)))
        g_all = nega_ref[...] * softplus
        beta_all = 1.0 / (1.0 + jnp.exp(-ab[:, LANES:2 * LANES]))
        gc_all = _dot3(tri_ref[...], g_all)
        gc_t = gc_all.T

        for h in range(DN_HEADS):
            qh = y[:, h * DN_HEAD_DIM:(h + 1) * DN_HEAD_DIM]
            kh = y[:, DN_WIDTH + h * DN_HEAD_DIM:DN_WIDTH + (h + 1) * DN_HEAD_DIM]
            vh = y[:, 2 * DN_WIDTH + h * DN_HEAD_DIM:2 * DN_WIDTH + (h + 1) * DN_HEAD_DIM]
            qh = qh * lax.rsqrt(jnp.sum(qh * qh, axis=-1, keepdims=True) + EPS) * (DN_HEAD_DIM ** -0.5)
            kh = kh * lax.rsqrt(jnp.sum(kh * kh, axis=-1, keepdims=True) + EPS)
            gcol = gc_all[:, h:h + 1]
            grow = gc_t[h:h + 1, :]
            bcol = beta_all[:, h:h + 1]
            glast = gc_all[C - 1:C, h:h + 1]
            eg = jnp.exp(gcol)
            kb = kh * bcol
            ids.append((bb, h))
            qs.append(qh)
            ks.append(kh.astype(BF16))
            vbs.append(jnp.concatenate([vh * bcol, kb * eg], axis=1).astype(BF16))
            kbes.append(kb.astype(BF16))
            decs.append(jnp.where(causal, jnp.exp(jnp.where(causal, gcol - grow, zero_cc)), zero_cc))
            egs.append(eg)
            kdecs.append((kh * jnp.exp(glast - gcol)).T.astype(BF16))
            cds.append(jnp.exp(glast))

    n = len(ids)
    m_lows = [jnp.where(strict, _dot_nt(kbes[i], ks[i]) * decs[i], zero_cc) for i in range(n)]
    qks = [jnp.where(causal, _dot_nt(qs[i].astype(BF16), ks[i]) * decs[i], zero_cc).astype(BF16)
           for i in range(n)]
    tinvs = _unit_lower_inverses(m_lows, row, col)
    uws = [_dot(tinvs[i].astype(BF16), vbs[i]) for i in range(n)]
    s_olds = [state_ref[bb, h] for (bb, h) in ids]
    s16s = [s.astype(BF16) for s in s_olds]
    v16s = [(uws[i][:, 0:DN_HEAD_DIM] - _dot(uws[i][:, DN_HEAD_DIM:].astype(BF16), s16s[i])).astype(BF16)
            for i in range(n)]
    outs = [_dot((qs[i] * egs[i]).astype(BF16), s16s[i]) + _dot(qks[i], v16s[i]) for i in range(n)]
    for i, (bb, h) in enumerate(ids):
        state_ref[bb, h] = s_olds[i] * cds[i] + _dot(kdecs[i], v16s[i])

    for i, (bb, h) in enumerate(ids):
        hs = slice(h * DN_HEAD_DIM, (h + 1) * DN_HEAD_DIM)
        o = outs[i]
        ms = jnp.mean(o * o, axis=-1, keepdims=True)
        o_ref[bb, :, hs] = (o * lax.rsqrt(ms + EPS) * gn_ref[...] * _silu(z_ref[bb, :, hs])).astype(BF16)


def _gdn(dn_in, ab, z, conv_w, neg_a, dt_bias, gnorm, tri, batch, seq, nb):
    C = GDN_CHUNK
    nc = seq // C
    blk3 = lambda w: pl.BlockSpec((nb, C, w), lambda bi, t: (bi, t, 0))
    const = lambda shape: pl.BlockSpec(shape, lambda bi, t: (0, 0))
    out = pl.pallas_call(
        _gdn_kernel,
        grid=(batch // nb, nc),
        in_specs=[blk3(3 * DN_WIDTH), blk3(2 * LANES), blk3(DN_WIDTH),
                  const((CONV_WIDTH, 3 * DN_WIDTH)), const((1, LANES)), const((1, LANES)),
                  const((1, DN_HEAD_DIM)), const((C, C))],
        out_specs=blk3(DN_WIDTH),
        out_shape=jax.ShapeDtypeStruct((batch, seq, DN_WIDTH), BF16),
        scratch_shapes=[pltpu.VMEM((nb, DN_HEADS, DN_HEAD_DIM, DN_HEAD_DIM), F32),
                        pltpu.VMEM((nb, SUBLANES, 3 * DN_WIDTH), F32)],
        compiler_params=pltpu.CompilerParams(dimension_semantics=("arbitrary", "arbitrary"),
                                             vmem_limit_bytes=VMEM_LIMIT_BYTES),
        name="gated_deltanet",
    )(dn_in.reshape(batch, seq, -1), ab.reshape(batch, seq, -1), z.reshape(batch, seq, -1),
      conv_w, neg_a, dt_bias, gnorm, tri)
    return out.reshape(batch * seq, DN_WIDTH)


def _outproj_kernel(x_ref, att_ref, dn_ref, wo_ref, g_ref, h_ref, xn_ref):
    h = x_ref[...] + (_dot(att_ref[...], wo_ref[0:ATT_WIDTH, :])
                      + _dot(dn_ref[...], wo_ref[ATT_WIDTH:ATT_WIDTH + DN_WIDTH, :]))
    h_ref[...] = h
    ms = jnp.mean(h * h, axis=-1, keepdims=True)
    xn_ref[...] = (h * lax.rsqrt(ms + EPS) * g_ref[...]).astype(BF16)


def _outproj(x2, att_n, dn_o, w_out, g, tm):
    T = x2.shape[0]
    return pl.pallas_call(
        _outproj_kernel,
        grid=(T // tm,),
        in_specs=[pl.BlockSpec((tm, D_MODEL), lambda i: (i, 0)),
                  pl.BlockSpec((tm, ATT_WIDTH), lambda i: (i, 0)),
                  pl.BlockSpec((tm, DN_WIDTH), lambda i: (i, 0)),
                  pl.BlockSpec((ATT_WIDTH + DN_WIDTH, D_MODEL), lambda i: (0, 0)),
                  pl.BlockSpec((1, D_MODEL), lambda i: (0, 0))],
        out_specs=[pl.BlockSpec((tm, D_MODEL), lambda i: (i, 0)),
                   pl.BlockSpec((tm, D_MODEL), lambda i: (i, 0))],
        out_shape=[jax.ShapeDtypeStruct((T, D_MODEL), F32),
                   jax.ShapeDtypeStruct((T, D_MODEL), BF16)],
        compiler_params=pltpu.CompilerParams(dimension_semantics=("arbitrary",),
                                             vmem_limit_bytes=VMEM_LIMIT_BYTES),
        name="outproj",
    )(x2, att_n, dn_o, w_out, g)


def _oddeven_merge(lo, hi, r):
    step = r * 2
    if step < hi - lo:
        yield from _oddeven_merge(lo, hi, step)
        yield from _oddeven_merge(lo + r, hi, step)
        for i in range(lo + r, hi - r, step):
            yield (i, i + r)
    else:
        yield (lo, lo + r)


def _oddeven_sort_net(lo, hi):
    if hi - lo >= 1:
        mid = lo + (hi - lo) // 2
        yield from _oddeven_sort_net(lo, mid)
        yield from _oddeven_sort_net(mid + 1, hi)
        yield from _oddeven_merge(lo, hi, 1)


_SORT16 = tuple(_oddeven_sort_net(0, PEER_TOPK - 1))
_BITONIC16 = tuple((i, i + d) for d in (8, 4, 2, 1) for i in range(PEER_TOPK) if (i & d) == 0)
_STAIR = tuple((r, q) for r in range(PEER_TOPK) for q in range(PEER_TOPK) if (r + 1) * (q + 1) <= PEER_TOPK)


def _ce_vi(a, b):
    va, ia = a
    vb, ib = b
    a_first = (va > vb) | ((va == vb) & (ia < ib))
    return ((jnp.where(a_first, va, vb), jnp.where(a_first, ia, ib)),
            (jnp.where(a_first, vb, va), jnp.where(a_first, ib, ia)))


def _first_vi(a, b):
    va, ia = a
    vb, ib = b
    a_first = (va > vb) | ((va == vb) & (ia < ib))
    return (jnp.where(a_first, va, vb), jnp.where(a_first, ia, ib))


def _sort16_vi(items):
    items = list(items)
    for i, j in _SORT16:
        items[i], items[j] = _ce_vi(items[i], items[j])
    return items


def _merge_top16_vi(a, b):
    n = PEER_TOPK
    items = [_first_vi(a[i], b[n - 1 - i]) for i in range(n)]
    for i, j in _BITONIC16:
        items[i], items[j] = _ce_vi(items[i], items[j])
    return items


def _sort16_v(vals):
    vals = list(vals)
    for i, j in _SORT16:
        hi = jnp.maximum(vals[i], vals[j])
        lo = jnp.minimum(vals[i], vals[j])
        vals[i], vals[j] = hi, lo
    return vals


def _merge_top16_v(a, b):
    n = PEER_TOPK
    vals = [jnp.maximum(a[i], b[n - 1 - i]) for i in range(n)]
    for i, j in _BITONIC16:
        hi = jnp.maximum(vals[i], vals[j])
        lo = jnp.minimum(vals[i], vals[j])
        vals[i], vals[j] = hi, lo
    return vals


def _top16_vi_of_128(load_key):
    groups = []
    for gi in range(N_KEYS // PEER_TOPK):
        items = [(load_key(gi * PEER_TOPK + t), jnp.full((SUBLANES, LANES), float(gi * PEER_TOPK + t), F32))
                 for t in range(PEER_TOPK)]
        groups.append(_sort16_vi(items))
    while len(groups) > 1:
        groups = [_merge_top16_vi(groups[2 * t], groups[2 * t + 1]) for t in range(len(groups) // 2)]
    return groups[0]


def _top16_v_of_128(load_key):
    groups = [_sort16_v([load_key(gi * PEER_TOPK + t) for t in range(PEER_TOPK)])
              for gi in range(N_KEYS // PEER_TOPK)]
    while len(groups) > 1:
        groups = [_merge_top16_v(groups[2 * t], groups[2 * t + 1]) for t in range(len(groups) // 2)]
    return groups[0]


def _top16_is_distinct(load_key, vals):
    zero = jnp.zeros((SUBLANES, LANES), F32)
    one = jnp.ones((SUBLANES, LANES), F32)
    n_ge = zero
    for k in range(N_KEYS):
        n_ge = n_ge + jnp.where(load_key(k) >= vals[PEER_TOPK - 1], one, zero)
    ok = n_ge == float(PEER_TOPK)
    for r in range(PEER_TOPK - 1):
        ok = ok & (vals[r] > vals[r + 1])
    return jnp.where(ok, one, zero)


def _select_pairs(av, bv):
    zero = jnp.zeros((SUBLANES, LANES), F32)
    one = jnp.ones((SUBLANES, LANES), F32)
    cand = {(r, q): av[r] + bv[q] for (r, q) in _STAIR}
    row0 = [cand[(0, q)] for q in range(PEER_TOPK)]
    rest = [cand[rq] for rq in _STAIR if rq[0] > 0]
    neg_inf = jnp.full((SUBLANES, LANES), -jnp.inf, F32)
    best = row0
    for s in range(0, len(rest), PEER_TOPK):
        grp = rest[s:s + PEER_TOPK]
        grp = grp + [neg_inf] * (PEER_TOPK - len(grp))
        best = _merge_top16_v(best, _sort16_v(grp))
    thr = best[PEER_TOPK - 1]
    zsum = one
    for jj in range(1, PEER_TOPK):
        zsum = zsum + jnp.exp(best[jj] - best[0])
    inv_z_half = 0.5 / zsum

    n_gt = zero
    for rq in _STAIR:
        n_gt = n_gt + jnp.where(cand[rq] > thr, one, zero)
    need = float(PEER_TOPK) - n_gt
    cnt = zero
    n_row = [zero] * PEER_TOPK
    for (r, q) in _STAIR:
        c = cand[(r, q)]
        eq = c == thr
        take = (c > thr) | (eq & (cnt < need))
        cnt = cnt + jnp.where(eq, one, zero)
        n_row[r] = n_row[r] + jnp.where(take, one, zero)
    return n_row, inv_z_half


def _match_count(keys, probe, n_row):
    n_k = jnp.zeros((SUBLANES, LANES), F32)
    for r in reversed(range(PEER_TOPK)):
        n_k = jnp.where(keys[r] == probe, n_row[r], n_k)
    return n_k


def _match_rank(probe, rows):
    rk = jnp.full(probe.shape, float(PEER_TOPK), F32)
    for q in reversed(range(PEER_TOPK)):
        rk = jnp.where(rows[q] == probe, jnp.full_like(rk, float(q)), rk)
    return rk


def _peer_topk_kernel(xn_ref, wqt_ref, k0_ref, k1_ref, r1_ref, e1_ref, n_ref, c_ref, s0_ref, s1_ref):
    tb = xn_ref.shape[0]
    half_w = PEER_HEADS * PEER_HALF
    qt = _dot_nt(wqt_ref[...], xn_ref[...]).astype(BF16)
    s0 = _dot(k0_ref[...], qt[0:half_w])
    s1 = _dot(k1_ref[...], qt[half_w:2 * half_w])
    for g in range(tb // LANES):
        s0_ref[g] = s0[:, g * LANES:(g + 1) * LANES]
        s1_ref[g] = s1[:, g * LANES:(g + 1) * LANES]

    sub_iota = lax.broadcasted_iota(jnp.int32, (_ROWS16, LANES), 0).astype(F32)

    def slab(gi, carry):
        ls = pl.ds(pl.multiple_of(gi * LANES, LANES), LANES)
        load0 = lambda k: s0_ref[gi, pl.ds(k * SUBLANES, SUBLANES), :]
        load1 = lambda k: s1_ref[gi, pl.ds(k * SUBLANES, SUBLANES), :]
        load1_head = lambda h, kb: s1_ref[gi, pl.ds(kb * _ROWS16 * SUBLANES + h, _ROWS16, stride=SUBLANES), :]

        av = _top16_v_of_128(load0)
        bv = _top16_v_of_128(load1)
        n_row, inv_z_half = _select_pairs(av, bv)
        distinct = _top16_is_distinct(load0, av) * _top16_is_distinct(load1, bv)
        has_tie = jnp.min(distinct) < 0.5

        def emit(keys0, probe0, keys1, probe1):
            for k in range(N_KEYS):
                rs = pl.ds(k * SUBLANES, SUBLANES)
                s0k = load0(k)
                n_ref[gi, rs, :] = _match_count(keys0, probe0(k, s0k), n_row)
                c_ref[gi, rs, :] = jnp.exp(s0k - av[0]) * inv_z_half
            for h in range(PEER_HEADS):
                rows = [jnp.broadcast_to(keys1[q][h:h + 1, :], (_ROWS16, LANES)) for q in range(PEER_TOPK)]
                b0h = jnp.broadcast_to(bv[0][h:h + 1, :], (_ROWS16, LANES))
                for kb in range(N_KEYS // _ROWS16):
                    rs = pl.ds(h * N_KEYS + kb * _ROWS16, _ROWS16)
                    tile = load1_head(h, kb)
                    r1_ref[rs, ls] = _match_rank(probe1(kb, tile), rows).astype(BF16)
                    e1_ref[rs, ls] = jnp.exp(tile - b0h).astype(BF16)

        @pl.when(jnp.logical_not(has_tie))
        def _():
            emit(av, lambda k, s0k: s0k, bv, lambda kb, tile: tile)

        @pl.when(has_tie)
        def _():
            a = _top16_vi_of_128(load0)
            b = _top16_vi_of_128(load1)
            emit([it[1] for it in a], lambda k, s0k: float(k),
                 [it[1] for it in b], lambda kb, tile: sub_iota + float(kb * _ROWS16))

        return carry

    lax.fori_loop(0, tb // LANES, slab, 0)


def _peer_topk(xn2, wqt, k0, k1, tb):
    T = xn2.shape[0]
    rows = PEER_HEADS * N_KEYS
    kspec = pl.BlockSpec((rows, PEER_HEADS * PEER_HALF), lambda i: (0, 0))
    ospec = pl.BlockSpec((rows, tb), lambda i: (0, i))
    slab_spec = pl.BlockSpec((tb // LANES, rows, LANES), lambda i: (i, 0, 0))
    return pl.pallas_call(
        _peer_topk_kernel,
        grid=(T // tb,),
        in_specs=[pl.BlockSpec((tb, D_MODEL), lambda i: (i, 0)),
                  pl.BlockSpec((D_MODEL, D_MODEL), lambda i: (0, 0)),
                  kspec, kspec],
        out_specs=[ospec, ospec, slab_spec, slab_spec],
        out_shape=[jax.ShapeDtypeStruct((rows, T), BF16),
                   jax.ShapeDtypeStruct((rows, T), BF16),
                   jax.ShapeDtypeStruct((T // LANES, rows, LANES), F32),
                   jax.ShapeDtypeStruct((T // LANES, rows, LANES), F32)],
        scratch_shapes=[pltpu.VMEM((tb // LANES, rows, LANES), F32),
                        pltpu.VMEM((tb // LANES, rows, LANES), F32)],
        compiler_params=pltpu.CompilerParams(dimension_semantics=("arbitrary",),
                                             vmem_limit_bytes=VMEM_LIMIT_BYTES),
        name="peer_topk",
    )(xn2, wqt, k0, k1)


_SUB_E = 512
_W_TILE = 256


def _peer_dense_kernel(xn_ref, xn_ahead_ref, h_ref, r1_ref, e1_ref, n_ref, c_ref, u_ref, u_ahead_ref,
                       vt_ref, o_ref, acc_ref, hm_ref, at_ref):
    i = pl.program_id(0)
    c = pl.program_id(1)
    tb = xn_ref.shape[0]
    n_sub = vt_ref.shape[0]
    i0_per_sub = _SUB_E // N_KEYS
    xn = xn_ref[...]

    def gelu2(a):
        return (a * (1.0 + lax.erf(a * 0.7071067811865476))).astype(BF16)

    @pl.when(c == 0)
    def _():
        acc_ref[...] = jnp.zeros_like(acc_ref)

    @pl.when((c == 0) & (i == 0))
    def _():
        at_ref[0] = gelu2(_dot_nt(u_ref[0:_SUB_E, :], xn))

    tw = min(tb, _W_TILE)
    zero16 = jnp.zeros((_ROWS16, tw), BF16)
    n_j = N_KEYS // _ROWS16

    def gated_acts(sub, slot, out_ref):
        for ii in range(i0_per_sub):
            i0 = (c * n_sub + sub) * i0_per_sub + ii
            for lt in range(tb // tw):
                ls = slice(lt * tw, (lt + 1) * tw)
                wacc = [None] * n_j
                for h in range(PEER_HEADS):
                    rowi = pl.ds(i0 * PEER_HEADS + h, SUBLANES, stride=0)
                    slabs = range(lt * tw // LANES, (lt + 1) * tw // LANES)
                    n8 = jnp.concatenate([n_ref[g, rowi, :] for g in slabs], axis=1)
                    c8 = jnp.concatenate([c_ref[g, rowi, :] for g in slabs], axis=1)
                    nb = jnp.concatenate([n8, n8], axis=0).astype(BF16)
                    cb = jnp.concatenate([c8, c8], axis=0).astype(BF16)
                    for j in range(n_j):
                        rs = slice(h * N_KEYS + j * _ROWS16, h * N_KEYS + (j + 1) * _ROWS16)
                        term = jnp.where(r1_ref[rs, ls] < nb, e1_ref[rs, ls] * cb, zero16)
                        wacc[j] = term if wacc[j] is None else wacc[j] + term
                for j in range(n_j):
                    er = slice(ii * N_KEYS + j * _ROWS16, ii * N_KEYS + (j + 1) * _ROWS16)
                    out_ref[er, ls] = at_ref[slot, er, ls] * wacc[j]

    for sub in range(n_sub):
        par = sub % 2
        if sub + 1 < n_sub:
            at_ref[1 - par] = gelu2(_dot_nt(u_ref[(sub + 1) * _SUB_E:(sub + 2) * _SUB_E, :], xn))
        else:
            at_ref[1 - par] = gelu2(_dot_nt(u_ahead_ref[...], xn_ahead_ref[...]))
        if sub > 0:
            acc_ref[...] += _dot(vt_ref[sub - 1], hm_ref[1 - par])
        gated_acts(sub, par, hm_ref.at[par])
    acc_ref[...] += _dot(vt_ref[n_sub - 1], hm_ref[(n_sub - 1) % 2])

    @pl.when(c == pl.num_programs(1) - 1)
    def _():
        o_ref[...] = h_ref[...] + acc_ref[...].T


def _peer_dense(xn2, h, r1, e1, n_sel, c_gate, u16, vt_slabs, tb, ec):
    T = xn2.shape[0]
    rows = PEER_HEADS * N_KEYS
    n_sub = ec // _SUB_E
    assert n_sub >= 2
    tspec = pl.BlockSpec((rows, tb), lambda i, c: (0, i))
    slab_spec = pl.BlockSpec((tb // LANES, rows, LANES), lambda i, c: (i, 0, 0))
    n_tok_blocks = T // tb
    n_chunks = N_EXPERTS // ec
    n_slabs = N_EXPERTS // _SUB_E

    def ahead_tokens(i, c):
        return (jnp.where(c == n_chunks - 1, jnp.minimum(i + 1, n_tok_blocks - 1), i), 0)

    return pl.pallas_call(
        _peer_dense_kernel,
        grid=(n_tok_blocks, n_chunks),
        in_specs=[pl.BlockSpec((tb, D_MODEL), lambda i, c: (i, 0)),
                  pl.BlockSpec((tb, D_MODEL), ahead_tokens),
                  pl.BlockSpec((tb, D_MODEL), lambda i, c: (i, 0)),
                  tspec, tspec, slab_spec, slab_spec,
                  pl.BlockSpec((ec, D_MODEL), lambda i, c: (c, 0)),
                  pl.BlockSpec((_SUB_E, D_MODEL), lambda i, c: (((c + 1) * n_sub) % n_slabs, 0)),
                  pl.BlockSpec((n_sub, D_MODEL, _SUB_E), lambda i, c: (c, 0, 0))],
        out_specs=pl.BlockSpec((tb, D_MODEL), lambda i, c: (i, 0)),
        out_shape=jax.ShapeDtypeStruct((T, D_MODEL), F32),
        scratch_shapes=[pltpu.VMEM((D_MODEL, tb), F32),
                        pltpu.VMEM((2, _SUB_E, tb), BF16),
                        pltpu.VMEM((2, _SUB_E, tb), BF16)],
        compiler_params=pltpu.CompilerParams(dimension_semantics=("arbitrary", "arbitrary"),
                                             vmem_limit_bytes=VMEM_LIMIT_BYTES),
        name="peer_dense",
    )(xn2, xn2, h, r1, e1, n_sel, c_gate, u16, u16, vt_slabs)


def _block_diag_ones(width, blk):
    idx = np.arange(width) // blk
    return jnp.asarray((idx[:, None] == idx[None, :]).astype(np.float32), dtype=BF16)


def _layer(x, norm_mix_g, w_in, att_q_norm_g, att_k_norm_g, att_sinks, att_out_norm_g, dn_conv_w,
           dn_a_log, dn_dt_bias, dn_out_norm_g, w_out, norm_ffn_g, peer_w_q, peer_sub_keys, peer_u, peer_v):
    B, S, _ = x.shape
    T = B * S
    x2 = x.reshape(T, D_MODEL)

    o_q, o_k, o_v = 0, ATT_WIDTH, ATT_WIDTH + ATT_KV_WIDTH
    o_dn = ATT_WIDTH + 2 * ATT_KV_WIDTH
    o_a = o_dn + 3 * DN_WIDTH
    o_b = o_a + DN_HEADS
    o_z = o_b + DN_HEADS
    pad = jnp.zeros((D_MODEL, LANES - DN_HEADS), w_in.dtype)
    w_all = jnp.concatenate([w_in[:, o_q:o_dn], w_in[:, o_dn:o_a], w_in[:, o_z:o_z + DN_WIDTH],
                             w_in[:, o_a:o_b], pad, w_in[:, o_b:o_z], pad], axis=1).astype(BF16)
    del o_k, o_v

    def lane_pad(vec):
        return jnp.concatenate([vec.astype(F32), jnp.zeros((LANES - vec.shape[0],), F32)]).reshape(1, LANES)

    att_in, dn_in, z_in, ab_in = _inproj(x2, norm_mix_g.reshape(1, D_MODEL), w_all, tm=512)

    att_n = _attention(
        att_in, att_sinks.astype(F32),
        jnp.tile(att_q_norm_g.astype(F32), ATT_HEADS).reshape(1, ATT_WIDTH),
        jnp.tile(att_k_norm_g.astype(F32), ATT_KV_HEADS).reshape(1, ATT_KV_WIDTH),
        att_out_norm_g.astype(F32).reshape(1, ATT_WIDTH),
        _block_diag_ones(ATT_WIDTH, ATT_HEAD_DIM), _block_diag_ones(ATT_KV_WIDTH, ATT_HEAD_DIM),
        seq=S, tq=1024)

    tri = jnp.asarray(np.tril(np.ones((GDN_CHUNK, GDN_CHUNK), np.float32)))
    dn_o = _gdn(dn_in, ab_in, z_in, dn_conv_w.astype(F32),
                lane_pad(-jnp.exp(dn_a_log.astype(F32))), lane_pad(dn_dt_bias),
                dn_out_norm_g.astype(F32).reshape(1, DN_HEAD_DIM), tri, batch=B, seq=S, nb=4)

    h, xn2 = _outproj(x2, att_n, dn_o, w_out.astype(BF16), norm_ffn_g.reshape(1, D_MODEL), tm=512)

    wqt = (peer_w_q.T.reshape(PEER_HEADS, 2, PEER_HALF, D_MODEL).transpose(1, 0, 2, 3)
           .reshape(D_MODEL, D_MODEL).astype(BF16))
    eye = jnp.eye(PEER_HEADS, dtype=peer_sub_keys.dtype)
    kd = jnp.einsum('hpkc,hg->pkhgc', peer_sub_keys, eye)
    k_km = kd.reshape(2, N_KEYS * PEER_HEADS, PEER_HEADS * PEER_HALF).astype(BF16)
    r1, e1, n_sel, c_gate = _peer_topk(xn2, wqt, k_km[0], k_km[1], tb=512)

    vt_slabs = (peer_v.reshape(N_EXPERTS // _SUB_E, _SUB_E, D_MODEL).transpose(0, 2, 1).astype(BF16))
    out = _peer_dense(xn2, h, r1, e1, n_sel, c_gate, peer_u.astype(BF16), vt_slabs, tb=512, ec=2048)
    return out.reshape(B, S, D_MODEL)


def kernel(x, norm_mix_g, w_in, att_q_norm_g, att_k_norm_g, att_sinks, att_out_norm_g, dn_conv_w, dn_a_log, dn_dt_bias, dn_out_norm_g, w_out, norm_ffn_g, peer_w_q, peer_sub_keys, peer_u, peer_v):
    h = x
    for l in range(norm_mix_g.shape[0]):
        h = _layer(h, norm_mix_g[l], w_in[l], att_q_norm_g[l], att_k_norm_g[l], att_sinks[l],
                   att_out_norm_g[l], dn_conv_w[l], dn_a_log[l], dn_dt_bias[l], dn_out_norm_g[l],
                   w_out[l], norm_ffn_g[l], peer_w_q[l], peer_sub_keys[l], peer_u[l], peer_v[l])
    return h
```

```python
import functools

import numpy as np
import jax
import jax.numpy as jnp
from jax import lax
from jax.experimental import pallas as pl
from jax.experimental.pallas import tpu as pltpu

F32 = jnp.float32
BF16 = jnp.bfloat16

D_MODEL = 1024
ATT_HEADS = 8
ATT_KV_HEADS = 2
ATT_HEAD_DIM = 64
ATT_WIDTH = 512
ATT_KV_WIDTH = 128
ATT_BLOCK = 128
DN_HEADS = 4
DN_HEAD_DIM = 128
DN_WIDTH = 512
CONV_WIDTH = 4
PEER_HEADS = 8
N_KEYS = 128
N_EXPERTS = N_KEYS * N_KEYS
PEER_HALF = 64
PEER_TOPK = 16
EPS = 1e-6

LANES = 128
SUBLANES = 8
V7X_VMEM_BYTES = 64 * 1024 * 1024
VMEM_LIMIT_BYTES = V7X_VMEM_BYTES - 8 * 1024 * 1024

PROJ_TOKENS = 512
ATT_TOKENS = 1024
GDN_SEQS = 4
PEER_TOKENS = 512
PEER_CHUNK = 2048

ATT_IN_WIDTH = ATT_WIDTH + 2 * ATT_KV_WIDTH

_C_ATT = 0
_C_DN = ATT_IN_WIDTH
_C_Z = _C_DN + 3 * DN_WIDTH
_C_AB = _C_Z + DN_WIDTH
_C_END = _C_AB + 2 * LANES

GDN_CHUNK = 128
_ROWS16 = 2 * SUBLANES


def _dot(a, b):
    return jnp.dot(a, b, preferred_element_type=F32)


def _dot_nt(a, b):
    return lax.dot_general(a, b, (((1,), (1,)), ((), ())), preferred_element_type=F32)


def _split2(a):
    hi = a.astype(BF16)
    lo = (a - hi.astype(F32)).astype(BF16)
    return hi, lo


def _dot3s(a_split, b_split):
    ah, al = a_split
    bh, bl = b_split
    return _dot(ah, bh) + (_dot(ah, bl) + _dot(al, bh))


def _dot3(a, b):
    return _dot3s(_split2(a), _split2(b))


def _inproj_kernel(x_ref, g_ref, w_ref, att_ref, dn_ref, z_ref, ab_ref):
    x = x_ref[...]
    ms = jnp.mean(x * x, axis=-1, keepdims=True)
    xn = (x * lax.rsqrt(ms + EPS) * g_ref[...]).astype(BF16)
    att_ref[...] = _dot(xn, w_ref[:, _C_ATT:_C_DN])
    dn_ref[...] = _dot(xn, w_ref[:, _C_DN:_C_Z])
    z_ref[...] = _dot(xn, w_ref[:, _C_Z:_C_AB])
    ab_ref[...] = _dot(xn, w_ref[:, _C_AB:_C_END])


def _inproj(x2, g, w_all, tm):
    T = x2.shape[0]
    return pl.pallas_call(
        _inproj_kernel,
        grid=(T // tm,),
        in_specs=[pl.BlockSpec((tm, D_MODEL), lambda i: (i, 0)),
                  pl.BlockSpec((1, D_MODEL), lambda i: (0, 0)),
                  pl.BlockSpec((D_MODEL, _C_END), lambda i: (0, 0))],
        out_specs=[pl.BlockSpec((tm, ATT_IN_WIDTH), lambda i: (i, 0)),
                   pl.BlockSpec((tm, 3 * DN_WIDTH), lambda i: (i, 0)),
                   pl.BlockSpec((tm, DN_WIDTH), lambda i: (i, 0)),
                   pl.BlockSpec((tm, 2 * LANES), lambda i: (i, 0))],
        out_shape=[jax.ShapeDtypeStruct((T, ATT_IN_WIDTH), F32),
                   jax.ShapeDtypeStruct((T, 3 * DN_WIDTH), F32),
                   jax.ShapeDtypeStruct((T, DN_WIDTH), F32),
                   jax.ShapeDtypeStruct((T, 2 * LANES), F32)],
        compiler_params=pltpu.CompilerParams(dimension_semantics=("arbitrary",),
                                             vmem_limit_bytes=VMEM_LIMIT_BYTES),
        name="inproj",
    )(x2, g, w_all)


def _head_rms(t, bd, g):
    hi, lo = _split2(t * t)
    ss = _dot(hi, bd) + _dot(lo, bd)
    return t * lax.rsqrt(ss * (1.0 / ATT_HEAD_DIM) + EPS) * g


def _attn_kernel(sink_ref, cur_ref, prev_ref, gq_ref, gk_ref, go_ref, bdq_ref, bdk_ref, o_ref,
                 *, tiles_per_seq):
    i = pl.program_id(0)
    tq = cur_ref.shape[0]
    nblk = tq // ATT_BLOCK
    first = (i % tiles_per_seq) == 0

    q = cur_ref[:, 0:ATT_WIDTH]
    k = cur_ref[:, ATT_WIDTH:ATT_WIDTH + ATT_KV_WIDTH]
    v = cur_ref[:, ATT_WIDTH + ATT_KV_WIDTH:ATT_WIDTH + 2 * ATT_KV_WIDTH]
    kp = prev_ref[:, 0:ATT_KV_WIDTH]
    vp = prev_ref[:, ATT_KV_WIDTH:2 * ATT_KV_WIDTH]

    qn = (_head_rms(q, bdq_ref[...], gq_ref[...]) * (ATT_HEAD_DIM ** -0.5)).astype(BF16)
    kn = jnp.concatenate([_head_rms(kp, bdk_ref[...], gk_ref[...]),
                          _head_rms(k, bdk_ref[...], gk_ref[...])], axis=0)
    vf = jnp.concatenate([vp, v], axis=0)

    lane = lax.broadcasted_iota(jnp.int32, kn.shape, 1)
    left = lane < ATT_HEAD_DIM
    zero = jnp.zeros_like(kn)
    k0l = jnp.where(left, kn, zero)
    k1r = jnp.where(left, zero, kn)
    v0l = jnp.where(left, vf, zero)
    v1r = jnp.where(left, zero, vf)
    kl = (k0l.astype(BF16), pltpu.roll(k1r, ATT_HEAD_DIM, 1).astype(BF16))
    kr = (pltpu.roll(k0l, ATT_HEAD_DIM, 1).astype(BF16), k1r.astype(BF16))
    vl = (v0l.astype(BF16), pltpu.roll(v1r, ATT_HEAD_DIM, 1).astype(BF16))
    vr = (pltpu.roll(v0l, ATT_HEAD_DIM, 1).astype(BF16), v1r.astype(BF16))

    qi = lax.broadcasted_iota(jnp.int32, (ATT_BLOCK, 2 * ATT_BLOCK), 0)
    kj = lax.broadcasted_iota(jnp.int32, (ATT_BLOCK, 2 * ATT_BLOCK), 1)
    rel = qi + ATT_BLOCK - kj
    in_window = (rel >= 0) & (rel < ATT_BLOCK)
    first_key = jnp.where(first, ATT_BLOCK, 0)
    neg_inf = jnp.full((ATT_BLOCK, 2 * ATT_BLOCK), -jnp.inf, F32)

    def softmax_rows(s, sink):
        m = jnp.maximum(jnp.max(s, axis=-1, keepdims=True), sink)
        p = jnp.exp(s - m)
        den = jnp.sum(p, axis=-1, keepdims=True) + jnp.exp(sink - m)
        return (p * (1.0 / den)).astype(BF16)

    for j in range(nblk):
        rows = slice(j * ATT_BLOCK, (j + 1) * ATT_BLOCK)
        krows = slice(j * ATT_BLOCK, (j + 2) * ATT_BLOCK)
        if j == 0:
            mask = in_window & (kj >= first_key)
        else:
            mask = in_window
        pairs = []
        for c in range(ATT_KV_HEADS):
            qe = jnp.concatenate([qn[rows, (2 * c) * LANES:(2 * c + 1) * LANES],
                                  qn[rows, (2 * c + 1) * LANES:(2 * c + 2) * LANES]], axis=0)
            s_even = _dot_nt(qe, kl[c][krows])
            s_odd = _dot_nt(qe, kr[c][krows])
            for half in range(2):
                hr = slice(half * ATT_BLOCK, (half + 1) * ATT_BLOCK)
                h_even = 4 * c + 2 * half
                p_e = softmax_rows(jnp.where(mask, s_even[hr], neg_inf), sink_ref[h_even])
                p_o = softmax_rows(jnp.where(mask, s_odd[hr], neg_inf), sink_ref[h_even + 1])
                pairs.append(_dot(p_e, vl[c][krows]) + _dot(p_o, vr[c][krows]))
        att = jnp.concatenate(pairs, axis=1)
        ms = jnp.mean(att * att, axis=-1, keepdims=True)
        o_ref[rows, :] = (att * lax.rsqrt(ms + EPS) * go_ref[...]).astype(BF16)


def _attention(att_in, sinks, gq, gk, go, bdq, bdk, seq, tq):
    T = att_in.shape[0]
    tiles_per_seq = seq // tq
    blk_per_tile = tq // ATT_BLOCK
    kv_col_blk = ATT_WIDTH // (2 * ATT_KV_WIDTH)
    return pl.pallas_call(
        functools.partial(_attn_kernel, tiles_per_seq=tiles_per_seq),
        grid_spec=pltpu.PrefetchScalarGridSpec(
            num_scalar_prefetch=1,
            grid=(T // tq,),
            in_specs=[pl.BlockSpec((tq, ATT_IN_WIDTH), lambda i, s: (i, 0)),
                      pl.BlockSpec((ATT_BLOCK, 2 * ATT_KV_WIDTH),
                                   lambda i, s: (jnp.maximum(i * blk_per_tile - 1, 0), kv_col_blk)),
                      pl.BlockSpec((1, ATT_WIDTH), lambda i, s: (0, 0)),
                      pl.BlockSpec((1, ATT_KV_WIDTH), lambda i, s: (0, 0)),
                      pl.BlockSpec((1, ATT_WIDTH), lambda i, s: (0, 0)),
                      pl.BlockSpec((ATT_WIDTH, ATT_WIDTH), lambda i, s: (0, 0)),
                      pl.BlockSpec((ATT_KV_WIDTH, ATT_KV_WIDTH), lambda i, s: (0, 0))],
            out_specs=pl.BlockSpec((tq, ATT_WIDTH), lambda i, s: (i, 0)),
        ),
        out_shape=jax.ShapeDtypeStruct((T, ATT_WIDTH), BF16),
        compiler_params=pltpu.CompilerParams(dimension_semantics=("arbitrary",),
                                             vmem_limit_bytes=VMEM_LIMIT_BYTES),
        name="swa_attention",
    )(sinks, att_in, att_in, gq, gk, go, bdq, bdk)


def _unit_lower_inverses(m_lows, row, col):
    n = m_lows[0].shape[0]
    zero = jnp.zeros_like(m_lows[0])
    zero16 = jnp.zeros(zero.shape, BF16)
    eye = jnp.where(row == col, jnp.ones_like(zero), zero)
    same = (row >> 3) == (col >> 3)
    m_sp = [_split2(m) for m in m_lows]
    nm_sp = [(jnp.where(same, -mh, zero16), jnp.where(same, -ml, zero16)) for mh, ml in m_sp]
    ps = [eye + jnp.where(same, -m, zero) for m in m_lows]
    n2_sp = [_split2(_dot3s(nm, nm)) for nm in nm_sp]
    ps = [p + _dot3s(_split2(p), n2) for p, n2 in zip(ps, n2_sp)]
    n4_sp = [_split2(_dot3s(n2, n2)) for n2 in n2_sp]
    ps = [p + _dot3s(_split2(p), n4) for p, n4 in zip(ps, n4_sp)]
    shift = 3
    while (1 << shift) < n:
        same_next = (row >> (shift + 1)) == (col >> (shift + 1))
        lower_left = same_next & jnp.logical_not(same)
        p16 = [p.astype(BF16) for p in ps]
        ts = [_dot(ph, jnp.where(lower_left, mh, zero16)) for ph, (mh, _) in zip(p16, m_sp)]
        ps = [p - _dot(t.astype(BF16), ph) for p, t, ph in zip(ps, ts, p16)]
        same = same_next
        shift += 1
    return ps


def _silu(v):
    return v * (1.0 / (1.0 + jnp.exp(-v)))


def _gdn_kernel(dn_ref, ab_ref, z_ref, cw_ref, nega_ref, dtb_ref, gn_ref, tri_ref, o_ref,
                state_ref, carry_ref):
    t = pl.program_id(1)
    C = GDN_CHUNK
    nb = dn_ref.shape[0]

    @pl.when(t == 0)
    def _():
        state_ref[...] = jnp.zeros_like(state_ref)
        carry_ref[...] = jnp.zeros_like(carry_ref)

    row = lax.broadcasted_iota(jnp.int32, (C, C), 0)
    col = lax.broadcasted_iota(jnp.int32, (C, C), 1)
    causal = row >= col
    strict = row > col
    zero_cc = jnp.zeros((C, C), F32)
    cw = cw_ref[...]
    row8 = lax.broadcasted_iota(jnp.int32, (SUBLANES, 3 * DN_WIDTH), 0)

    ids, qs, ks, vbs, kbes, decs, egs, kdecs, cds = [], [], [], [], [], [], [], [], []
    for bb in range(nb):
        x = dn_ref[bb]
        tail = carry_ref[bb]
        carry_ref[bb] = x[C - SUBLANES:C, :]
        y = x * cw[CONV_WIDTH - 1:CONV_WIDTH, :]
        for back in range(1, CONV_WIDTH):
            xr = pltpu.roll(x, back, 0)
            top = jnp.where(row8 < back, pltpu.roll(tail, back, 0), xr[0:SUBLANES])
            shifted = jnp.concatenate([top, xr[SUBLANES:]], axis=0)
            y = y + shifted * cw[CONV_WIDTH - 1 - back:CONV_WIDTH - back, :]
        y = _silu(y)

        ab = ab_ref[bb]
        sp_in = ab[:, 0:LANES] + dtb_ref[...]
        softplus = jnp.maximum(sp_in, 0.0) + jnp.log1p(jnp.exp(-jnp.abs(sp_in)))
        g_all = nega_ref[...] * softplus
        beta_all = 1.0 / (1.0 + jnp.exp(-ab[:, LANES:2 * LANES]))
        gc_all = _dot3(tri_ref[...], g_all)
        gc_t = gc_all.T

        for h in range(DN_HEADS):
            qh = y[:, h * DN_HEAD_DIM:(h + 1) * DN_HEAD_DIM]
            kh = y[:, DN_WIDTH + h * DN_HEAD_DIM:DN_WIDTH + (h + 1) * DN_HEAD_DIM]
            vh = y[:, 2 * DN_WIDTH + h * DN_HEAD_DIM:2 * DN_WIDTH + (h + 1) * DN_HEAD_DIM]
            qh = qh * lax.rsqrt(jnp.sum(qh * qh, axis=-1, keepdims=True) + EPS) * (DN_HEAD_DIM ** -0.5)
            kh = kh * lax.rsqrt(jnp.sum(kh * kh, axis=-1, keepdims=True) + EPS)
            gcol = gc_all[:, h:h + 1]
            grow = gc_t[h:h + 1, :]
            bcol = beta_all[:, h:h + 1]
            glast = gc_all[C - 1:C, h:h + 1]
            eg = jnp.exp(gcol)
            kb = kh * bcol
            ids.append((bb, h))
            qs.append(qh)
            ks.append(kh.astype(BF16))
            vbs.append(jnp.concatenate([vh * bcol, kb * eg], axis=1).astype(BF16))
            kbes.append(kb.astype(BF16))
            decs.append(jnp.where(causal, jnp.exp(jnp.where(causal, gcol - grow, zero_cc)), zero_cc))
            egs.append(eg)
            kdecs.append((kh * jnp.exp(glast - gcol)).T.astype(BF16))
            cds.append(jnp.exp(glast))

    n = len(ids)
    m_lows = [jnp.where(strict, _dot_nt(kbes[i], ks[i]) * decs[i], zero_cc) for i in range(n)]
    qks = [jnp.where(causal, _dot_nt(qs[i].astype(BF16), ks[i]) * decs[i], zero_cc).astype(BF16)
           for i in range(n)]
    tinvs = _unit_lower_inverses(m_lows, row, col)
    uws = [_dot(tinvs[i].astype(BF16), vbs[i]) for i in range(n)]
    s_olds = [state_ref[bb, h] for (bb, h) in ids]
    s16s = [s.astype(BF16) for s in s_olds]
    v16s = [(uws[i][:, 0:DN_HEAD_DIM] - _dot(uws[i][:, DN_HEAD_DIM:].astype(BF16), s16s[i])).astype(BF16)
            for i in range(n)]
    outs = [_dot((qs[i] * egs[i]).astype(BF16), s16s[i]) + _dot(qks[i], v16s[i]) for i in range(n)]
    for i, (bb, h) in enumerate(ids):
        state_ref[bb, h] = s_olds[i] * cds[i] + _dot(kdecs[i], v16s[i])

    for i, (bb, h) in enumerate(ids):
        hs = slice(h * DN_HEAD_DIM, (h + 1) * DN_HEAD_DIM)
        o = outs[i]
        ms = jnp.mean(o * o, axis=-1, keepdims=True)
        o_ref[bb, :, hs] = (o * lax.rsqrt(ms + EPS) * gn_ref[...] * _silu(z_ref[bb, :, hs])).astype(BF16)


def _gdn(dn_in, ab, z, conv_w, neg_a, dt_bias, gnorm, tri, batch, seq, nb):
    C = GDN_CHUNK
    nc = seq // C
    blk3 = lambda w: pl.BlockSpec((nb, C, w), lambda bi, t: (bi, t, 0))
    const = lambda shape: pl.BlockSpec(shape, lambda bi, t: (0, 0))
    out = pl.pallas_call(
        _gdn_kernel,
        grid=(batch // nb, nc),
        in_specs=[blk3(3 * DN_WIDTH), blk3(2 * LANES), blk3(DN_WIDTH),
                  const((CONV_WIDTH, 3 * DN_WIDTH)), const((1, LANES)), const((1, LANES)),
                  const((1, DN_HEAD_DIM)), const((C, C))],
        out_specs=blk3(DN_WIDTH),
        out_shape=jax.ShapeDtypeStruct((batch, seq, DN_WIDTH), BF16),
        scratch_shapes=[pltpu.VMEM((nb, DN_HEADS, DN_HEAD_DIM, DN_HEAD_DIM), F32),
                        pltpu.VMEM((nb, SUBLANES, 3 * DN_WIDTH), F32)],
        compiler_params=pltpu.CompilerParams(dimension_semantics=("arbitrary", "arbitrary"),
                                             vmem_limit_bytes=VMEM_LIMIT_BYTES),
        name="gated_deltanet",
    )(dn_in.reshape(batch, seq, -1), ab.reshape(batch, seq, -1), z.reshape(batch, seq, -1),
      conv_w, neg_a, dt_bias, gnorm, tri)
    return out.reshape(batch * seq, DN_WIDTH)


def _outproj_kernel(x_ref, att_ref, dn_ref, wo_ref, g_ref, h_ref, xn_ref):
    h = x_ref[...] + (_dot(att_ref[...], wo_ref[0:ATT_WIDTH, :])
                      + _dot(dn_ref[...], wo_ref[ATT_WIDTH:ATT_WIDTH + DN_WIDTH, :]))
    h_ref[...] = h
    ms = jnp.mean(h * h, axis=-1, keepdims=True)
    xn_ref[...] = (h * lax.rsqrt(ms + EPS) * g_ref[...]).astype(BF16)


def _outproj(x2, att_n, dn_o, w_out, g, tm):
    T = x2.shape[0]
    return pl.pallas_call(
        _outproj_kernel,
        grid=(T // tm,),
        in_specs=[pl.BlockSpec((tm, D_MODEL), lambda i: (i, 0)),
                  pl.BlockSpec((tm, ATT_WIDTH), lambda i: (i, 0)),
                  pl.BlockSpec((tm, DN_WIDTH), lambda i: (i, 0)),
                  pl.BlockSpec((ATT_WIDTH + DN_WIDTH, D_MODEL), lambda i: (0, 0)),
                  pl.BlockSpec((1, D_MODEL), lambda i: (0, 0))],
        out_specs=[pl.BlockSpec((tm, D_MODEL), lambda i: (i, 0)),
                   pl.BlockSpec((tm, D_MODEL), lambda i: (i, 0))],
        out_shape=[jax.ShapeDtypeStruct((T, D_MODEL), F32),
                   jax.ShapeDtypeStruct((T, D_MODEL), BF16)],
        compiler_params=pltpu.CompilerParams(dimension_semantics=("arbitrary",),
                                             vmem_limit_bytes=VMEM_LIMIT_BYTES),
        name="outproj",
    )(x2, att_n, dn_o, w_out, g)


def _oddeven_merge(lo, hi, r):
    step = r * 2
    if step < hi - lo:
        yield from _oddeven_merge(lo, hi, step)
        yield from _oddeven_merge(lo + r, hi, step)
        for i in range(lo + r, hi - r, step):
            yield (i, i + r)
    else:
        yield (lo, lo + r)


def _oddeven_sort_net(lo, hi):
    if hi - lo >= 1:
        mid = lo + (hi - lo) // 2
        yield from _oddeven_sort_net(lo, mid)
        yield from _oddeven_sort_net(mid + 1, hi)
        yield from _oddeven_merge(lo, hi, 1)


_SORT16 = tuple(_oddeven_sort_net(0, PEER_TOPK - 1))
_BITONIC16 = tuple((i, i + d) for d in (8, 4, 2, 1) for i in range(PEER_TOPK) if (i & d) == 0)
_STAIR = tuple((r, q) for r in range(PEER_TOPK) for q in range(PEER_TOPK) if (r + 1) * (q + 1) <= PEER_TOPK)


def _ce_vi(a, b):
    va, ia = a
    vb, ib = b
    a_first = (va > vb) | ((va == vb) & (ia < ib))
    return ((jnp.where(a_first, va, vb), jnp.where(a_first, ia, ib)),
            (jnp.where(a_first, vb, va), jnp.where(a_first, ib, ia)))


def _first_vi(a, b):
    va, ia = a
    vb, ib = b
    a_first = (va > vb) | ((va == vb) & (ia < ib))
    return (jnp.where(a_first, va, vb), jnp.where(a_first, ia, ib))


def _sort16_vi(items):
    items = list(items)
    for i, j in _SORT16:
        items[i], items[j] = _ce_vi(items[i], items[j])
    return items


def _merge_top16_vi(a, b):
    n = PEER_TOPK
    items = [_first_vi(a[i], b[n - 1 - i]) for i in range(n)]
    for i, j in _BITONIC16:
        items[i], items[j] = _ce_vi(items[i], items[j])
    return items


def _sort16_v(vals):
    vals = list(vals)
    for i, j in _SORT16:
        hi = jnp.maximum(vals[i], vals[j])
        lo = jnp.minimum(vals[i], vals[j])
        vals[i], vals[j] = hi, lo
    return vals


def _merge_top16_v(a, b):
    n = PEER_TOPK
    vals = [jnp.maximum(a[i], b[n - 1 - i]) for i in range(n)]
    for i, j in _BITONIC16:
        hi = jnp.maximum(vals[i], vals[j])
        lo = jnp.minimum(vals[i], vals[j])
        vals[i], vals[j] = hi, lo
    return vals


def _top16_vi_of_128(load_key):
    groups = []
    for gi in range(N_KEYS // PEER_TOPK):
        items = [(load_key(gi * PEER_TOPK + t), jnp.full((SUBLANES, LANES), float(gi * PEER_TOPK + t), F32))
                 for t in range(PEER_TOPK)]
        groups.append(_sort16_vi(items))
    while len(groups) > 1:
        groups = [_merge_top16_vi(groups[2 * t], groups[2 * t + 1]) for t in range(len(groups) // 2)]
    return groups[0]


def _top16_v_of_128(load_key):
    groups = [_sort16_v([load_key(gi * PEER_TOPK + t) for t in range(PEER_TOPK)])
              for gi in range(N_KEYS // PEER_TOPK)]
    while len(groups) > 1:
        groups = [_merge_top16_v(groups[2 * t], groups[2 * t + 1]) for t in range(len(groups) // 2)]
    return groups[0]


def _top16_is_distinct(load_key, vals):
    zero = jnp.zeros((SUBLANES, LANES), F32)
    one = jnp.ones((SUBLANES, LANES), F32)
    n_ge = zero
    for k in range(N_KEYS):
        n_ge = n_ge + jnp.where(load_key(k) >= vals[PEER_TOPK - 1], one, zero)
    ok = n_ge == float(PEER_TOPK)
    for r in range(PEER_TOPK - 1):
        ok = ok & (vals[r] > vals[r + 1])
    return jnp.where(ok, one, zero)


def _select_pairs(av, bv):
    zero = jnp.zeros((SUBLANES, LANES), F32)
    one = jnp.ones((SUBLANES, LANES), F32)
    cand = {(r, q): av[r] + bv[q] for (r, q) in _STAIR}
    row0 = [cand[(0, q)] for q in range(PEER_TOPK)]
    rest = [cand[rq] for rq in _STAIR if rq[0] > 0]
    neg_inf = jnp.full((SUBLANES, LANES), -jnp.inf, F32)
    best = row0
    for s in range(0, len(rest), PEER_TOPK):
        grp = rest[s:s + PEER_TOPK]
        grp = grp + [neg_inf] * (PEER_TOPK - len(grp))
        best = _merge_top16_v(best, _sort16_v(grp))
    thr = best[PEER_TOPK - 1]
    zsum = one
    for jj in range(1, PEER_TOPK):
        zsum = zsum + jnp.exp(best[jj] - best[0])
    inv_z_half = 0.5 / zsum

    n_gt = zero
    for rq in _STAIR:
        n_gt = n_gt + jnp.where(cand[rq] > thr, one, zero)
    need = float(PEER_TOPK) - n_gt
    cnt = zero
    n_row = [zero] * PEER_TOPK
    for (r, q) in _STAIR:
        c = cand[(r, q)]
        eq = c == thr
        take = (c > thr) | (eq & (cnt < need))
        cnt = cnt + jnp.where(eq, one, zero)
        n_row[r] = n_row[r] + jnp.where(take, one, zero)
    return n_row, inv_z_half


def _match_count(keys, probe, n_row):
    n_k = jnp.zeros((SUBLANES, LANES), F32)
    for r in reversed(range(PEER_TOPK)):
        n_k = jnp.where(keys[r] == probe, n_row[r], n_k)
    return n_k


def _match_rank(probe, rows):
    rk = jnp.full(probe.shape, float(PEER_TOPK), F32)
    for q in reversed(range(PEER_TOPK)):
        rk = jnp.where(rows[q] == probe, jnp.full_like(rk, float(q)), rk)
    return rk


def _peer_topk_kernel(xn_ref, wqt_ref, k0_ref, k1_ref, r1_ref, e1_ref, n_ref, c_ref, s0_ref, s1_ref):
    tb = xn_ref.shape[0]
    half_w = PEER_HEADS * PEER_HALF
    qt = _dot_nt(wqt_ref[...], xn_ref[...]).astype(BF16)
    s0 = _dot(k0_ref[...], qt[0:half_w])
    s1 = _dot(k1_ref[...], qt[half_w:2 * half_w])
    for g in range(tb // LANES):
        s0_ref[g] = s0[:, g * LANES:(g + 1) * LANES]
        s1_ref[g] = s1[:, g * LANES:(g + 1) * LANES]

    sub_iota = lax.broadcasted_iota(jnp.int32, (_ROWS16, LANES), 0).astype(F32)

    def slab(gi, carry):
        ls = pl.ds(pl.multiple_of(gi * LANES, LANES), LANES)
        load0 = lambda k: s0_ref[gi, pl.ds(k * SUBLANES, SUBLANES), :]
        load1 = lambda k: s1_ref[gi, pl.ds(k * SUBLANES, SUBLANES), :]
        load1_head = lambda h, kb: s1_ref[gi, pl.ds(kb * _ROWS16 * SUBLANES + h, _ROWS16, stride=SUBLANES), :]

        av = _top16_v_of_128(load0)
        bv = _top16_v_of_128(load1)
        n_row, inv_z_half = _select_pairs(av, bv)
        distinct = _top16_is_distinct(load0, av) * _top16_is_distinct(load1, bv)
        has_tie = jnp.min(distinct) < 0.5

        def emit(keys0, probe0, keys1, probe1):
            for k in range(N_KEYS):
                rs = pl.ds(k * SUBLANES, SUBLANES)
                s0k = load0(k)
                n_ref[gi, rs, :] = _match_count(keys0, probe0(k, s0k), n_row)
                c_ref[gi, rs, :] = jnp.exp(s0k - av[0]) * inv_z_half
            for h in range(PEER_HEADS):
                rows = [jnp.broadcast_to(keys1[q][h:h + 1, :], (_ROWS16, LANES)) for q in range(PEER_TOPK)]
                b0h = jnp.broadcast_to(bv[0][h:h + 1, :], (_ROWS16, LANES))
                for kb in range(N_KEYS // _ROWS16):
                    rs = pl.ds(h * N_KEYS + kb * _ROWS16, _ROWS16)
                    tile = load1_head(h, kb)
                    r1_ref[rs, ls] = _match_rank(probe1(kb, tile), rows).astype(BF16)
                    e1_ref[rs, ls] = jnp.exp(tile - b0h).astype(BF16)

        @pl.when(jnp.logical_not(has_tie))
        def _():
            emit(av, lambda k, s0k: s0k, bv, lambda kb, tile: tile)

        @pl.when(has_tie)
        def _():
            a = _top16_vi_of_128(load0)
            b = _top16_vi_of_128(load1)
            emit([it[1] for it in a], lambda k, s0k: float(k),
                 [it[1] for it in b], lambda kb, tile: sub_iota + float(kb * _ROWS16))

        return carry

    lax.fori_loop(0, tb // LANES, slab, 0)


def _peer_topk(xn2, wqt, k0, k1, tb):
    T = xn2.shape[0]
    rows = PEER_HEADS * N_KEYS
    kspec = pl.BlockSpec((rows, PEER_HEADS * PEER_HALF), lambda i: (0, 0))
    ospec = pl.BlockSpec((rows, tb), lambda i: (0, i))
    slab_spec = pl.BlockSpec((tb // LANES, rows, LANES), lambda i: (i, 0, 0))
    return pl.pallas_call(
        _peer_topk_kernel,
        grid=(T // tb,),
        in_specs=[pl.BlockSpec((tb, D_MODEL), lambda i: (i, 0)),
                  pl.BlockSpec((D_MODEL, D_MODEL), lambda i: (0, 0)),
                  kspec, kspec],
        out_specs=[ospec, ospec, slab_spec, slab_spec],
        out_shape=[jax.ShapeDtypeStruct((rows, T), BF16),
                   jax.ShapeDtypeStruct((rows, T), BF16),
                   jax.ShapeDtypeStruct((T // LANES, rows, LANES), F32),
                   jax.ShapeDtypeStruct((T // LANES, rows, LANES), F32)],
        scratch_shapes=[pltpu.VMEM((tb // LANES, rows, LANES), F32),
                        pltpu.VMEM((tb // LANES, rows, LANES), F32)],
        compiler_params=pltpu.CompilerParams(dimension_semantics=("arbitrary",),
                                             vmem_limit_bytes=VMEM_LIMIT_BYTES),
        name="peer_topk",
    )(xn2, wqt, k0, k1)


_SUB_E = 512
_W_TILE = 256


def _peer_dense_kernel(xn_ref, xn_ahead_ref, h_ref, r1_ref, e1_ref, n_ref, c_ref, u_ref, u_ahead_ref,
                       vt_ref, o_ref, acc_ref, hm_ref, at_ref):
    i = pl.program_id(0)
    c = pl.program_id(1)
    tb = xn_ref.shape[0]
    n_sub = vt_ref.shape[0]
    i0_per_sub = _SUB_E // N_KEYS
    xn = xn_ref[...]

    def gelu2(a):
        return (a * (1.0 + lax.erf(a * 0.7071067811865476))).astype(BF16)

    @pl.when(c == 0)
    def _():
        acc_ref[...] = jnp.zeros_like(acc_ref)

    @pl.when((c == 0) & (i == 0))
    def _():
        at_ref[0] = gelu2(_dot_nt(u_ref[0:_SUB_E, :], xn))

    tw = min(tb, _W_TILE)
    zero16 = jnp.zeros((_ROWS16, tw), BF16)
    n_j = N_KEYS // _ROWS16

    def gated_acts(sub, slot, out_ref):
        for ii in range(i0_per_sub):
            i0 = (c * n_sub + sub) * i0_per_sub + ii
            for lt in range(tb // tw):
                ls = slice(lt * tw, (lt + 1) * tw)
                wacc = [None] * n_j
                for h in range(PEER_HEADS):
                    rowi = pl.ds(i0 * PEER_HEADS + h, SUBLANES, stride=0)
                    slabs = range(lt * tw // LANES, (lt + 1) * tw // LANES)
                    n8 = jnp.concatenate([n_ref[g, rowi, :] for g in slabs], axis=1)
                    c8 = jnp.concatenate([c_ref[g, rowi, :] for g in slabs], axis=1)
                    nb = jnp.concatenate([n8, n8], axis=0).astype(BF16)
                    cb = jnp.concatenate([c8, c8], axis=0).astype(BF16)
                    for j in range(n_j):
                        rs = slice(h * N_KEYS + j * _ROWS16, h * N_KEYS + (j + 1) * _ROWS16)
                        term = jnp.where(r1_ref[rs, ls] < nb, e1_ref[rs, ls] * cb, zero16)
                        wacc[j] = term if wacc[j] is None else wacc[j] + term
                for j in range(n_j):
                    er = slice(ii * N_KEYS + j * _ROWS16, ii * N_KEYS + (j + 1) * _ROWS16)
                    out_ref[er, ls] = at_ref[slot, er, ls] * wacc[j]

    for sub in range(n_sub):
        par = sub % 2
        if sub + 1 < n_sub:
            at_ref[1 - par] = gelu2(_dot_nt(u_ref[(sub + 1) * _SUB_E:(sub + 2) * _SUB_E, :], xn))
        else:
            at_ref[1 - par] = gelu2(_dot_nt(u_ahead_ref[...], xn_ahead_ref[...]))
        if sub > 0:
            acc_ref[...] += _dot(vt_ref[sub - 1], hm_ref[1 - par])
        gated_acts(sub, par, hm_ref.at[par])
    acc_ref[...] += _dot(vt_ref[n_sub - 1], hm_ref[(n_sub - 1) % 2])

    @pl.when(c == pl.num_programs(1) - 1)
    def _():
        o_ref[...] = h_ref[...] + acc_ref[...].T


def _peer_dense(xn2, h, r1, e1, n_sel, c_gate, u16, vt_slabs, tb, ec):
    T = xn2.shape[0]
    rows = PEER_HEADS * N_KEYS
    n_sub = ec // _SUB_E
    assert n_sub >= 2
    tspec = pl.BlockSpec((rows, tb), lambda i, c: (0, i))
    slab_spec = pl.BlockSpec((tb // LANES, rows, LANES), lambda i, c: (i, 0, 0))
    n_tok_blocks = T // tb
    n_chunks = N_EXPERTS // ec
    n_slabs = N_EXPERTS // _SUB_E

    def ahead_tokens(i, c):
        return (jnp.where(c == n_chunks - 1, jnp.minimum(i + 1, n_tok_blocks - 1), i), 0)

    return pl.pallas_call(
        _peer_dense_kernel,
        grid=(n_tok_blocks, n_chunks),
        in_specs=[pl.BlockSpec((tb, D_MODEL), lambda i, c: (i, 0)),
                  pl.BlockSpec((tb, D_MODEL), ahead_tokens),
                  pl.BlockSpec((tb, D_MODEL), lambda i, c: (i, 0)),
                  tspec, tspec, slab_spec, slab_spec,
                  pl.BlockSpec((ec, D_MODEL), lambda i, c: (c, 0)),
                  pl.BlockSpec((_SUB_E, D_MODEL), lambda i, c: (((c + 1) * n_sub) % n_slabs, 0)),
                  pl.BlockSpec((n_sub, D_MODEL, _SUB_E), lambda i, c: (c, 0, 0))],
        out_specs=pl.BlockSpec((tb, D_MODEL), lambda i, c: (i, 0)),
        out_shape=jax.ShapeDtypeStruct((T, D_MODEL), F32),
        scratch_shapes=[pltpu.VMEM((D_MODEL, tb), F32),
                        pltpu.VMEM((2, _SUB_E, tb), BF16),
                        pltpu.VMEM((2, _SUB_E, tb), BF16)],
        compiler_params=pltpu.CompilerParams(dimension_semantics=("arbitrary", "arbitrary"),
                                             vmem_limit_bytes=VMEM_LIMIT_BYTES),
        name="peer_dense",
    )(xn2, xn2, h, r1, e1, n_sel, c_gate, u16, u16, vt_slabs)


def _block_diag_ones(width, blk):
    idx = np.arange(width) // blk
    return jnp.asarray((idx[:, None] == idx[None, :]).astype(np.float32), dtype=BF16)


def _layer(x, norm_mix_g, w_in, att_q_norm_g, att_k_norm_g, att_sinks, att_out_norm_g, dn_conv_w,
           dn_a_log, dn_dt_bias, dn_out_norm_g, w_out, norm_ffn_g, peer_w_q, peer_sub_keys, peer_u, peer_v):
    B, S, _ = x.shape
    T = B * S
    x2 = x.reshape(T, D_MODEL)

    o_q, o_k, o_v = 0, ATT_WIDTH, ATT_WIDTH + ATT_KV_WIDTH
    o_dn = ATT_WIDTH + 2 * ATT_KV_WIDTH
    o_a = o_dn + 3 * DN_WIDTH
    o_b = o_a + DN_HEADS
    o_z = o_b + DN_HEADS
    pad = jnp.zeros((D_MODEL, LANES - DN_HEADS), w_in.dtype)
    w_all = jnp.concatenate([w_in[:, o_q:o_dn], w_in[:, o_dn:o_a], w_in[:, o_z:o_z + DN_WIDTH],
                             w_in[:, o_a:o_b], pad, w_in[:, o_b:o_z], pad], axis=1).astype(BF16)
    del o_k, o_v

    def lane_pad(vec):
        return jnp.concatenate([vec.astype(F32), jnp.zeros((LANES - vec.shape[0],), F32)]).reshape(1, LANES)

    assert T % PROJ_TOKENS == 0 and S % ATT_TOKENS == 0 and S % GDN_CHUNK == 0 and B % GDN_SEQS == 0
    assert T % PEER_TOKENS == 0
    att_in, dn_in, z_in, ab_in = _inproj(x2, norm_mix_g.reshape(1, D_MODEL), w_all, tm=PROJ_TOKENS)

    att_n = _attention(
        att_in, att_sinks.astype(F32),
        jnp.tile(att_q_norm_g.astype(F32), ATT_HEADS).reshape(1, ATT_WIDTH),
        jnp.tile(att_k_norm_g.astype(F32), ATT_KV_HEADS).reshape(1, ATT_KV_WIDTH),
        att_out_norm_g.astype(F32).reshape(1, ATT_WIDTH),
        _block_diag_ones(ATT_WIDTH, ATT_HEAD_DIM), _block_diag_ones(ATT_KV_WIDTH, ATT_HEAD_DIM),
        seq=S, tq=ATT_TOKENS)

    tri = jnp.asarray(np.tril(np.ones((GDN_CHUNK, GDN_CHUNK), np.float32)))
    dn_o = _gdn(dn_in, ab_in, z_in, dn_conv_w.astype(F32),
                lane_pad(-jnp.exp(dn_a_log.astype(F32))), lane_pad(dn_dt_bias),
                dn_out_norm_g.astype(F32).reshape(1, DN_HEAD_DIM), tri, batch=B, seq=S, nb=GDN_SEQS)

    h, xn2 = _outproj(x2, att_n, dn_o, w_out.astype(BF16), norm_ffn_g.reshape(1, D_MODEL), tm=PROJ_TOKENS)

    wqt = (peer_w_q.T.reshape(PEER_HEADS, 2, PEER_HALF, D_MODEL).transpose(1, 0, 2, 3)
           .reshape(D_MODEL, D_MODEL).astype(BF16))
    eye = jnp.eye(PEER_HEADS, dtype=peer_sub_keys.dtype)
    kd = jnp.einsum('hpkc,hg->pkhgc', peer_sub_keys, eye)
    k_km = kd.reshape(2, N_KEYS * PEER_HEADS, PEER_HEADS * PEER_HALF).astype(BF16)
    r1, e1, n_sel, c_gate = _peer_topk(xn2, wqt, k_km[0], k_km[1], tb=PEER_TOKENS)

    vt_slabs = (peer_v.reshape(N_EXPERTS // _SUB_E, _SUB_E, D_MODEL).transpose(0, 2, 1).astype(BF16))
    out = _peer_dense(xn2, h, r1, e1, n_sel, c_gate, peer_u.astype(BF16), vt_slabs,
                      tb=PEER_TOKENS, ec=PEER_CHUNK)
    return out.reshape(B, S, D_MODEL)


def kernel(x, norm_mix_g, w_in, att_q_norm_g, att_k_norm_g, att_sinks, att_out_norm_g, dn_conv_w, dn_a_log, dn_dt_bias, dn_out_norm_g, w_out, norm_ffn_g, peer_w_q, peer_sub_keys, peer_u, peer_v):
    h = x
    for l in range(norm_mix_g.shape[0]):
        h = _layer(h, norm_mix_g[l], w_in[l], att_q_norm_g[l], att_k_norm_g[l], att_sinks[l],
                   att_out_norm_g[l], dn_conv_w[l], dn_a_log[l], dn_dt_bias[l], dn_out_norm_g[l],
                   w_out[l], norm_ffn_g[l], peer_w_q[l], peer_sub_keys[l], peer_u[l], peer_v[l])
    return h
```

```python
import functools

import numpy as np
import jax
import jax.numpy as jnp
from jax import lax
from jax.experimental import pallas as pl
from jax.experimental.pallas import tpu as pltpu

F32 = jnp.float32
BF16 = jnp.bfloat16

D_MODEL = 1024
ATT_HEADS = 8
ATT_KV_HEADS = 2
ATT_HEAD_DIM = 64
ATT_WIDTH = 512
ATT_KV_WIDTH = 128
ATT_BLOCK = 128
DN_HEADS = 4
DN_HEAD_DIM = 128
DN_WIDTH = 512
CONV_WIDTH = 4
PEER_HEADS = 8
N_KEYS = 128
N_EXPERTS = N_KEYS * N_KEYS
PEER_HALF = 64
PEER_TOPK = 16
EPS = 1e-6

LANES = 128
SUBLANES = 8
V7X_VMEM_BYTES = 64 * 1024 * 1024
VMEM_LIMIT_BYTES = V7X_VMEM_BYTES - 8 * 1024 * 1024

PROJ_TOKENS = 512
ATT_TOKENS = 1024
GDN_SEQS = 4
PEER_TOKENS = 512
PEER_CHUNK = 2048

ATT_IN_WIDTH = ATT_WIDTH + 2 * ATT_KV_WIDTH

_C_ATT = 0
_C_DN = ATT_IN_WIDTH
_C_Z = _C_DN + 3 * DN_WIDTH
_C_AB = _C_Z + DN_WIDTH
_C_END = _C_AB + 2 * LANES

GDN_CHUNK = 128
_ROWS16 = 2 * SUBLANES


def _dot(a, b):
    return jnp.dot(a, b, preferred_element_type=F32)


def _dot_nt(a, b):
    return lax.dot_general(a, b, (((1,), (1,)), ((), ())), preferred_element_type=F32)


def _split2(a):
    hi = a.astype(BF16)
    lo = (a - hi.astype(F32)).astype(BF16)
    return hi, lo


def _dot3s(a_split, b_split):
    ah, al = a_split
    bh, bl = b_split
    return _dot(ah, bh) + (_dot(ah, bl) + _dot(al, bh))


def _dot3(a, b):
    return _dot3s(_split2(a), _split2(b))


def _inproj_kernel(x_ref, g_ref, w_ref, att_ref, dn_ref, z_ref, ab_ref):
    x = x_ref[...]
    ms = jnp.mean(x * x, axis=-1, keepdims=True)
    xn = (x * lax.rsqrt(ms + EPS) * g_ref[...]).astype(BF16)
    att_ref[...] = _dot(xn, w_ref[:, _C_ATT:_C_DN])
    dn_ref[...] = _dot(xn, w_ref[:, _C_DN:_C_Z])
    z_ref[...] = _dot(xn, w_ref[:, _C_Z:_C_AB])
    ab_ref[...] = _dot(xn, w_ref[:, _C_AB:_C_END])


def _inproj(x2, g, w_all, tm):
    T = x2.shape[0]
    return pl.pallas_call(
        _inproj_kernel,
        grid=(T // tm,),
        in_specs=[pl.BlockSpec((tm, D_MODEL), lambda i: (i, 0)),
                  pl.BlockSpec((1, D_MODEL), lambda i: (0, 0)),
                  pl.BlockSpec((D_MODEL, _C_END), lambda i: (0, 0))],
        out_specs=[pl.BlockSpec((tm, ATT_IN_WIDTH), lambda i: (i, 0)),
                   pl.BlockSpec((tm, 3 * DN_WIDTH), lambda i: (i, 0)),
                   pl.BlockSpec((tm, DN_WIDTH), lambda i: (i, 0)),
                   pl.BlockSpec((tm, 2 * LANES), lambda i: (i, 0))],
        out_shape=[jax.ShapeDtypeStruct((T, ATT_IN_WIDTH), F32),
                   jax.ShapeDtypeStruct((T, 3 * DN_WIDTH), F32),
                   jax.ShapeDtypeStruct((T, DN_WIDTH), F32),
                   jax.ShapeDtypeStruct((T, 2 * LANES), F32)],
        compiler_params=pltpu.CompilerParams(dimension_semantics=("arbitrary",),
                                             vmem_limit_bytes=VMEM_LIMIT_BYTES),
        name="inproj",
    )(x2, g, w_all)


def _head_rms(t, bd, g):
    hi, lo = _split2(t * t)
    ss = _dot(hi, bd) + _dot(lo, bd)
    return t * lax.rsqrt(ss * (1.0 / ATT_HEAD_DIM) + EPS) * g


def _attn_kernel(sink_ref, cur_ref, prev_ref, gq_ref, gk_ref, go_ref, bdq_ref, bdk_ref, o_ref,
                 *, tiles_per_seq):
    i = pl.program_id(0)
    tq = cur_ref.shape[0]
    nblk = tq // ATT_BLOCK
    first = (i % tiles_per_seq) == 0

    q = cur_ref[:, 0:ATT_WIDTH]
    k = cur_ref[:, ATT_WIDTH:ATT_WIDTH + ATT_KV_WIDTH]
    v = cur_ref[:, ATT_WIDTH + ATT_KV_WIDTH:ATT_WIDTH + 2 * ATT_KV_WIDTH]
    kp = prev_ref[:, 0:ATT_KV_WIDTH]
    vp = prev_ref[:, ATT_KV_WIDTH:2 * ATT_KV_WIDTH]

    qn = (_head_rms(q, bdq_ref[...], gq_ref[...]) * (ATT_HEAD_DIM ** -0.5)).astype(BF16)
    kn = jnp.concatenate([_head_rms(kp, bdk_ref[...], gk_ref[...]),
                          _head_rms(k, bdk_ref[...], gk_ref[...])], axis=0)
    vf = jnp.concatenate([vp, v], axis=0)

    lane = lax.broadcasted_iota(jnp.int32, kn.shape, 1)
    left = lane < ATT_HEAD_DIM
    zero = jnp.zeros_like(kn)
    k0l = jnp.where(left, kn, zero)
    k1r = jnp.where(left, zero, kn)
    v0l = jnp.where(left, vf, zero)
    v1r = jnp.where(left, zero, vf)
    kl = (k0l.astype(BF16), pltpu.roll(k1r, ATT_HEAD_DIM, 1).astype(BF16))
    kr = (pltpu.roll(k0l, ATT_HEAD_DIM, 1).astype(BF16), k1r.astype(BF16))
    vl = (v0l.astype(BF16), pltpu.roll(v1r, ATT_HEAD_DIM, 1).astype(BF16))
    vr = (pltpu.roll(v0l, ATT_HEAD_DIM, 1).astype(BF16), v1r.astype(BF16))

    qi = lax.broadcasted_iota(jnp.int32, (ATT_BLOCK, 2 * ATT_BLOCK), 0)
    kj = lax.broadcasted_iota(jnp.int32, (ATT_BLOCK, 2 * ATT_BLOCK), 1)
    rel = qi + ATT_BLOCK - kj
    in_window = (rel >= 0) & (rel < ATT_BLOCK)
    first_key = jnp.where(first, ATT_BLOCK, 0)
    neg_inf = jnp.full((ATT_BLOCK, 2 * ATT_BLOCK), -jnp.inf, F32)

    def softmax_rows(s, sink):
        m = jnp.maximum(jnp.max(s, axis=-1, keepdims=True), sink)
        p = jnp.exp(s - m)
        den = jnp.sum(p, axis=-1, keepdims=True) + jnp.exp(sink - m)
        return (p * (1.0 / den)).astype(BF16)

    for j in range(nblk):
        rows = slice(j * ATT_BLOCK, (j + 1) * ATT_BLOCK)
        krows = slice(j * ATT_BLOCK, (j + 2) * ATT_BLOCK)
        if j == 0:
            mask = in_window & (kj >= first_key)
        else:
            mask = in_window
        pairs = []
        for c in range(ATT_KV_HEADS):
            qe = jnp.concatenate([qn[rows, (2 * c) * LANES:(2 * c + 1) * LANES],
                                  qn[rows, (2 * c + 1) * LANES:(2 * c + 2) * LANES]], axis=0)
            s_even = _dot_nt(qe, kl[c][krows])
            s_odd = _dot_nt(qe, kr[c][krows])
            for half in range(2):
                hr = slice(half * ATT_BLOCK, (half + 1) * ATT_BLOCK)
                h_even = 4 * c + 2 * half
                p_e = softmax_rows(jnp.where(mask, s_even[hr], neg_inf), sink_ref[h_even])
                p_o = softmax_rows(jnp.where(mask, s_odd[hr], neg_inf), sink_ref[h_even + 1])
                pairs.append(_dot(p_e, vl[c][krows]) + _dot(p_o, vr[c][krows]))
        att = jnp.concatenate(pairs, axis=1)
        ms = jnp.mean(att * att, axis=-1, keepdims=True)
        o_ref[rows, :] = (att * lax.rsqrt(ms + EPS) * go_ref[...]).astype(BF16)


def _attention(att_in, sinks, gq, gk, go, bdq, bdk, seq, tq):
    T = att_in.shape[0]
    tiles_per_seq = seq // tq
    blk_per_tile = tq // ATT_BLOCK
    kv_col_blk = ATT_WIDTH // (2 * ATT_KV_WIDTH)
    return pl.pallas_call(
        functools.partial(_attn_kernel, tiles_per_seq=tiles_per_seq),
        grid_spec=pltpu.PrefetchScalarGridSpec(
            num_scalar_prefetch=1,
            grid=(T // tq,),
            in_specs=[pl.BlockSpec((tq, ATT_IN_WIDTH), lambda i, s: (i, 0)),
                      pl.BlockSpec((ATT_BLOCK, 2 * ATT_KV_WIDTH),
                                   lambda i, s: (jnp.maximum(i * blk_per_tile - 1, 0), kv_col_blk)),
                      pl.BlockSpec((1, ATT_WIDTH), lambda i, s: (0, 0)),
                      pl.BlockSpec((1, ATT_KV_WIDTH), lambda i, s: (0, 0)),
                      pl.BlockSpec((1, ATT_WIDTH), lambda i, s: (0, 0)),
                      pl.BlockSpec((ATT_WIDTH, ATT_WIDTH), lambda i, s: (0, 0)),
                      pl.BlockSpec((ATT_KV_WIDTH, ATT_KV_WIDTH), lambda i, s: (0, 0))],
            out_specs=pl.BlockSpec((tq, ATT_WIDTH), lambda i, s: (i, 0)),
        ),
        out_shape=jax.ShapeDtypeStruct((T, ATT_WIDTH), BF16),
        compiler_params=pltpu.CompilerParams(dimension_semantics=("arbitrary",),
                                             vmem_limit_bytes=VMEM_LIMIT_BYTES),
        name="swa_attention",
    )(sinks, att_in, att_in, gq, gk, go, bdq, bdk)


def _unit_lower_inverses(m_lows, row, col):
    n = m_lows[0].shape[0]
    zero = jnp.zeros_like(m_lows[0])
    zero16 = jnp.zeros(zero.shape, BF16)
    eye = jnp.where(row == col, jnp.ones_like(zero), zero)
    same = (row >> 3) == (col >> 3)
    m16 = [m.astype(BF16) for m in m_lows]
    nm16 = [jnp.where(same, -mh, zero16) for mh in m16]
    ps = [eye + jnp.where(same, -m, zero) for m in m_lows]
    n2_16 = [_dot(nm, nm).astype(BF16) for nm in nm16]
    ps = [p + _dot(p.astype(BF16), n2) for p, n2 in zip(ps, n2_16)]
    n4_16 = [_dot(n2, n2).astype(BF16) for n2 in n2_16]
    ps = [p + _dot(p.astype(BF16), n4) for p, n4 in zip(ps, n4_16)]
    shift = 3
    while (1 << shift) < n:
        same_next = (row >> (shift + 1)) == (col >> (shift + 1))
        lower_left = same_next & jnp.logical_not(same)
        p16 = [p.astype(BF16) for p in ps]
        ts = [_dot(ph, jnp.where(lower_left, mh, zero16)) for ph, mh in zip(p16, m16)]
        ps = [p - _dot(t.astype(BF16), ph) for p, t, ph in zip(ps, ts, p16)]
        same = same_next
        shift += 1
    return ps


def _silu(v):
    return v * (1.0 / (1.0 + jnp.exp(-v)))


def _gdn_kernel(dn_ref, ab_ref, z_ref, cw_ref, nega_ref, dtb_ref, gn_ref, tri_ref, o_ref,
                state_ref, carry_ref):
    t = pl.program_id(1)
    C = GDN_CHUNK
    nb = dn_ref.shape[0]

    @pl.when(t == 0)
    def _():
        state_ref[...] = jnp.zeros_like(state_ref)
        carry_ref[...] = jnp.zeros_like(carry_ref)

    row = lax.broadcasted_iota(jnp.int32, (C, C), 0)
    col = lax.broadcasted_iota(jnp.int32, (C, C), 1)
    causal = row >= col
    strict = row > col
    zero_cc = jnp.zeros((C, C), F32)
    cw = cw_ref[...]
    row8 = lax.broadcasted_iota(jnp.int32, (SUBLANES, 3 * DN_WIDTH), 0)

    ids, qs, ks, vbs, kbes, decs, egs, kdecs, cds = [], [], [], [], [], [], [], [], []
    for bb in range(nb):
        x = dn_ref[bb]
        tail = carry_ref[bb]
        carry_ref[bb] = x[C - SUBLANES:C, :]
        y = x * cw[CONV_WIDTH - 1:CONV_WIDTH, :]
        for back in range(1, CONV_WIDTH):
            xr = pltpu.roll(x, back, 0)
            top = jnp.where(row8 < back, pltpu.roll(tail, back, 0), xr[0:SUBLANES])
            shifted = jnp.concatenate([top, xr[SUBLANES:]], axis=0)
            y = y + shifted * cw[CONV_WIDTH - 1 - back:CONV_WIDTH - back, :]
        y = _silu(y)

        ab = ab_ref[bb]
        sp_in = ab[:, 0:LANES] + dtb_ref[...]
        softplus = jnp.maximum(sp_in, 0.0) + jnp.log1p(jnp.exp(-jnp.abs(sp_in)))
        g_all = nega_ref[...] * softplus
        beta_all = 1.0 / (1.0 + jnp.exp(-ab[:, LANES:2 * LANES]))
        gc_all = _dot3(tri_ref[...], g_all)
        gc_t = gc_all.T

        for h in range(DN_HEADS):
            qh = y[:, h * DN_HEAD_DIM:(h + 1) * DN_HEAD_DIM]
            kh = y[:, DN_WIDTH + h * DN_HEAD_DIM:DN_WIDTH + (h + 1) * DN_HEAD_DIM]
            vh = y[:, 2 * DN_WIDTH + h * DN_HEAD_DIM:2 * DN_WIDTH + (h + 1) * DN_HEAD_DIM]
            qh = qh * lax.rsqrt(jnp.sum(qh * qh, axis=-1, keepdims=True) + EPS) * (DN_HEAD_DIM ** -0.5)
            kh = kh * lax.rsqrt(jnp.sum(kh * kh, axis=-1, keepdims=True) + EPS)
            gcol = gc_all[:, h:h + 1]
            grow = gc_t[h:h + 1, :]
            bcol = beta_all[:, h:h + 1]
            glast = gc_all[C - 1:C, h:h + 1]
            eg = jnp.exp(gcol)
            kb = kh * bcol
            ids.append((bb, h))
            qs.append(qh)
            ks.append(kh.astype(BF16))
            vbs.append(jnp.concatenate([vh * bcol, kb * eg], axis=1).astype(BF16))
            kbes.append(kb.astype(BF16))
            decs.append(jnp.where(causal, jnp.exp(jnp.where(causal, gcol - grow, zero_cc)), zero_cc))
            egs.append(eg)
            kdecs.append((kh * jnp.exp(glast - gcol)).T.astype(BF16))
            cds.append(jnp.exp(glast))

    n = len(ids)
    m_lows = [jnp.where(strict, _dot_nt(kbes[i], ks[i]) * decs[i], zero_cc) for i in range(n)]
    qks = [jnp.where(causal, _dot_nt(qs[i].astype(BF16), ks[i]) * decs[i], zero_cc).astype(BF16)
           for i in range(n)]
    tinvs = _unit_lower_inverses(m_lows, row, col)
    uws = [_dot(tinvs[i].astype(BF16), vbs[i]) for i in range(n)]
    s_olds = [state_ref[bb, h] for (bb, h) in ids]
    s16s = [s.astype(BF16) for s in s_olds]
    v16s = [(uws[i][:, 0:DN_HEAD_DIM] - _dot(uws[i][:, DN_HEAD_DIM:].astype(BF16), s16s[i])).astype(BF16)
            for i in range(n)]
    outs = [_dot((qs[i] * egs[i]).astype(BF16), s16s[i]) + _dot(qks[i], v16s[i]) for i in range(n)]
    for i, (bb, h) in enumerate(ids):
        state_ref[bb, h] = s_olds[i] * cds[i] + _dot(kdecs[i], v16s[i])

    for i, (bb, h) in enumerate(ids):
        hs = slice(h * DN_HEAD_DIM, (h + 1) * DN_HEAD_DIM)
        o = outs[i]
        ms = jnp.mean(o * o, axis=-1, keepdims=True)
        o_ref[bb, :, hs] = (o * lax.rsqrt(ms + EPS) * gn_ref[...] * _silu(z_ref[bb, :, hs])).astype(BF16)


def _gdn(dn_in, ab, z, conv_w, neg_a, dt_bias, gnorm, tri, batch, seq, nb):
    C = GDN_CHUNK
    nc = seq // C
    blk3 = lambda w: pl.BlockSpec((nb, C, w), lambda bi, t: (bi, t, 0))
    const = lambda shape: pl.BlockSpec(shape, lambda bi, t: (0, 0))
    out = pl.pallas_call(
        _gdn_kernel,
        grid=(batch // nb, nc),
        in_specs=[blk3(3 * DN_WIDTH), blk3(2 * LANES), blk3(DN_WIDTH),
                  const((CONV_WIDTH, 3 * DN_WIDTH)), const((1, LANES)), const((1, LANES)),
                  const((1, DN_HEAD_DIM)), const((C, C))],
        out_specs=blk3(DN_WIDTH),
        out_shape=jax.ShapeDtypeStruct((batch, seq, DN_WIDTH), BF16),
        scratch_shapes=[pltpu.VMEM((nb, DN_HEADS, DN_HEAD_DIM, DN_HEAD_DIM), F32),
                        pltpu.VMEM((nb, SUBLANES, 3 * DN_WIDTH), F32)],
        compiler_params=pltpu.CompilerParams(dimension_semantics=("arbitrary", "arbitrary"),
                                             vmem_limit_bytes=VMEM_LIMIT_BYTES),
        name="gated_deltanet",
    )(dn_in.reshape(batch, seq, -1), ab.reshape(batch, seq, -1), z.reshape(batch, seq, -1),
      conv_w, neg_a, dt_bias, gnorm, tri)
    return out.reshape(batch * seq, DN_WIDTH)


def _outproj_kernel(x_ref, att_ref, dn_ref, wo_ref, g_ref, h_ref, xn_ref):
    h = x_ref[...] + (_dot(att_ref[...], wo_ref[0:ATT_WIDTH, :])
                      + _dot(dn_ref[...], wo_ref[ATT_WIDTH:ATT_WIDTH + DN_WIDTH, :]))
    h_ref[...] = h
    ms = jnp.mean(h * h, axis=-1, keepdims=True)
    xn_ref[...] = (h * lax.rsqrt(ms + EPS) * g_ref[...]).astype(BF16)


def _outproj(x2, att_n, dn_o, w_out, g, tm):
    T = x2.shape[0]
    return pl.pallas_call(
        _outproj_kernel,
        grid=(T // tm,),
        in_specs=[pl.BlockSpec((tm, D_MODEL), lambda i: (i, 0)),
                  pl.BlockSpec((tm, ATT_WIDTH), lambda i: (i, 0)),
                  pl.BlockSpec((tm, DN_WIDTH), lambda i: (i, 0)),
                  pl.BlockSpec((ATT_WIDTH + DN_WIDTH, D_MODEL), lambda i: (0, 0)),
                  pl.BlockSpec((1, D_MODEL), lambda i: (0, 0))],
        out_specs=[pl.BlockSpec((tm, D_MODEL), lambda i: (i, 0)),
                   pl.BlockSpec((tm, D_MODEL), lambda i: (i, 0))],
        out_shape=[jax.ShapeDtypeStruct((T, D_MODEL), F32),
                   jax.ShapeDtypeStruct((T, D_MODEL), BF16)],
        compiler_params=pltpu.CompilerParams(dimension_semantics=("arbitrary",),
                                             vmem_limit_bytes=VMEM_LIMIT_BYTES),
        name="outproj",
    )(x2, att_n, dn_o, w_out, g)


def _oddeven_merge(lo, hi, r):
    step = r * 2
    if step < hi - lo:
        yield from _oddeven_merge(lo, hi, step)
        yield from _oddeven_merge(lo + r, hi, step)
        for i in range(lo + r, hi - r, step):
            yield (i, i + r)
    else:
        yield (lo, lo + r)


def _oddeven_sort_net(lo, hi):
    if hi - lo >= 1:
        mid = lo + (hi - lo) // 2
        yield from _oddeven_sort_net(lo, mid)
        yield from _oddeven_sort_net(mid + 1, hi)
        yield from _oddeven_merge(lo, hi, 1)


_SORT16 = tuple(_oddeven_sort_net(0, PEER_TOPK - 1))
_BITONIC16 = tuple((i, i + d) for d in (8, 4, 2, 1) for i in range(PEER_TOPK) if (i & d) == 0)
_STAIR = tuple((r, q) for r in range(PEER_TOPK) for q in range(PEER_TOPK) if (r + 1) * (q + 1) <= PEER_TOPK)


def _ce_vi(a, b):
    va, ia = a
    vb, ib = b
    a_first = (va > vb) | ((va == vb) & (ia < ib))
    return ((jnp.where(a_first, va, vb), jnp.where(a_first, ia, ib)),
            (jnp.where(a_first, vb, va), jnp.where(a_first, ib, ia)))


def _first_vi(a, b):
    va, ia = a
    vb, ib = b
    a_first = (va > vb) | ((va == vb) & (ia < ib))
    return (jnp.where(a_first, va, vb), jnp.where(a_first, ia, ib))


def _sort16_vi(items):
    items = list(items)
    for i, j in _SORT16:
        items[i], items[j] = _ce_vi(items[i], items[j])
    return items


def _merge_top16_vi(a, b):
    n = PEER_TOPK
    items = [_first_vi(a[i], b[n - 1 - i]) for i in range(n)]
    for i, j in _BITONIC16:
        items[i], items[j] = _ce_vi(items[i], items[j])
    return items


def _sort16_v(vals):
    vals = list(vals)
    for i, j in _SORT16:
        hi = jnp.maximum(vals[i], vals[j])
        lo = jnp.minimum(vals[i], vals[j])
        vals[i], vals[j] = hi, lo
    return vals


def _merge_top16_v(a, b):
    n = PEER_TOPK
    vals = [jnp.maximum(a[i], b[n - 1 - i]) for i in range(n)]
    for i, j in _BITONIC16:
        hi = jnp.maximum(vals[i], vals[j])
        lo = jnp.minimum(vals[i], vals[j])
        vals[i], vals[j] = hi, lo
    return vals


def _top16_vi_of_128(load_key):
    groups = []
    for gi in range(N_KEYS // PEER_TOPK):
        items = [(load_key(gi * PEER_TOPK + t), jnp.full((SUBLANES, LANES), float(gi * PEER_TOPK + t), F32))
                 for t in range(PEER_TOPK)]
        groups.append(_sort16_vi(items))
    while len(groups) > 1:
        groups = [_merge_top16_vi(groups[2 * t], groups[2 * t + 1]) for t in range(len(groups) // 2)]
    return groups[0]


def _top16_v_of_128(load_key):
    groups = [_sort16_v([load_key(gi * PEER_TOPK + t) for t in range(PEER_TOPK)])
              for gi in range(N_KEYS // PEER_TOPK)]
    while len(groups) > 1:
        groups = [_merge_top16_v(groups[2 * t], groups[2 * t + 1]) for t in range(len(groups) // 2)]
    return groups[0]


def _top16_is_distinct(load_key, vals):
    zero = jnp.zeros((SUBLANES, LANES), F32)
    one = jnp.ones((SUBLANES, LANES), F32)
    n_ge = zero
    for k in range(N_KEYS):
        n_ge = n_ge + jnp.where(load_key(k) >= vals[PEER_TOPK - 1], one, zero)
    ok = n_ge == float(PEER_TOPK)
    for r in range(PEER_TOPK - 1):
        ok = ok & (vals[r] > vals[r + 1])
    return jnp.where(ok, one, zero)


def _select_pairs(av, bv):
    zero = jnp.zeros((SUBLANES, LANES), F32)
    one = jnp.ones((SUBLANES, LANES), F32)
    cand = {(r, q): av[r] + bv[q] for (r, q) in _STAIR}
    row0 = [cand[(0, q)] for q in range(PEER_TOPK)]
    rest = [cand[rq] for rq in _STAIR if rq[0] > 0]
    neg_inf = jnp.full((SUBLANES, LANES), -jnp.inf, F32)
    best = row0
    for s in range(0, len(rest), PEER_TOPK):
        grp = rest[s:s + PEER_TOPK]
        grp = grp + [neg_inf] * (PEER_TOPK - len(grp))
        best = _merge_top16_v(best, _sort16_v(grp))
    thr = best[PEER_TOPK - 1]
    zsum = one
    for jj in range(1, PEER_TOPK):
        zsum = zsum + jnp.exp(best[jj] - best[0])
    inv_z_half = 0.5 / zsum

    n_gt = zero
    for rq in _STAIR:
        n_gt = n_gt + jnp.where(cand[rq] > thr, one, zero)
    need = float(PEER_TOPK) - n_gt
    cnt = zero
    n_row = [zero] * PEER_TOPK
    for (r, q) in _STAIR:
        c = cand[(r, q)]
        eq = c == thr
        take = (c > thr) | (eq & (cnt < need))
        cnt = cnt + jnp.where(eq, one, zero)
        n_row[r] = n_row[r] + jnp.where(take, one, zero)
    return n_row, inv_z_half


def _match_count(keys, probe, n_row):
    n_k = jnp.zeros((SUBLANES, LANES), F32)
    for r in reversed(range(PEER_TOPK)):
        n_k = jnp.where(keys[r] == probe, n_row[r], n_k)
    return n_k


def _match_rank(probe, rows):
    rk = jnp.full(probe.shape, float(PEER_TOPK), F32)
    for q in reversed(range(PEER_TOPK)):
        rk = jnp.where(rows[q] == probe, jnp.full_like(rk, float(q)), rk)
    return rk


def _peer_topk_kernel(xn_ref, wqt_ref, k0_ref, k1_ref, r1_ref, e1_ref, n_ref, c_ref, s0_ref, s1_ref):
    tb = xn_ref.shape[0]
    half_w = PEER_HEADS * PEER_HALF
    qt = _dot_nt(wqt_ref[...], xn_ref[...]).astype(BF16)
    s0 = _dot(k0_ref[...], qt[0:half_w])
    s1 = _dot(k1_ref[...], qt[half_w:2 * half_w])
    for g in range(tb // LANES):
        s0_ref[g] = s0[:, g * LANES:(g + 1) * LANES]
        s1_ref[g] = s1[:, g * LANES:(g + 1) * LANES]

    sub_iota = lax.broadcasted_iota(jnp.int32, (_ROWS16, LANES), 0).astype(F32)

    def slab(gi, carry):
        ls = pl.ds(pl.multiple_of(gi * LANES, LANES), LANES)
        load0 = lambda k: s0_ref[gi, pl.ds(k * SUBLANES, SUBLANES), :]
        load1 = lambda k: s1_ref[gi, pl.ds(k * SUBLANES, SUBLANES), :]
        load1_head = lambda h, kb: s1_ref[gi, pl.ds(kb * _ROWS16 * SUBLANES + h, _ROWS16, stride=SUBLANES), :]

        av = _top16_v_of_128(load0)
        bv = _top16_v_of_128(load1)
        n_row, inv_z_half = _select_pairs(av, bv)
        distinct = _top16_is_distinct(load0, av) * _top16_is_distinct(load1, bv)
        has_tie = jnp.min(distinct) < 0.5

        def emit(keys0, probe0, keys1, probe1):
            for k in range(N_KEYS):
                rs = pl.ds(k * SUBLANES, SUBLANES)
                s0k = load0(k)
                n_ref[gi, rs, :] = _match_count(keys0, probe0(k, s0k), n_row)
                c_ref[gi, rs, :] = jnp.exp(s0k - av[0]) * inv_z_half
            for h in range(PEER_HEADS):
                rows = [jnp.broadcast_to(keys1[q][h:h + 1, :], (_ROWS16, LANES)) for q in range(PEER_TOPK)]
                b0h = jnp.broadcast_to(bv[0][h:h + 1, :], (_ROWS16, LANES))
                for kb in range(N_KEYS // _ROWS16):
                    rs = pl.ds(h * N_KEYS + kb * _ROWS16, _ROWS16)
                    tile = load1_head(h, kb)
                    r1_ref[rs, ls] = _match_rank(probe1(kb, tile), rows).astype(BF16)
                    e1_ref[rs, ls] = jnp.exp(tile - b0h).astype(BF16)

        @pl.when(jnp.logical_not(has_tie))
        def _():
            emit(av, lambda k, s0k: s0k, bv, lambda kb, tile: tile)

        @pl.when(has_tie)
        def _():
            a = _top16_vi_of_128(load0)
            b = _top16_vi_of_128(load1)
            emit([it[1] for it in a], lambda k, s0k: float(k),
                 [it[1] for it in b], lambda kb, tile: sub_iota + float(kb * _ROWS16))

        return carry

    lax.fori_loop(0, tb // LANES, slab, 0)


def _peer_topk(xn2, wqt, k0, k1, tb):
    T = xn2.shape[0]
    rows = PEER_HEADS * N_KEYS
    kspec = pl.BlockSpec((rows, PEER_HEADS * PEER_HALF), lambda i: (0, 0))
    ospec = pl.BlockSpec((rows, tb), lambda i: (0, i))
    slab_spec = pl.BlockSpec((tb // LANES, rows, LANES), lambda i: (i, 0, 0))
    return pl.pallas_call(
        _peer_topk_kernel,
        grid=(T // tb,),
        in_specs=[pl.BlockSpec((tb, D_MODEL), lambda i: (i, 0)),
                  pl.BlockSpec((D_MODEL, D_MODEL), lambda i: (0, 0)),
                  kspec, kspec],
        out_specs=[ospec, ospec, slab_spec, slab_spec],
        out_shape=[jax.ShapeDtypeStruct((rows, T), BF16),
                   jax.ShapeDtypeStruct((rows, T), BF16),
                   jax.ShapeDtypeStruct((T // LANES, rows, LANES), F32),
                   jax.ShapeDtypeStruct((T // LANES, rows, LANES), F32)],
        scratch_shapes=[pltpu.VMEM((tb // LANES, rows, LANES), F32),
                        pltpu.VMEM((tb // LANES, rows, LANES), F32)],
        compiler_params=pltpu.CompilerParams(dimension_semantics=("arbitrary",),
                                             vmem_limit_bytes=VMEM_LIMIT_BYTES),
        name="peer_topk",
    )(xn2, wqt, k0, k1)


_SUB_E = 512
_W_TILE = 256


def _peer_dense_kernel(xn_ref, xn_ahead_ref, h_ref, r1_ref, e1_ref, n_ref, c_ref, u_ref, u_ahead_ref,
                       vt_ref, o_ref, acc_ref, hm_ref, at_ref):
    i = pl.program_id(0)
    c = pl.program_id(1)
    tb = xn_ref.shape[0]
    n_sub = vt_ref.shape[0]
    i0_per_sub = _SUB_E // N_KEYS
    xn = xn_ref[...]

    def gelu2(a):
        return (a * (1.0 + lax.erf(a * 0.7071067811865476))).astype(BF16)

    @pl.when(c == 0)
    def _():
        acc_ref[...] = jnp.zeros_like(acc_ref)

    @pl.when((c == 0) & (i == 0))
    def _():
        at_ref[0] = gelu2(_dot_nt(u_ref[0:_SUB_E, :], xn))

    tw = min(tb, _W_TILE)
    zero16 = jnp.zeros((_ROWS16, tw), BF16)
    n_j = N_KEYS // _ROWS16

    def gated_acts(sub, slot, out_ref):
        for ii in range(i0_per_sub):
            i0 = (c * n_sub + sub) * i0_per_sub + ii
            for lt in range(tb // tw):
                ls = slice(lt * tw, (lt + 1) * tw)
                wacc = [None] * n_j
                for h in range(PEER_HEADS):
                    rowi = pl.ds(i0 * PEER_HEADS + h, SUBLANES, stride=0)
                    slabs = range(lt * tw // LANES, (lt + 1) * tw // LANES)
                    n8 = jnp.concatenate([n_ref[g, rowi, :] for g in slabs], axis=1)
                    c8 = jnp.concatenate([c_ref[g, rowi, :] for g in slabs], axis=1)
                    nb = jnp.concatenate([n8, n8], axis=0).astype(BF16)
                    cb = jnp.concatenate([c8, c8], axis=0).astype(BF16)
                    for j in range(n_j):
                        rs = slice(h * N_KEYS + j * _ROWS16, h * N_KEYS + (j + 1) * _ROWS16)
                        term = jnp.where(r1_ref[rs, ls] < nb, e1_ref[rs, ls] * cb, zero16)
                        wacc[j] = term if wacc[j] is None else wacc[j] + term
                for j in range(n_j):
                    er = slice(ii * N_KEYS + j * _ROWS16, ii * N_KEYS + (j + 1) * _ROWS16)
                    out_ref[er, ls] = at_ref[slot, er, ls] * wacc[j]

    for sub in range(n_sub):
        par = sub % 2
        if sub + 1 < n_sub:
            at_ref[1 - par] = gelu2(_dot_nt(u_ref[(sub + 1) * _SUB_E:(sub + 2) * _SUB_E, :], xn))
        else:
            at_ref[1 - par] = gelu2(_dot_nt(u_ahead_ref[...], xn_ahead_ref[...]))
        if sub > 0:
            acc_ref[...] += _dot(vt_ref[sub - 1], hm_ref[1 - par])
        gated_acts(sub, par, hm_ref.at[par])
    acc_ref[...] += _dot(vt_ref[n_sub - 1], hm_ref[(n_sub - 1) % 2])

    @pl.when(c == pl.num_programs(1) - 1)
    def _():
        o_ref[...] = h_ref[...] + acc_ref[...].T


def _peer_dense(xn2, h, r1, e1, n_sel, c_gate, u16, vt_slabs, tb, ec):
    T = xn2.shape[0]
    rows = PEER_HEADS * N_KEYS
    n_sub = ec // _SUB_E
    assert n_sub >= 2
    tspec = pl.BlockSpec((rows, tb), lambda i, c: (0, i))
    slab_spec = pl.BlockSpec((tb // LANES, rows, LANES), lambda i, c: (i, 0, 0))
    n_tok_blocks = T // tb
    n_chunks = N_EXPERTS // ec
    n_slabs = N_EXPERTS // _SUB_E

    def ahead_tokens(i, c):
        return (jnp.where(c == n_chunks - 1, jnp.minimum(i + 1, n_tok_blocks - 1), i), 0)

    return pl.pallas_call(
        _peer_dense_kernel,
        grid=(n_tok_blocks, n_chunks),
        in_specs=[pl.BlockSpec((tb, D_MODEL), lambda i, c: (i, 0)),
                  pl.BlockSpec((tb, D_MODEL), ahead_tokens),
                  pl.BlockSpec((tb, D_MODEL), lambda i, c: (i, 0)),
                  tspec, tspec, slab_spec, slab_spec,
                  pl.BlockSpec((ec, D_MODEL), lambda i, c: (c, 0)),
                  pl.BlockSpec((_SUB_E, D_MODEL), lambda i, c: (((c + 1) * n_sub) % n_slabs, 0)),
                  pl.BlockSpec((n_sub, D_MODEL, _SUB_E), lambda i, c: (c, 0, 0))],
        out_specs=pl.BlockSpec((tb, D_MODEL), lambda i, c: (i, 0)),
        out_shape=jax.ShapeDtypeStruct((T, D_MODEL), F32),
        scratch_shapes=[pltpu.VMEM((D_MODEL, tb), F32),
                        pltpu.VMEM((2, _SUB_E, tb), BF16),
                        pltpu.VMEM((2, _SUB_E, tb), BF16)],
        compiler_params=pltpu.CompilerParams(dimension_semantics=("arbitrary", "arbitrary"),
                                             vmem_limit_bytes=VMEM_LIMIT_BYTES),
        name="peer_dense",
    )(xn2, xn2, h, r1, e1, n_sel, c_gate, u16, u16, vt_slabs)


def _block_diag_ones(width, blk):
    idx = np.arange(width) // blk
    return jnp.asarray((idx[:, None] == idx[None, :]).astype(np.float32), dtype=BF16)


def _layer(x, norm_mix_g, w_in, att_q_norm_g, att_k_norm_g, att_sinks, att_out_norm_g, dn_conv_w,
           dn_a_log, dn_dt_bias, dn_out_norm_g, w_out, norm_ffn_g, peer_w_q, peer_sub_keys, peer_u, peer_v):
    B, S, _ = x.shape
    T = B * S
    x2 = x.reshape(T, D_MODEL)

    o_q, o_k, o_v = 0, ATT_WIDTH, ATT_WIDTH + ATT_KV_WIDTH
    o_dn = ATT_WIDTH + 2 * ATT_KV_WIDTH
    o_a = o_dn + 3 * DN_WIDTH
    o_b = o_a + DN_HEADS
    o_z = o_b + DN_HEADS
    pad = jnp.zeros((D_MODEL, LANES - DN_HEADS), w_in.dtype)
    w_all = jnp.concatenate([w_in[:, o_q:o_dn], w_in[:, o_dn:o_a], w_in[:, o_z:o_z + DN_WIDTH],
                             w_in[:, o_a:o_b], pad, w_in[:, o_b:o_z], pad], axis=1).astype(BF16)
    del o_k, o_v

    def lane_pad(vec):
        return jnp.concatenate([vec.astype(F32), jnp.zeros((LANES - vec.shape[0],), F32)]).reshape(1, LANES)

    assert T % PROJ_TOKENS == 0 and S % ATT_TOKENS == 0 and S % GDN_CHUNK == 0 and B % GDN_SEQS == 0
    assert T % PEER_TOKENS == 0
    att_in, dn_in, z_in, ab_in = _inproj(x2, norm_mix_g.reshape(1, D_MODEL), w_all, tm=PROJ_TOKENS)

    att_n = _attention(
        att_in, att_sinks.astype(F32),
        jnp.tile(att_q_norm_g.astype(F32), ATT_HEADS).reshape(1, ATT_WIDTH),
        jnp.tile(att_k_norm_g.astype(F32), ATT_KV_HEADS).reshape(1, ATT_KV_WIDTH),
        att_out_norm_g.astype(F32).reshape(1, ATT_WIDTH),
        _block_diag_ones(ATT_WIDTH, ATT_HEAD_DIM), _block_diag_ones(ATT_KV_WIDTH, ATT_HEAD_DIM),
        seq=S, tq=ATT_TOKENS)

    tri = jnp.asarray(np.tril(np.ones((GDN_CHUNK, GDN_CHUNK), np.float32)))
    dn_o = _gdn(dn_in, ab_in, z_in, dn_conv_w.astype(F32),
                lane_pad(-jnp.exp(dn_a_log.astype(F32))), lane_pad(dn_dt_bias),
                dn_out_norm_g.astype(F32).reshape(1, DN_HEAD_DIM), tri, batch=B, seq=S, nb=GDN_SEQS)

    h, xn2 = _outproj(x2, att_n, dn_o, w_out.astype(BF16), norm_ffn_g.reshape(1, D_MODEL), tm=PROJ_TOKENS)

    wqt = (peer_w_q.T.reshape(PEER_HEADS, 2, PEER_HALF, D_MODEL).transpose(1, 0, 2, 3)
           .reshape(D_MODEL, D_MODEL).astype(BF16))
    eye = jnp.eye(PEER_HEADS, dtype=peer_sub_keys.dtype)
    kd = jnp.einsum('hpkc,hg->pkhgc', peer_sub_keys, eye)
    k_km = kd.reshape(2, N_KEYS * PEER_HEADS, PEER_HEADS * PEER_HALF).astype(BF16)
    r1, e1, n_sel, c_gate = _peer_topk(xn2, wqt, k_km[0], k_km[1], tb=PEER_TOKENS)

    vt_slabs = (peer_v.reshape(N_EXPERTS // _SUB_E, _SUB_E, D_MODEL).transpose(0, 2, 1).astype(BF16))
    out = _peer_dense(xn2, h, r1, e1, n_sel, c_gate, peer_u.astype(BF16), vt_slabs,
                      tb=PEER_TOKENS, ec=PEER_CHUNK)
    return out.reshape(B, S, D_MODEL)


def kernel(x, norm_mix_g, w_in, att_q_norm_g, att_k_norm_g, att_sinks, att_out_norm_g, dn_conv_w, dn_a_log, dn_dt_bias, dn_out_norm_g, w_out, norm_ffn_g, peer_w_q, peer_sub_keys, peer_u, peer_v):
    h = x
    for l in range(norm_mix_g.shape[0]):
        h = _layer(h, norm_mix_g[l], w_in[l], att_q_norm_g[l], att_k_norm_g[l], att_sinks[l],
                   att_out_norm_g[l], dn_conv_w[l], dn_a_log[l], dn_dt_bias[l], dn_out_norm_g[l],
                   w_out[l], norm_ffn_g[l], peer_w_q[l], peer_sub_keys[l], peer_u[l], peer_v[l])
    return h
```

```python
import functools

import numpy as np
import jax
import jax.numpy as jnp
from jax import lax
from jax.experimental import pallas as pl
from jax.experimental.pallas import tpu as pltpu

F32 = jnp.float32
BF16 = jnp.bfloat16

D_MODEL = 1024
ATT_HEADS = 8
ATT_KV_HEADS = 2
ATT_HEAD_DIM = 64
ATT_WIDTH = 512
ATT_KV_WIDTH = 128
ATT_BLOCK = 128
DN_HEADS = 4
DN_HEAD_DIM = 128
DN_WIDTH = 512
CONV_WIDTH = 4
PEER_HEADS = 8
N_KEYS = 128
N_EXPERTS = N_KEYS * N_KEYS
PEER_HALF = 64
PEER_TOPK = 16
EPS = 1e-6

LANES = 128
SUBLANES = 8
V7X_VMEM_BYTES = 64 * 1024 * 1024
VMEM_LIMIT_BYTES = V7X_VMEM_BYTES - 8 * 1024 * 1024

PROJ_TOKENS = 512
ATT_TOKENS = 1024
GDN_SEQS = 4
PEER_TOKENS = 512
PEER_CHUNK = 2048

ATT_IN_WIDTH = ATT_WIDTH + 2 * ATT_KV_WIDTH

_C_ATT = 0
_C_DN = ATT_IN_WIDTH
_C_Z = _C_DN + 3 * DN_WIDTH
_C_AB = _C_Z + DN_WIDTH
_C_END = _C_AB + 2 * LANES

GDN_CHUNK = 128
_ROWS16 = 2 * SUBLANES


def _dot(a, b):
    return jnp.dot(a, b, preferred_element_type=F32)


def _dot_nt(a, b):
    return lax.dot_general(a, b, (((1,), (1,)), ((), ())), preferred_element_type=F32)


def _split2(a):
    hi = a.astype(BF16)
    lo = (a - hi.astype(F32)).astype(BF16)
    return hi, lo


def _dot3s(a_split, b_split):
    ah, al = a_split
    bh, bl = b_split
    return _dot(ah, bh) + (_dot(ah, bl) + _dot(al, bh))


def _dot3(a, b):
    return _dot3s(_split2(a), _split2(b))


def _inproj_kernel(x_ref, g_ref, w_ref, att_ref, dn_ref, z_ref, ab_ref):
    x = x_ref[...]
    ms = jnp.mean(x * x, axis=-1, keepdims=True)
    xn = (x * lax.rsqrt(ms + EPS) * g_ref[...]).astype(BF16)
    att_ref[...] = _dot(xn, w_ref[:, _C_ATT:_C_DN])
    dn_ref[...] = _dot(xn, w_ref[:, _C_DN:_C_Z])
    z_ref[...] = _dot(xn, w_ref[:, _C_Z:_C_AB])
    ab_ref[...] = _dot(xn, w_ref[:, _C_AB:_C_END])


def _inproj(x2, g, w_all, tm):
    T = x2.shape[0]
    return pl.pallas_call(
        _inproj_kernel,
        grid=(T // tm,),
        in_specs=[pl.BlockSpec((tm, D_MODEL), lambda i: (i, 0)),
                  pl.BlockSpec((1, D_MODEL), lambda i: (0, 0)),
                  pl.BlockSpec((D_MODEL, _C_END), lambda i: (0, 0))],
        out_specs=[pl.BlockSpec((tm, ATT_IN_WIDTH), lambda i: (i, 0)),
                   pl.BlockSpec((tm, 3 * DN_WIDTH), lambda i: (i, 0)),
                   pl.BlockSpec((tm, DN_WIDTH), lambda i: (i, 0)),
                   pl.BlockSpec((tm, 2 * LANES), lambda i: (i, 0))],
        out_shape=[jax.ShapeDtypeStruct((T, ATT_IN_WIDTH), F32),
                   jax.ShapeDtypeStruct((T, 3 * DN_WIDTH), F32),
                   jax.ShapeDtypeStruct((T, DN_WIDTH), F32),
                   jax.ShapeDtypeStruct((T, 2 * LANES), F32)],
        compiler_params=pltpu.CompilerParams(dimension_semantics=("arbitrary",),
                                             vmem_limit_bytes=VMEM_LIMIT_BYTES),
        name="inproj",
    )(x2, g, w_all)


def _head_rms(t, bd, g):
    hi, lo = _split2(t * t)
    ss = _dot(hi, bd) + _dot(lo, bd)
    return t * lax.rsqrt(ss * (1.0 / ATT_HEAD_DIM) + EPS) * g


def _attn_kernel(sink_ref, cur_ref, prev_ref, gq_ref, gk_ref, go_ref, bdq_ref, bdk_ref, o_ref,
                 *, tiles_per_seq):
    i = pl.program_id(0)
    tq = cur_ref.shape[0]
    nblk = tq // ATT_BLOCK
    first = (i % tiles_per_seq) == 0

    q = cur_ref[:, 0:ATT_WIDTH]
    k = cur_ref[:, ATT_WIDTH:ATT_WIDTH + ATT_KV_WIDTH]
    v = cur_ref[:, ATT_WIDTH + ATT_KV_WIDTH:ATT_WIDTH + 2 * ATT_KV_WIDTH]
    kp = prev_ref[:, 0:ATT_KV_WIDTH]
    vp = prev_ref[:, ATT_KV_WIDTH:2 * ATT_KV_WIDTH]

    qn = (_head_rms(q, bdq_ref[...], gq_ref[...]) * (ATT_HEAD_DIM ** -0.5)).astype(BF16)
    kn = jnp.concatenate([_head_rms(kp, bdk_ref[...], gk_ref[...]),
                          _head_rms(k, bdk_ref[...], gk_ref[...])], axis=0)
    vf = jnp.concatenate([vp, v], axis=0)

    lane = lax.broadcasted_iota(jnp.int32, kn.shape, 1)
    left = lane < ATT_HEAD_DIM
    zero = jnp.zeros_like(kn)
    k0l = jnp.where(left, kn, zero)
    k1r = jnp.where(left, zero, kn)
    v0l = jnp.where(left, vf, zero)
    v1r = jnp.where(left, zero, vf)
    kl = (k0l.astype(BF16), pltpu.roll(k1r, ATT_HEAD_DIM, 1).astype(BF16))
    kr = (pltpu.roll(k0l, ATT_HEAD_DIM, 1).astype(BF16), k1r.astype(BF16))
    vl = (v0l.astype(BF16), pltpu.roll(v1r, ATT_HEAD_DIM, 1).astype(BF16))
    vr = (pltpu.roll(v0l, ATT_HEAD_DIM, 1).astype(BF16), v1r.astype(BF16))

    qi = lax.broadcasted_iota(jnp.int32, (ATT_BLOCK, 2 * ATT_BLOCK), 0)
    kj = lax.broadcasted_iota(jnp.int32, (ATT_BLOCK, 2 * ATT_BLOCK), 1)
    rel = qi + ATT_BLOCK - kj
    in_window = (rel >= 0) & (rel < ATT_BLOCK)
    first_key = jnp.where(first, ATT_BLOCK, 0)
    neg_inf = jnp.full((ATT_BLOCK, 2 * ATT_BLOCK), -jnp.inf, F32)

    def softmax_rows(s, sink):
        m = jnp.maximum(jnp.max(s, axis=-1, keepdims=True), sink)
        p = jnp.exp(s - m)
        den = jnp.sum(p, axis=-1, keepdims=True) + jnp.exp(sink - m)
        return (p * (1.0 / den)).astype(BF16)

    for j in range(nblk):
        rows = slice(j * ATT_BLOCK, (j + 1) * ATT_BLOCK)
        krows = slice(j * ATT_BLOCK, (j + 2) * ATT_BLOCK)
        if j == 0:
            mask = in_window & (kj >= first_key)
        else:
            mask = in_window
        pairs = []
        for c in range(ATT_KV_HEADS):
            qe = jnp.concatenate([qn[rows, (2 * c) * LANES:(2 * c + 1) * LANES],
                                  qn[rows, (2 * c + 1) * LANES:(2 * c + 2) * LANES]], axis=0)
            s_even = _dot_nt(qe, kl[c][krows])
            s_odd = _dot_nt(qe, kr[c][krows])
            for half in range(2):
                hr = slice(half * ATT_BLOCK, (half + 1) * ATT_BLOCK)
                h_even = 4 * c + 2 * half
                p_e = softmax_rows(jnp.where(mask, s_even[hr], neg_inf), sink_ref[h_even])
                p_o = softmax_rows(jnp.where(mask, s_odd[hr], neg_inf), sink_ref[h_even + 1])
                pairs.append(_dot(p_e, vl[c][krows]) + _dot(p_o, vr[c][krows]))
        att = jnp.concatenate(pairs, axis=1)
        ms = jnp.mean(att * att, axis=-1, keepdims=True)
        o_ref[rows, :] = (att * lax.rsqrt(ms + EPS) * go_ref[...]).astype(BF16)


def _attention(att_in, sinks, gq, gk, go, bdq, bdk, seq, tq):
    T = att_in.shape[0]
    tiles_per_seq = seq // tq
    blk_per_tile = tq // ATT_BLOCK
    kv_col_blk = ATT_WIDTH // (2 * ATT_KV_WIDTH)
    return pl.pallas_call(
        functools.partial(_attn_kernel, tiles_per_seq=tiles_per_seq),
        grid_spec=pltpu.PrefetchScalarGridSpec(
            num_scalar_prefetch=1,
            grid=(T // tq,),
            in_specs=[pl.BlockSpec((tq, ATT_IN_WIDTH), lambda i, s: (i, 0)),
                      pl.BlockSpec((ATT_BLOCK, 2 * ATT_KV_WIDTH),
                                   lambda i, s: (jnp.maximum(i * blk_per_tile - 1, 0), kv_col_blk)),
                      pl.BlockSpec((1, ATT_WIDTH), lambda i, s: (0, 0)),
                      pl.BlockSpec((1, ATT_KV_WIDTH), lambda i, s: (0, 0)),
                      pl.BlockSpec((1, ATT_WIDTH), lambda i, s: (0, 0)),
                      pl.BlockSpec((ATT_WIDTH, ATT_WIDTH), lambda i, s: (0, 0)),
                      pl.BlockSpec((ATT_KV_WIDTH, ATT_KV_WIDTH), lambda i, s: (0, 0))],
            out_specs=pl.BlockSpec((tq, ATT_WIDTH), lambda i, s: (i, 0)),
        ),
        out_shape=jax.ShapeDtypeStruct((T, ATT_WIDTH), BF16),
        compiler_params=pltpu.CompilerParams(dimension_semantics=("arbitrary",),
                                             vmem_limit_bytes=VMEM_LIMIT_BYTES),
        name="swa_attention",
    )(sinks, att_in, att_in, gq, gk, go, bdq, bdk)


def _unit_lower_inverses(m_lows, row, col):
    n = m_lows[0].shape[0]
    zero = jnp.zeros_like(m_lows[0])
    zero16 = jnp.zeros(zero.shape, BF16)
    eye = jnp.where(row == col, jnp.ones_like(zero), zero)
    same = (row >> 3) == (col >> 3)
    m16 = [m.astype(BF16) for m in m_lows]
    nm16 = [jnp.where(same, -mh, zero16) for mh in m16]
    ps = [eye + jnp.where(same, -m, zero) for m in m_lows]
    n2_16 = [_dot(nm, nm).astype(BF16) for nm in nm16]
    ps = [p + _dot(p.astype(BF16), n2) for p, n2 in zip(ps, n2_16)]
    n4_16 = [_dot(n2, n2).astype(BF16) for n2 in n2_16]
    ps = [p + _dot(p.astype(BF16), n4) for p, n4 in zip(ps, n4_16)]
    shift = 3
    while (1 << shift) < n:
        same_next = (row >> (shift + 1)) == (col >> (shift + 1))
        lower_left = same_next & jnp.logical_not(same)
        p16 = [p.astype(BF16) for p in ps]
        ts = [_dot(ph, jnp.where(lower_left, mh, zero16)) for ph, mh in zip(p16, m16)]
        ps = [p - _dot(t.astype(BF16), ph) for p, t, ph in zip(ps, ts, p16)]
        same = same_next
        shift += 1
    return ps


def _silu(v):
    return v * (1.0 / (1.0 + jnp.exp(-v)))


def _gdn_kernel(dn_ref, ab_ref, z_ref, cw_ref, nega_ref, dtb_ref, gn_ref, tri_ref, o_ref,
                state_ref, carry_ref):
    t = pl.program_id(1)
    C = GDN_CHUNK
    nb = dn_ref.shape[0]

    @pl.when(t == 0)
    def _():
        state_ref[...] = jnp.zeros_like(state_ref)
        carry_ref[...] = jnp.zeros_like(carry_ref)

    row = lax.broadcasted_iota(jnp.int32, (C, C), 0)
    col = lax.broadcasted_iota(jnp.int32, (C, C), 1)
    causal = row >= col
    strict = row > col
    zero_cc = jnp.zeros((C, C), F32)
    cw = cw_ref[...]
    row8 = lax.broadcasted_iota(jnp.int32, (SUBLANES, 3 * DN_WIDTH), 0)

    ids, qs, ks, vbs, kbes, decs, egs, kdecs, cds = [], [], [], [], [], [], [], [], []
    for bb in range(nb):
        x = dn_ref[bb]
        tail = carry_ref[bb]
        carry_ref[bb] = x[C - SUBLANES:C, :]
        y = x * cw[CONV_WIDTH - 1:CONV_WIDTH, :]
        for back in range(1, CONV_WIDTH):
            xr = pltpu.roll(x, back, 0)
            top = jnp.where(row8 < back, pltpu.roll(tail, back, 0), xr[0:SUBLANES])
            shifted = jnp.concatenate([top, xr[SUBLANES:]], axis=0)
            y = y + shifted * cw[CONV_WIDTH - 1 - back:CONV_WIDTH - back, :]
        y = _silu(y)

        ab = ab_ref[bb]
        sp_in = ab[:, 0:LANES] + dtb_ref[...]
        softplus = jnp.maximum(sp_in, 0.0) + jnp.log1p(jnp.exp(-jnp.abs(sp_in)))
        g_all = nega_ref[...] * softplus
        beta_all = 1.0 / (1.0 + jnp.exp(-ab[:, LANES:2 * LANES]))
        gc_all = _dot3(tri_ref[...], g_all)
        gc_t = gc_all.T

        for h in range(DN_HEADS):
            qh = y[:, h * DN_HEAD_DIM:(h + 1) * DN_HEAD_DIM]
            kh = y[:, DN_WIDTH + h * DN_HEAD_DIM:DN_WIDTH + (h + 1) * DN_HEAD_DIM]
            vh = y[:, 2 * DN_WIDTH + h * DN_HEAD_DIM:2 * DN_WIDTH + (h + 1) * DN_HEAD_DIM]
            qh = qh * lax.rsqrt(jnp.sum(qh * qh, axis=-1, keepdims=True) + EPS) * (DN_HEAD_DIM ** -0.5)
            kh = kh * lax.rsqrt(jnp.sum(kh * kh, axis=-1, keepdims=True) + EPS)
            gcol = gc_all[:, h:h + 1]
            grow = gc_t[h:h + 1, :]
            bcol = beta_all[:, h:h + 1]
            glast = gc_all[C - 1:C, h:h + 1]
            eg = jnp.exp(gcol)
            kb = kh * bcol
            ids.append((bb, h))
            qs.append(qh)
            ks.append(kh.astype(BF16))
            vbs.append(jnp.concatenate([vh * bcol, kb * eg], axis=1).astype(BF16))
            kbes.append(kb.astype(BF16))
            decs.append(jnp.where(causal, jnp.exp(jnp.where(causal, gcol - grow, zero_cc)), zero_cc))
            egs.append(eg)
            kdecs.append((kh * jnp.exp(glast - gcol)).T.astype(BF16))
            cds.append(jnp.exp(glast))

    n = len(ids)
    m_lows = [jnp.where(strict, _dot_nt(kbes[i], ks[i]) * decs[i], zero_cc) for i in range(n)]
    qks = [jnp.where(causal, _dot_nt(qs[i].astype(BF16), ks[i]) * decs[i], zero_cc).astype(BF16)
           for i in range(n)]
    tinvs = _unit_lower_inverses(m_lows, row, col)
    uws = [_dot(tinvs[i].astype(BF16), vbs[i]) for i in range(n)]
    s_olds = [state_ref[bb, h] for (bb, h) in ids]
    s16s = [s.astype(BF16) for s in s_olds]
    v16s = [(uws[i][:, 0:DN_HEAD_DIM] - _dot(uws[i][:, DN_HEAD_DIM:].astype(BF16), s16s[i])).astype(BF16)
            for i in range(n)]
    outs = [_dot((qs[i] * egs[i]).astype(BF16), s16s[i]) + _dot(qks[i], v16s[i]) for i in range(n)]
    for i, (bb, h) in enumerate(ids):
        state_ref[bb, h] = s_olds[i] * cds[i] + _dot(kdecs[i], v16s[i])

    for i, (bb, h) in enumerate(ids):
        hs = slice(h * DN_HEAD_DIM, (h + 1) * DN_HEAD_DIM)
        o = outs[i]
        ms = jnp.mean(o * o, axis=-1, keepdims=True)
        o_ref[bb, :, hs] = (o * lax.rsqrt(ms + EPS) * gn_ref[...] * _silu(z_ref[bb, :, hs])).astype(BF16)


def _gdn(dn_in, ab, z, conv_w, neg_a, dt_bias, gnorm, tri, batch, seq, nb):
    C = GDN_CHUNK
    nc = seq // C
    blk3 = lambda w: pl.BlockSpec((nb, C, w), lambda bi, t: (bi, t, 0))
    const = lambda shape: pl.BlockSpec(shape, lambda bi, t: (0, 0))
    out = pl.pallas_call(
        _gdn_kernel,
        grid=(batch // nb, nc),
        in_specs=[blk3(3 * DN_WIDTH), blk3(2 * LANES), blk3(DN_WIDTH),
                  const((CONV_WIDTH, 3 * DN_WIDTH)), const((1, LANES)), const((1, LANES)),
                  const((1, DN_HEAD_DIM)), const((C, C))],
        out_specs=blk3(DN_WIDTH),
        out_shape=jax.ShapeDtypeStruct((batch, seq, DN_WIDTH), BF16),
        scratch_shapes=[pltpu.VMEM((nb, DN_HEADS, DN_HEAD_DIM, DN_HEAD_DIM), F32),
                        pltpu.VMEM((nb, SUBLANES, 3 * DN_WIDTH), F32)],
        compiler_params=pltpu.CompilerParams(dimension_semantics=("arbitrary", "arbitrary"),
                                             vmem_limit_bytes=VMEM_LIMIT_BYTES),
        name="gated_deltanet",
    )(dn_in.reshape(batch, seq, -1), ab.reshape(batch, seq, -1), z.reshape(batch, seq, -1),
      conv_w, neg_a, dt_bias, gnorm, tri)
    return out.reshape(batch * seq, DN_WIDTH)


def _outproj_kernel(x_ref, att_ref, dn_ref, wo_ref, g_ref, h_ref, xn_ref):
    h = x_ref[...] + (_dot(att_ref[...], wo_ref[0:ATT_WIDTH, :])
                      + _dot(dn_ref[...], wo_ref[ATT_WIDTH:ATT_WIDTH + DN_WIDTH, :]))
    h_ref[...] = h
    ms = jnp.mean(h * h, axis=-1, keepdims=True)
    xn_ref[...] = (h * lax.rsqrt(ms + EPS) * g_ref[...]).astype(BF16)


def _outproj(x2, att_n, dn_o, w_out, g, tm):
    T = x2.shape[0]
    return pl.pallas_call(
        _outproj_kernel,
        grid=(T // tm,),
        in_specs=[pl.BlockSpec((tm, D_MODEL), lambda i: (i, 0)),
                  pl.BlockSpec((tm, ATT_WIDTH), lambda i: (i, 0)),
                  pl.BlockSpec((tm, DN_WIDTH), lambda i: (i, 0)),
                  pl.BlockSpec((ATT_WIDTH + DN_WIDTH, D_MODEL), lambda i: (0, 0)),
                  pl.BlockSpec((1, D_MODEL), lambda i: (0, 0))],
        out_specs=[pl.BlockSpec((tm, D_MODEL), lambda i: (i, 0)),
                   pl.BlockSpec((tm, D_MODEL), lambda i: (i, 0))],
        out_shape=[jax.ShapeDtypeStruct((T, D_MODEL), F32),
                   jax.ShapeDtypeStruct((T, D_MODEL), BF16)],
        compiler_params=pltpu.CompilerParams(dimension_semantics=("arbitrary",),
                                             vmem_limit_bytes=VMEM_LIMIT_BYTES),
        name="outproj",
    )(x2, att_n, dn_o, w_out, g)


def _oddeven_merge(lo, hi, r):
    step = r * 2
    if step < hi - lo:
        yield from _oddeven_merge(lo, hi, step)
        yield from _oddeven_merge(lo + r, hi, step)
        for i in range(lo + r, hi - r, step):
            yield (i, i + r)
    else:
        yield (lo, lo + r)


def _oddeven_sort_net(lo, hi):
    if hi - lo >= 1:
        mid = lo + (hi - lo) // 2
        yield from _oddeven_sort_net(lo, mid)
        yield from _oddeven_sort_net(mid + 1, hi)
        yield from _oddeven_merge(lo, hi, 1)


_SORT16 = tuple(_oddeven_sort_net(0, PEER_TOPK - 1))
_BITONIC16 = tuple((i, i + d) for d in (8, 4, 2, 1) for i in range(PEER_TOPK) if (i & d) == 0)
_STAIR = tuple((r, q) for r in range(PEER_TOPK) for q in range(PEER_TOPK) if (r + 1) * (q + 1) <= PEER_TOPK)


def _ce_vi(a, b):
    va, ia = a
    vb, ib = b
    a_first = (va > vb) | ((va == vb) & (ia < ib))
    return ((jnp.where(a_first, va, vb), jnp.where(a_first, ia, ib)),
            (jnp.where(a_first, vb, va), jnp.where(a_first, ib, ia)))


def _first_vi(a, b):
    va, ia = a
    vb, ib = b
    a_first = (va > vb) | ((va == vb) & (ia < ib))
    return (jnp.where(a_first, va, vb), jnp.where(a_first, ia, ib))


def _sort16_vi(items):
    items = list(items)
    for i, j in _SORT16:
        items[i], items[j] = _ce_vi(items[i], items[j])
    return items


def _merge_top16_vi(a, b):
    n = PEER_TOPK
    items = [_first_vi(a[i], b[n - 1 - i]) for i in range(n)]
    for i, j in _BITONIC16:
        items[i], items[j] = _ce_vi(items[i], items[j])
    return items


def _sort16_v(vals):
    vals = list(vals)
    for i, j in _SORT16:
        hi = jnp.maximum(vals[i], vals[j])
        lo = jnp.minimum(vals[i], vals[j])
        vals[i], vals[j] = hi, lo
    return vals


def _merge_top16_v(a, b):
    n = PEER_TOPK
    vals = [jnp.maximum(a[i], b[n - 1 - i]) for i in range(n)]
    for i, j in _BITONIC16:
        hi = jnp.maximum(vals[i], vals[j])
        lo = jnp.minimum(vals[i], vals[j])
        vals[i], vals[j] = hi, lo
    return vals


def _top16_vi_of_128(load_key):
    groups = []
    for gi in range(N_KEYS // PEER_TOPK):
        items = [(load_key(gi * PEER_TOPK + t), jnp.full((SUBLANES, LANES), float(gi * PEER_TOPK + t), F32))
                 for t in range(PEER_TOPK)]
        groups.append(_sort16_vi(items))
    while len(groups) > 1:
        groups = [_merge_top16_vi(groups[2 * t], groups[2 * t + 1]) for t in range(len(groups) // 2)]
    return groups[0]


def _top16_v_of_128(load_key):
    groups = [_sort16_v([load_key(gi * PEER_TOPK + t) for t in range(PEER_TOPK)])
              for gi in range(N_KEYS // PEER_TOPK)]
    while len(groups) > 1:
        groups = [_merge_top16_v(groups[2 * t], groups[2 * t + 1]) for t in range(len(groups) // 2)]
    return groups[0]


def _top16_is_distinct(load_key, vals):
    zero = jnp.zeros((SUBLANES, LANES), F32)
    one = jnp.ones((SUBLANES, LANES), F32)
    n_ge = zero
    for k in range(N_KEYS):
        n_ge = n_ge + jnp.where(load_key(k) >= vals[PEER_TOPK - 1], one, zero)
    ok = n_ge == float(PEER_TOPK)
    for r in range(PEER_TOPK - 1):
        ok = ok & (vals[r] > vals[r + 1])
    return jnp.where(ok, one, zero)


def _select_pairs(av, bv):
    zero = jnp.zeros((SUBLANES, LANES), F32)
    one = jnp.ones((SUBLANES, LANES), F32)
    cand = {(r, q): av[r] + bv[q] for (r, q) in _STAIR}
    row0 = [cand[(0, q)] for q in range(PEER_TOPK)]
    rest = [cand[rq] for rq in _STAIR if rq[0] > 0]
    neg_inf = jnp.full((SUBLANES, LANES), -jnp.inf, F32)
    best = row0
    for s in range(0, len(rest), PEER_TOPK):
        grp = rest[s:s + PEER_TOPK]
        grp = grp + [neg_inf] * (PEER_TOPK - len(grp))
        best = _merge_top16_v(best, _sort16_v(grp))
    thr = best[PEER_TOPK - 1]
    zsum = one
    for jj in range(1, PEER_TOPK):
        zsum = zsum + jnp.exp(best[jj] - best[0])
    inv_z_half = 0.5 / zsum

    n_gt = zero
    for rq in _STAIR:
        n_gt = n_gt + jnp.where(cand[rq] > thr, one, zero)
    need = float(PEER_TOPK) - n_gt
    cnt = zero
    n_row = [zero] * PEER_TOPK
    for (r, q) in _STAIR:
        c = cand[(r, q)]
        eq = c == thr
        take = (c > thr) | (eq & (cnt < need))
        cnt = cnt + jnp.where(eq, one, zero)
        n_row[r] = n_row[r] + jnp.where(take, one, zero)
    return n_row, inv_z_half


def _match_count(keys, probe, n_row):
    n_k = jnp.zeros((SUBLANES, LANES), F32)
    for r in reversed(range(PEER_TOPK)):
        n_k = jnp.where(keys[r] == probe, n_row[r], n_k)
    return n_k


def _match_rank(probe, rows):
    rk = jnp.full(probe.shape, float(PEER_TOPK), F32)
    for q in reversed(range(PEER_TOPK)):
        rk = jnp.where(rows[q] == probe, jnp.full_like(rk, float(q)), rk)
    return rk


def _peer_topk_kernel(xn_ref, wqt_ref, k0_ref, k1_ref, r1_ref, e1_ref, n_ref, c_ref, s0_ref, s1_ref):
    tb = xn_ref.shape[0]
    half_w = PEER_HEADS * PEER_HALF
    qt = _dot_nt(wqt_ref[...], xn_ref[...]).astype(BF16)
    s0 = _dot(k0_ref[...], qt[0:half_w])
    s1 = _dot(k1_ref[...], qt[half_w:2 * half_w])
    for g in range(tb // LANES):
        s0_ref[g] = s0[:, g * LANES:(g + 1) * LANES]
        s1_ref[g] = s1[:, g * LANES:(g + 1) * LANES]

    sub_iota = lax.broadcasted_iota(jnp.int32, (_ROWS16, LANES), 0).astype(F32)

    def slab(gi, carry):
        ls = pl.ds(pl.multiple_of(gi * LANES, LANES), LANES)
        load0 = lambda k: s0_ref[gi, pl.ds(k * SUBLANES, SUBLANES), :]
        load1 = lambda k: s1_ref[gi, pl.ds(k * SUBLANES, SUBLANES), :]
        load1_head = lambda h, kb: s1_ref[gi, pl.ds(kb * _ROWS16 * SUBLANES + h, _ROWS16, stride=SUBLANES), :]

        av = _top16_v_of_128(load0)
        bv = _top16_v_of_128(load1)
        n_row, inv_z_half = _select_pairs(av, bv)
        distinct = _top16_is_distinct(load0, av) * _top16_is_distinct(load1, bv)
        has_tie = jnp.min(distinct) < 0.5

        def emit(keys0, probe0, keys1, probe1):
            for k in range(N_KEYS):
                rs = pl.ds(k * SUBLANES, SUBLANES)
                s0k = load0(k)
                n_ref[gi, rs, :] = _match_count(keys0, probe0(k, s0k), n_row)
                c_ref[gi, rs, :] = jnp.exp(s0k - av[0]) * inv_z_half
            for h in range(PEER_HEADS):
                rows = [jnp.broadcast_to(keys1[q][h:h + 1, :], (_ROWS16, LANES)) for q in range(PEER_TOPK)]
                b0h = jnp.broadcast_to(bv[0][h:h + 1, :], (_ROWS16, LANES))
                for kb in range(N_KEYS // _ROWS16):
                    rs = pl.ds(h * N_KEYS + kb * _ROWS16, _ROWS16)
                    tile = load1_head(h, kb)
                    r1_ref[rs, ls] = _match_rank(probe1(kb, tile), rows).astype(BF16)
                    e1_ref[rs, ls] = jnp.exp(tile - b0h).astype(BF16)

        @pl.when(jnp.logical_not(has_tie))
        def _():
            emit(av, lambda k, s0k: s0k, bv, lambda kb, tile: tile)

        @pl.when(has_tie)
        def _():
            a = _top16_vi_of_128(load0)
            b = _top16_vi_of_128(load1)
            emit([it[1] for it in a], lambda k, s0k: float(k),
                 [it[1] for it in b], lambda kb, tile: sub_iota + float(kb * _ROWS16))

        return carry

    lax.fori_loop(0, tb // LANES, slab, 0)


def _peer_topk(xn2, wqt, k0, k1, tb):
    T = xn2.shape[0]
    rows = PEER_HEADS * N_KEYS
    kspec = pl.BlockSpec((rows, PEER_HEADS * PEER_HALF), lambda i: (0, 0))
    ospec = pl.BlockSpec((rows, tb), lambda i: (0, i))
    slab_spec = pl.BlockSpec((tb // LANES, rows, LANES), lambda i: (i, 0, 0))
    return pl.pallas_call(
        _peer_topk_kernel,
        grid=(T // tb,),
        in_specs=[pl.BlockSpec((tb, D_MODEL), lambda i: (i, 0)),
                  pl.BlockSpec((D_MODEL, D_MODEL), lambda i: (0, 0)),
                  kspec, kspec],
        out_specs=[ospec, ospec, slab_spec, slab_spec],
        out_shape=[jax.ShapeDtypeStruct((rows, T), BF16),
                   jax.ShapeDtypeStruct((rows, T), BF16),
                   jax.ShapeDtypeStruct((T // LANES, rows, LANES), F32),
                   jax.ShapeDtypeStruct((T // LANES, rows, LANES), F32)],
        scratch_shapes=[pltpu.VMEM((tb // LANES, rows, LANES), F32),
                        pltpu.VMEM((tb // LANES, rows, LANES), F32)],
        compiler_params=pltpu.CompilerParams(dimension_semantics=("arbitrary",),
                                             vmem_limit_bytes=VMEM_LIMIT_BYTES),
        name="peer_topk",
    )(xn2, wqt, k0, k1)


_SUB_E = 512
_W_TILE = 256


def _peer_dense_kernel(xn_ref, xn_ahead_ref, h_ref, r1_ref, e1_ref, n_ref, c_ref, u_ref, u_ahead_ref,
                       vt_ref, o_ref, acc_ref, hm_ref, at_ref):
    i = pl.program_id(0)
    c = pl.program_id(1)
    tb = xn_ref.shape[0]
    n_sub = vt_ref.shape[0]
    i0_per_sub = _SUB_E // N_KEYS
    xn = xn_ref[...]

    def gelu2(a):
        a = a.astype(BF16)
        return a * (1.0 + lax.erf(a * 0.7071067811865476))

    @pl.when(c == 0)
    def _():
        acc_ref[...] = jnp.zeros_like(acc_ref)

    @pl.when((c == 0) & (i == 0))
    def _():
        at_ref[0] = gelu2(_dot_nt(u_ref[0:_SUB_E, :], xn))

    tw = min(tb, _W_TILE)
    zero16 = jnp.zeros((_ROWS16, tw), BF16)
    n_j = N_KEYS // _ROWS16

    def gated_acts(sub, slot, out_ref):
        for ii in range(i0_per_sub):
            i0 = (c * n_sub + sub) * i0_per_sub + ii
            for lt in range(tb // tw):
                ls = slice(lt * tw, (lt + 1) * tw)
                wacc = [None] * n_j
                for h in range(PEER_HEADS):
                    rowi = pl.ds(i0 * PEER_HEADS + h, SUBLANES, stride=0)
                    slabs = range(lt * tw // LANES, (lt + 1) * tw // LANES)
                    n8 = jnp.concatenate([n_ref[g, rowi, :] for g in slabs], axis=1)
                    c8 = jnp.concatenate([c_ref[g, rowi, :] for g in slabs], axis=1)
                    nb = jnp.concatenate([n8, n8], axis=0).astype(BF16)
                    cb = jnp.concatenate([c8, c8], axis=0).astype(BF16)
                    for j in range(n_j):
                        rs = slice(h * N_KEYS + j * _ROWS16, h * N_KEYS + (j + 1) * _ROWS16)
                        term = jnp.where(r1_ref[rs, ls] < nb, e1_ref[rs, ls] * cb, zero16)
                        wacc[j] = term if wacc[j] is None else wacc[j] + term
                for j in range(n_j):
                    er = slice(ii * N_KEYS + j * _ROWS16, ii * N_KEYS + (j + 1) * _ROWS16)
                    out_ref[er, ls] = at_ref[slot, er, ls] * wacc[j]

    for sub in range(n_sub):
        par = sub % 2
        if sub + 1 < n_sub:
            at_ref[1 - par] = gelu2(_dot_nt(u_ref[(sub + 1) * _SUB_E:(sub + 2) * _SUB_E, :], xn))
        else:
            at_ref[1 - par] = gelu2(_dot_nt(u_ahead_ref[...], xn_ahead_ref[...]))
        if sub > 0:
            acc_ref[...] += _dot(vt_ref[sub - 1], hm_ref[1 - par])
        gated_acts(sub, par, hm_ref.at[par])
    acc_ref[...] += _dot(vt_ref[n_sub - 1], hm_ref[(n_sub - 1) % 2])

    @pl.when(c == pl.num_programs(1) - 1)
    def _():
        o_ref[...] = h_ref[...] + acc_ref[...].T


def _peer_dense(xn2, h, r1, e1, n_sel, c_gate, u16, vt_slabs, tb, ec):
    T = xn2.shape[0]
    rows = PEER_HEADS * N_KEYS
    n_sub = ec // _SUB_E
    assert n_sub >= 2
    tspec = pl.BlockSpec((rows, tb), lambda i, c: (0, i))
    slab_spec = pl.BlockSpec((tb // LANES, rows, LANES), lambda i, c: (i, 0, 0))
    n_tok_blocks = T // tb
    n_chunks = N_EXPERTS // ec
    n_slabs = N_EXPERTS // _SUB_E

    def ahead_tokens(i, c):
        return (jnp.where(c == n_chunks - 1, jnp.minimum(i + 1, n_tok_blocks - 1), i), 0)

    return pl.pallas_call(
        _peer_dense_kernel,
        grid=(n_tok_blocks, n_chunks),
        in_specs=[pl.BlockSpec((tb, D_MODEL), lambda i, c: (i, 0)),
                  pl.BlockSpec((tb, D_MODEL), ahead_tokens),
                  pl.BlockSpec((tb, D_MODEL), lambda i, c: (i, 0)),
                  tspec, tspec, slab_spec, slab_spec,
                  pl.BlockSpec((ec, D_MODEL), lambda i, c: (c, 0)),
                  pl.BlockSpec((_SUB_E, D_MODEL), lambda i, c: (((c + 1) * n_sub) % n_slabs, 0)),
                  pl.BlockSpec((n_sub, D_MODEL, _SUB_E), lambda i, c: (c, 0, 0))],
        out_specs=pl.BlockSpec((tb, D_MODEL), lambda i, c: (i, 0)),
        out_shape=jax.ShapeDtypeStruct((T, D_MODEL), F32),
        scratch_shapes=[pltpu.VMEM((D_MODEL, tb), F32),
                        pltpu.VMEM((2, _SUB_E, tb), BF16),
                        pltpu.VMEM((2, _SUB_E, tb), BF16)],
        compiler_params=pltpu.CompilerParams(dimension_semantics=("arbitrary", "arbitrary"),
                                             vmem_limit_bytes=VMEM_LIMIT_BYTES),
        name="peer_dense",
    )(xn2, xn2, h, r1, e1, n_sel, c_gate, u16, u16, vt_slabs)


def _block_diag_ones(width, blk):
    idx = np.arange(width) // blk
    return jnp.asarray((idx[:, None] == idx[None, :]).astype(np.float32), dtype=BF16)


def _layer(x, norm_mix_g, w_in, att_q_norm_g, att_k_norm_g, att_sinks, att_out_norm_g, dn_conv_w,
           dn_a_log, dn_dt_bias, dn_out_norm_g, w_out, norm_ffn_g, peer_w_q, peer_sub_keys, peer_u, peer_v):
    B, S, _ = x.shape
    T = B * S
    x2 = x.reshape(T, D_MODEL)

    o_q, o_k, o_v = 0, ATT_WIDTH, ATT_WIDTH + ATT_KV_WIDTH
    o_dn = ATT_WIDTH + 2 * ATT_KV_WIDTH
    o_a = o_dn + 3 * DN_WIDTH
    o_b = o_a + DN_HEADS
    o_z = o_b + DN_HEADS
    pad = jnp.zeros((D_MODEL, LANES - DN_HEADS), w_in.dtype)
    w_all = jnp.concatenate([w_in[:, o_q:o_dn], w_in[:, o_dn:o_a], w_in[:, o_z:o_z + DN_WIDTH],
                             w_in[:, o_a:o_b], pad, w_in[:, o_b:o_z], pad], axis=1).astype(BF16)
    del o_k, o_v

    def lane_pad(vec):
        return jnp.concatenate([vec.astype(F32), jnp.zeros((LANES - vec.shape[0],), F32)]).reshape(1, LANES)

    assert T % PROJ_TOKENS == 0 and S % ATT_TOKENS == 0 and S % GDN_CHUNK == 0 and B % GDN_SEQS == 0
    assert T % PEER_TOKENS == 0
    att_in, dn_in, z_in, ab_in = _inproj(x2, norm_mix_g.reshape(1, D_MODEL), w_all, tm=PROJ_TOKENS)

    att_n = _attention(
        att_in, att_sinks.astype(F32),
        jnp.tile(att_q_norm_g.astype(F32), ATT_HEADS).reshape(1, ATT_WIDTH),
        jnp.tile(att_k_norm_g.astype(F32), ATT_KV_HEADS).reshape(1, ATT_KV_WIDTH),
        att_out_norm_g.astype(F32).reshape(1, ATT_WIDTH),
        _block_diag_ones(ATT_WIDTH, ATT_HEAD_DIM), _block_diag_ones(ATT_KV_WIDTH, ATT_HEAD_DIM),
        seq=S, tq=ATT_TOKENS)

    tri = jnp.asarray(np.tril(np.ones((GDN_CHUNK, GDN_CHUNK), np.float32)))
    dn_o = _gdn(dn_in, ab_in, z_in, dn_conv_w.astype(F32),
                lane_pad(-jnp.exp(dn_a_log.astype(F32))), lane_pad(dn_dt_bias),
                dn_out_norm_g.astype(F32).reshape(1, DN_HEAD_DIM), tri, batch=B, seq=S, nb=GDN_SEQS)

    h, xn2 = _outproj(x2, att_n, dn_o, w_out.astype(BF16), norm_ffn_g.reshape(1, D_MODEL), tm=PROJ_TOKENS)

    wqt = (peer_w_q.T.reshape(PEER_HEADS, 2, PEER_HALF, D_MODEL).transpose(1, 0, 2, 3)
           .reshape(D_MODEL, D_MODEL).astype(BF16))
    eye = jnp.eye(PEER_HEADS, dtype=peer_sub_keys.dtype)
    kd = jnp.einsum('hpkc,hg->pkhgc', peer_sub_keys, eye)
    k_km = kd.reshape(2, N_KEYS * PEER_HEADS, PEER_HEADS * PEER_HALF).astype(BF16)
    r1, e1, n_sel, c_gate = _peer_topk(xn2, wqt, k_km[0], k_km[1], tb=PEER_TOKENS)

    vt_slabs = (peer_v.reshape(N_EXPERTS // _SUB_E, _SUB_E, D_MODEL).transpose(0, 2, 1).astype(BF16))
    out = _peer_dense(xn2, h, r1, e1, n_sel, c_gate, peer_u.astype(BF16), vt_slabs,
                      tb=PEER_TOKENS, ec=PEER_CHUNK)
    return out.reshape(B, S, D_MODEL)


def kernel(x, norm_mix_g, w_in, att_q_norm_g, att_k_norm_g, att_sinks, att_out_norm_g, dn_conv_w, dn_a_log, dn_dt_bias, dn_out_norm_g, w_out, norm_ffn_g, peer_w_q, peer_sub_keys, peer_u, peer_v):
    h = x
    for l in range(norm_mix_g.shape[0]):
        h = _layer(h, norm_mix_g[l], w_in[l], att_q_norm_g[l], att_k_norm_g[l], att_sinks[l],
                   att_out_norm_g[l], dn_conv_w[l], dn_a_log[l], dn_dt_bias[l], dn_out_norm_g[l],
                   w_out[l], norm_ffn_g[l], peer_w_q[l], peer_sub_keys[l], peer_u[l], peer_v[l])
    return h
```

```python
import functools

import numpy as np
import jax
import jax.numpy as jnp
from jax import lax
from jax.experimental import pallas as pl
from jax.experimental.pallas import tpu as pltpu

F32 = jnp.float32
BF16 = jnp.bfloat16

D_MODEL = 1024
ATT_HEADS = 8
ATT_KV_HEADS = 2
ATT_HEAD_DIM = 64
ATT_WIDTH = 512
ATT_KV_WIDTH = 128
ATT_BLOCK = 128
DN_HEADS = 4
DN_HEAD_DIM = 128
DN_WIDTH = 512
CONV_WIDTH = 4
PEER_HEADS = 8
N_KEYS = 128
N_EXPERTS = N_KEYS * N_KEYS
PEER_HALF = 64
PEER_TOPK = 16
EPS = 1e-6

LANES = 128
SUBLANES = 8
V7X_VMEM_BYTES = 64 * 1024 * 1024
VMEM_LIMIT_BYTES = V7X_VMEM_BYTES - 8 * 1024 * 1024

PROJ_TOKENS = 512
ATT_TOKENS = 1024
GDN_SEQS = 4
PEER_TOKENS = 512
PEER_CHUNK = 2048

ATT_IN_WIDTH = ATT_WIDTH + 2 * ATT_KV_WIDTH

_C_ATT = 0
_C_DN = ATT_IN_WIDTH
_C_Z = _C_DN + 3 * DN_WIDTH
_C_AB = _C_Z + DN_WIDTH
_C_END = _C_AB + 2 * LANES

GDN_CHUNK = 128
_ROWS16 = 2 * SUBLANES


def _dot(a, b):
    return jnp.dot(a, b, preferred_element_type=F32)


def _dot_nt(a, b):
    return lax.dot_general(a, b, (((1,), (1,)), ((), ())), preferred_element_type=F32)


def _split2(a):
    hi = a.astype(BF16)
    lo = (a - hi.astype(F32)).astype(BF16)
    return hi, lo


def _dot3s(a_split, b_split):
    ah, al = a_split
    bh, bl = b_split
    return _dot(ah, bh) + (_dot(ah, bl) + _dot(al, bh))


def _dot3(a, b):
    return _dot3s(_split2(a), _split2(b))


def _inproj_kernel(x_ref, g_ref, w_ref, att_ref, dn_ref, z_ref, ab_ref):
    x = x_ref[...]
    ms = jnp.mean(x * x, axis=-1, keepdims=True)
    xn = (x * lax.rsqrt(ms + EPS) * g_ref[...]).astype(BF16)
    att_ref[...] = _dot(xn, w_ref[:, _C_ATT:_C_DN])
    dn_ref[...] = _dot(xn, w_ref[:, _C_DN:_C_Z])
    z_ref[...] = _dot(xn, w_ref[:, _C_Z:_C_AB])
    ab_ref[...] = _dot(xn, w_ref[:, _C_AB:_C_END])


def _inproj(x2, g, w_all, tm):
    T = x2.shape[0]
    return pl.pallas_call(
        _inproj_kernel,
        grid=(T // tm,),
        in_specs=[pl.BlockSpec((tm, D_MODEL), lambda i: (i, 0)),
                  pl.BlockSpec((1, D_MODEL), lambda i: (0, 0)),
                  pl.BlockSpec((D_MODEL, _C_END), lambda i: (0, 0))],
        out_specs=[pl.BlockSpec((tm, ATT_IN_WIDTH), lambda i: (i, 0)),
                   pl.BlockSpec((tm, 3 * DN_WIDTH), lambda i: (i, 0)),
                   pl.BlockSpec((tm, DN_WIDTH), lambda i: (i, 0)),
                   pl.BlockSpec((tm, 2 * LANES), lambda i: (i, 0))],
        out_shape=[jax.ShapeDtypeStruct((T, ATT_IN_WIDTH), F32),
                   jax.ShapeDtypeStruct((T, 3 * DN_WIDTH), F32),
                   jax.ShapeDtypeStruct((T, DN_WIDTH), F32),
                   jax.ShapeDtypeStruct((T, 2 * LANES), F32)],
        compiler_params=pltpu.CompilerParams(dimension_semantics=("arbitrary",),
                                             vmem_limit_bytes=VMEM_LIMIT_BYTES),
        name="inproj",
    )(x2, g, w_all)


def _head_rms(t, bd, g):
    hi, lo = _split2(t * t)
    ss = _dot(hi, bd) + _dot(lo, bd)
    return t * lax.rsqrt(ss * (1.0 / ATT_HEAD_DIM) + EPS) * g


def _attn_kernel(sink_ref, cur_ref, prev_ref, gq_ref, gk_ref, go_ref, bdq_ref, bdk_ref, o_ref,
                 *, tiles_per_seq):
    i = pl.program_id(0)
    tq = cur_ref.shape[0]
    nblk = tq // ATT_BLOCK
    first = (i % tiles_per_seq) == 0

    q = cur_ref[:, 0:ATT_WIDTH]
    k = cur_ref[:, ATT_WIDTH:ATT_WIDTH + ATT_KV_WIDTH]
    v = cur_ref[:, ATT_WIDTH + ATT_KV_WIDTH:ATT_WIDTH + 2 * ATT_KV_WIDTH]
    kp = prev_ref[:, 0:ATT_KV_WIDTH]
    vp = prev_ref[:, ATT_KV_WIDTH:2 * ATT_KV_WIDTH]

    qn = (_head_rms(q, bdq_ref[...], gq_ref[...]) * (ATT_HEAD_DIM ** -0.5)).astype(BF16)
    kn = jnp.concatenate([_head_rms(kp, bdk_ref[...], gk_ref[...]),
                          _head_rms(k, bdk_ref[...], gk_ref[...])], axis=0)
    vf = jnp.concatenate([vp, v], axis=0)

    lane = lax.broadcasted_iota(jnp.int32, kn.shape, 1)
    left = lane < ATT_HEAD_DIM
    zero = jnp.zeros_like(kn)
    k0l = jnp.where(left, kn, zero)
    k1r = jnp.where(left, zero, kn)
    v0l = jnp.where(left, vf, zero)
    v1r = jnp.where(left, zero, vf)
    kl = (k0l.astype(BF16), pltpu.roll(k1r, ATT_HEAD_DIM, 1).astype(BF16))
    kr = (pltpu.roll(k0l, ATT_HEAD_DIM, 1).astype(BF16), k1r.astype(BF16))
    vl = (v0l.astype(BF16), pltpu.roll(v1r, ATT_HEAD_DIM, 1).astype(BF16))
    vr = (pltpu.roll(v0l, ATT_HEAD_DIM, 1).astype(BF16), v1r.astype(BF16))

    qi = lax.broadcasted_iota(jnp.int32, (ATT_BLOCK, 2 * ATT_BLOCK), 0)
    kj = lax.broadcasted_iota(jnp.int32, (ATT_BLOCK, 2 * ATT_BLOCK), 1)
    rel = qi + ATT_BLOCK - kj
    in_window = (rel >= 0) & (rel < ATT_BLOCK)
    first_key = jnp.where(first, ATT_BLOCK, 0)
    neg_inf = jnp.full((ATT_BLOCK, 2 * ATT_BLOCK), -jnp.inf, F32)

    def softmax_rows(s, sink):
        m = jnp.maximum(jnp.max(s, axis=-1, keepdims=True), sink)
        p = jnp.exp(s - m)
        den = jnp.sum(p, axis=-1, keepdims=True) + jnp.exp(sink - m)
        return (p * (1.0 / den)).astype(BF16)

    for j in range(nblk):
        rows = slice(j * ATT_BLOCK, (j + 1) * ATT_BLOCK)
        krows = slice(j * ATT_BLOCK, (j + 2) * ATT_BLOCK)
        if j == 0:
            mask = in_window & (kj >= first_key)
        else:
            mask = in_window
        pairs = []
        for c in range(ATT_KV_HEADS):
            qe = jnp.concatenate([qn[rows, (2 * c) * LANES:(2 * c + 1) * LANES],
                                  qn[rows, (2 * c + 1) * LANES:(2 * c + 2) * LANES]], axis=0)
            s_even = _dot_nt(qe, kl[c][krows])
            s_odd = _dot_nt(qe, kr[c][krows])
            for half in range(2):
                hr = slice(half * ATT_BLOCK, (half + 1) * ATT_BLOCK)
                h_even = 4 * c + 2 * half
                p_e = softmax_rows(jnp.where(mask, s_even[hr], neg_inf), sink_ref[h_even])
                p_o = softmax_rows(jnp.where(mask, s_odd[hr], neg_inf), sink_ref[h_even + 1])
                pairs.append(_dot(p_e, vl[c][krows]) + _dot(p_o, vr[c][krows]))
        att = jnp.concatenate(pairs, axis=1)
        ms = jnp.mean(att * att, axis=-1, keepdims=True)
        o_ref[rows, :] = (att * lax.rsqrt(ms + EPS) * go_ref[...]).astype(BF16)


def _attention(att_in, sinks, gq, gk, go, bdq, bdk, seq, tq):
    T = att_in.shape[0]
    tiles_per_seq = seq // tq
    blk_per_tile = tq // ATT_BLOCK
    kv_col_blk = ATT_WIDTH // (2 * ATT_KV_WIDTH)
    return pl.pallas_call(
        functools.partial(_attn_kernel, tiles_per_seq=tiles_per_seq),
        grid_spec=pltpu.PrefetchScalarGridSpec(
            num_scalar_prefetch=1,
            grid=(T // tq,),
            in_specs=[pl.BlockSpec((tq, ATT_IN_WIDTH), lambda i, s: (i, 0)),
                      pl.BlockSpec((ATT_BLOCK, 2 * ATT_KV_WIDTH),
                                   lambda i, s: (jnp.maximum(i * blk_per_tile - 1, 0), kv_col_blk)),
                      pl.BlockSpec((1, ATT_WIDTH), lambda i, s: (0, 0)),
                      pl.BlockSpec((1, ATT_KV_WIDTH), lambda i, s: (0, 0)),
                      pl.BlockSpec((1, ATT_WIDTH), lambda i, s: (0, 0)),
                      pl.BlockSpec((ATT_WIDTH, ATT_WIDTH), lambda i, s: (0, 0)),
                      pl.BlockSpec((ATT_KV_WIDTH, ATT_KV_WIDTH), lambda i, s: (0, 0))],
            out_specs=pl.BlockSpec((tq, ATT_WIDTH), lambda i, s: (i, 0)),
        ),
        out_shape=jax.ShapeDtypeStruct((T, ATT_WIDTH), BF16),
        compiler_params=pltpu.CompilerParams(dimension_semantics=("arbitrary",),
                                             vmem_limit_bytes=VMEM_LIMIT_BYTES),
        name="swa_attention",
    )(sinks, att_in, att_in, gq, gk, go, bdq, bdk)


def _unit_lower_inverses(m_lows, row, col):
    n = m_lows[0].shape[0]
    zero = jnp.zeros_like(m_lows[0])
    zero16 = jnp.zeros(zero.shape, BF16)
    eye = jnp.where(row == col, jnp.ones_like(zero), zero)
    same = (row >> 3) == (col >> 3)
    m16 = [m.astype(BF16) for m in m_lows]
    nm16 = [jnp.where(same, -mh, zero16) for mh in m16]
    ps = [eye + jnp.where(same, -m, zero) for m in m_lows]
    n2_16 = [_dot(nm, nm).astype(BF16) for nm in nm16]
    ps = [p + _dot(p.astype(BF16), n2) for p, n2 in zip(ps, n2_16)]
    n4_16 = [_dot(n2, n2).astype(BF16) for n2 in n2_16]
    ps = [p + _dot(p.astype(BF16), n4) for p, n4 in zip(ps, n4_16)]
    shift = 3
    while (1 << shift) < n:
        same_next = (row >> (shift + 1)) == (col >> (shift + 1))
        lower_left = same_next & jnp.logical_not(same)
        p16 = [p.astype(BF16) for p in ps]
        ts = [_dot(ph, jnp.where(lower_left, mh, zero16)) for ph, mh in zip(p16, m16)]
        ps = [p - _dot(t.astype(BF16), ph) for p, t, ph in zip(ps, ts, p16)]
        same = same_next
        shift += 1
    return ps


def _silu(v):
    return v * (1.0 / (1.0 + jnp.exp(-v)))


def _gdn_kernel(dn_ref, ab_ref, z_ref, cw_ref, nega_ref, dtb_ref, gn_ref, tri_ref, o_ref,
                state_ref, carry_ref):
    t = pl.program_id(1)
    C = GDN_CHUNK
    nb = dn_ref.shape[0]

    @pl.when(t == 0)
    def _():
        state_ref[...] = jnp.zeros_like(state_ref)
        carry_ref[...] = jnp.zeros_like(carry_ref)

    row = lax.broadcasted_iota(jnp.int32, (C, C), 0)
    col = lax.broadcasted_iota(jnp.int32, (C, C), 1)
    causal = row >= col
    strict = row > col
    zero_cc = jnp.zeros((C, C), F32)
    cw = cw_ref[...]
    row8 = lax.broadcasted_iota(jnp.int32, (SUBLANES, 3 * DN_WIDTH), 0)

    ids, qs, ks, vbs, kbes, decs, egs, kdecs, cds = [], [], [], [], [], [], [], [], []
    for bb in range(nb):
        x = dn_ref[bb]
        tail = carry_ref[bb]
        carry_ref[bb] = x[C - SUBLANES:C, :]
        y = x * cw[CONV_WIDTH - 1:CONV_WIDTH, :]
        for back in range(1, CONV_WIDTH):
            xr = pltpu.roll(x, back, 0)
            top = jnp.where(row8 < back, pltpu.roll(tail, back, 0), xr[0:SUBLANES])
            shifted = jnp.concatenate([top, xr[SUBLANES:]], axis=0)
            y = y + shifted * cw[CONV_WIDTH - 1 - back:CONV_WIDTH - back, :]
        y = _silu(y)

        ab = ab_ref[bb]
        sp_in = ab[:, 0:LANES] + dtb_ref[...]
        softplus = jnp.maximum(sp_in, 0.0) + jnp.log1p(jnp.exp(-jnp.abs(sp_in)))
        g_all = nega_ref[...] * softplus
        beta_all = 1.0 / (1.0 + jnp.exp(-ab[:, LANES:2 * LANES]))
        gc_all = _dot3(tri_ref[...], g_all)
        gc_t = gc_all.T

        for h in range(DN_HEADS):
            qh = y[:, h * DN_HEAD_DIM:(h + 1) * DN_HEAD_DIM]
            kh = y[:, DN_WIDTH + h * DN_HEAD_DIM:DN_WIDTH + (h + 1) * DN_HEAD_DIM]
            vh = y[:, 2 * DN_WIDTH + h * DN_HEAD_DIM:2 * DN_WIDTH + (h + 1) * DN_HEAD_DIM]
            qh = qh * lax.rsqrt(jnp.sum(qh * qh, axis=-1, keepdims=True) + EPS) * (DN_HEAD_DIM ** -0.5)
            kh = kh * lax.rsqrt(jnp.sum(kh * kh, axis=-1, keepdims=True) + EPS)
            gcol = gc_all[:, h:h + 1]
            grow = gc_t[h:h + 1, :]
            bcol = beta_all[:, h:h + 1]
            glast = gc_all[C - 1:C, h:h + 1]
            eg = jnp.exp(gcol)
            kb = kh * bcol
            ids.append((bb, h))
            qs.append(qh)
            ks.append(kh.astype(BF16))
            vbs.append(jnp.concatenate([vh * bcol, kb * eg], axis=1).astype(BF16))
            kbes.append(kb.astype(BF16))
            decs.append(jnp.where(causal, jnp.exp(jnp.where(causal, gcol - grow, zero_cc)), zero_cc))
            egs.append(eg)
            kdecs.append((kh * jnp.exp(glast - gcol)).T.astype(BF16))
            cds.append(jnp.exp(glast))

    n = len(ids)
    m_lows = [jnp.where(strict, _dot_nt(kbes[i], ks[i]) * decs[i], zero_cc) for i in range(n)]
    qks = [jnp.where(causal, _dot_nt(qs[i].astype(BF16), ks[i]) * decs[i], zero_cc).astype(BF16)
           for i in range(n)]
    tinvs = _unit_lower_inverses(m_lows, row, col)
    uws = [_dot(tinvs[i].astype(BF16), vbs[i]) for i in range(n)]
    s_olds = [state_ref[bb, h] for (bb, h) in ids]
    s16s = [s.astype(BF16) for s in s_olds]
    v16s = [(uws[i][:, 0:DN_HEAD_DIM] - _dot(uws[i][:, DN_HEAD_DIM:].astype(BF16), s16s[i])).astype(BF16)
            for i in range(n)]
    outs = [_dot((qs[i] * egs[i]).astype(BF16), s16s[i]) + _dot(qks[i], v16s[i]) for i in range(n)]
    for i, (bb, h) in enumerate(ids):
        state_ref[bb, h] = s_olds[i] * cds[i] + _dot(kdecs[i], v16s[i])

    for i, (bb, h) in enumerate(ids):
        hs = slice(h * DN_HEAD_DIM, (h + 1) * DN_HEAD_DIM)
        o = outs[i]
        ms = jnp.mean(o * o, axis=-1, keepdims=True)
        o_ref[bb, :, hs] = (o * lax.rsqrt(ms + EPS) * gn_ref[...] * _silu(z_ref[bb, :, hs])).astype(BF16)


def _gdn(dn_in, ab, z, conv_w, neg_a, dt_bias, gnorm, tri, batch, seq, nb):
    C = GDN_CHUNK
    nc = seq // C
    blk3 = lambda w: pl.BlockSpec((nb, C, w), lambda bi, t: (bi, t, 0))
    const = lambda shape: pl.BlockSpec(shape, lambda bi, t: (0, 0))
    out = pl.pallas_call(
        _gdn_kernel,
        grid=(batch // nb, nc),
        in_specs=[blk3(3 * DN_WIDTH), blk3(2 * LANES), blk3(DN_WIDTH),
                  const((CONV_WIDTH, 3 * DN_WIDTH)), const((1, LANES)), const((1, LANES)),
                  const((1, DN_HEAD_DIM)), const((C, C))],
        out_specs=blk3(DN_WIDTH),
        out_shape=jax.ShapeDtypeStruct((batch, seq, DN_WIDTH), BF16),
        scratch_shapes=[pltpu.VMEM((nb, DN_HEADS, DN_HEAD_DIM, DN_HEAD_DIM), F32),
                        pltpu.VMEM((nb, SUBLANES, 3 * DN_WIDTH), F32)],
        compiler_params=pltpu.CompilerParams(dimension_semantics=("arbitrary", "arbitrary"),
                                             vmem_limit_bytes=VMEM_LIMIT_BYTES),
        name="gated_deltanet",
    )(dn_in.reshape(batch, seq, -1), ab.reshape(batch, seq, -1), z.reshape(batch, seq, -1),
      conv_w, neg_a, dt_bias, gnorm, tri)
    return out.reshape(batch * seq, DN_WIDTH)


def _outproj_kernel(x_ref, att_ref, dn_ref, wo_ref, g_ref, h_ref, xn_ref):
    h = x_ref[...] + (_dot(att_ref[...], wo_ref[0:ATT_WIDTH, :])
                      + _dot(dn_ref[...], wo_ref[ATT_WIDTH:ATT_WIDTH + DN_WIDTH, :]))
    h_ref[...] = h
    ms = jnp.mean(h * h, axis=-1, keepdims=True)
    xn_ref[...] = (h * lax.rsqrt(ms + EPS) * g_ref[...]).astype(BF16)


def _oddeven_merge(lo, hi, r):
    step = r * 2
    if step < hi - lo:
        yield from _oddeven_merge(lo, hi, step)
        yield from _oddeven_merge(lo + r, hi, step)
        for i in range(lo + r, hi - r, step):
            yield (i, i + r)
    else:
        yield (lo, lo + r)


def _oddeven_sort_net(lo, hi):
    if hi - lo >= 1:
        mid = lo + (hi - lo) // 2
        yield from _oddeven_sort_net(lo, mid)
        yield from _oddeven_sort_net(mid + 1, hi)
        yield from _oddeven_merge(lo, hi, 1)


_SORT16 = tuple(_oddeven_sort_net(0, PEER_TOPK - 1))
_BITONIC16 = tuple((i, i + d) for d in (8, 4, 2, 1) for i in range(PEER_TOPK) if (i & d) == 0)
_STAIR = tuple((r, q) for r in range(PEER_TOPK) for q in range(PEER_TOPK) if (r + 1) * (q + 1) <= PEER_TOPK)


def _ce_vi(a, b):
    va, ia = a
    vb, ib = b
    a_first = (va > vb) | ((va == vb) & (ia < ib))
    return ((jnp.where(a_first, va, vb), jnp.where(a_first, ia, ib)),
            (jnp.where(a_first, vb, va), jnp.where(a_first, ib, ia)))


def _first_vi(a, b):
    va, ia = a
    vb, ib = b
    a_first = (va > vb) | ((va == vb) & (ia < ib))
    return (jnp.where(a_first, va, vb), jnp.where(a_first, ia, ib))


def _sort16_vi(items):
    items = list(items)
    for i, j in _SORT16:
        items[i], items[j] = _ce_vi(items[i], items[j])
    return items


def _merge_top16_vi(a, b):
    n = PEER_TOPK
    items = [_first_vi(a[i], b[n - 1 - i]) for i in range(n)]
    for i, j in _BITONIC16:
        items[i], items[j] = _ce_vi(items[i], items[j])
    return items


def _sort16_v(vals):
    vals = list(vals)
    for i, j in _SORT16:
        hi = jnp.maximum(vals[i], vals[j])
        lo = jnp.minimum(vals[i], vals[j])
        vals[i], vals[j] = hi, lo
    return vals


def _merge_top16_v(a, b):
    n = PEER_TOPK
    vals = [jnp.maximum(a[i], b[n - 1 - i]) for i in range(n)]
    for i, j in _BITONIC16:
        hi = jnp.maximum(vals[i], vals[j])
        lo = jnp.minimum(vals[i], vals[j])
        vals[i], vals[j] = hi, lo
    return vals


def _top16_vi_of_128(load_key):
    groups = []
    for gi in range(N_KEYS // PEER_TOPK):
        items = [(load_key(gi * PEER_TOPK + t), jnp.full((SUBLANES, LANES), float(gi * PEER_TOPK + t), F32))
                 for t in range(PEER_TOPK)]
        groups.append(_sort16_vi(items))
    while len(groups) > 1:
        groups = [_merge_top16_vi(groups[2 * t], groups[2 * t + 1]) for t in range(len(groups) // 2)]
    return groups[0]


def _top16_v_of_128(load_key):
    groups = [_sort16_v([load_key(gi * PEER_TOPK + t) for t in range(PEER_TOPK)])
              for gi in range(N_KEYS // PEER_TOPK)]
    while len(groups) > 1:
        groups = [_merge_top16_v(groups[2 * t], groups[2 * t + 1]) for t in range(len(groups) // 2)]
    return groups[0]


def _top16_is_distinct(load_key, vals):
    zero = jnp.zeros((SUBLANES, LANES), F32)
    one = jnp.ones((SUBLANES, LANES), F32)
    n_ge = zero
    for k in range(N_KEYS):
        n_ge = n_ge + jnp.where(load_key(k) >= vals[PEER_TOPK - 1], one, zero)
    ok = n_ge == float(PEER_TOPK)
    for r in range(PEER_TOPK - 1):
        ok = ok & (vals[r] > vals[r + 1])
    return jnp.where(ok, one, zero)


def _select_pairs(av, bv):
    zero = jnp.zeros((SUBLANES, LANES), F32)
    one = jnp.ones((SUBLANES, LANES), F32)
    cand = {(r, q): av[r] + bv[q] for (r, q) in _STAIR}
    row0 = [cand[(0, q)] for q in range(PEER_TOPK)]
    rest = [cand[rq] for rq in _STAIR if rq[0] > 0]
    neg_inf = jnp.full((SUBLANES, LANES), -jnp.inf, F32)
    best = row0
    for s in range(0, len(rest), PEER_TOPK):
        grp = rest[s:s + PEER_TOPK]
        grp = grp + [neg_inf] * (PEER_TOPK - len(grp))
        best = _merge_top16_v(best, _sort16_v(grp))
    thr = best[PEER_TOPK - 1]
    zsum = one
    for jj in range(1, PEER_TOPK):
        zsum = zsum + jnp.exp(best[jj] - best[0])
    inv_z_half = 0.5 / zsum

    n_gt = zero
    for rq in _STAIR:
        n_gt = n_gt + jnp.where(cand[rq] > thr, one, zero)
    need = float(PEER_TOPK) - n_gt
    cnt = zero
    n_row = [zero] * PEER_TOPK
    for (r, q) in _STAIR:
        c = cand[(r, q)]
        eq = c == thr
        take = (c > thr) | (eq & (cnt < need))
        cnt = cnt + jnp.where(eq, one, zero)
        n_row[r] = n_row[r] + jnp.where(take, one, zero)
    return n_row, inv_z_half


def _match_count(keys, probe, n_row):
    n_k = jnp.zeros((SUBLANES, LANES), F32)
    for r in reversed(range(PEER_TOPK)):
        n_k = jnp.where(keys[r] == probe, n_row[r], n_k)
    return n_k


def _match_rank(probe, rows):
    rk = jnp.full(probe.shape, float(PEER_TOPK), F32)
    for q in reversed(range(PEER_TOPK)):
        rk = jnp.where(rows[q] == probe, jnp.full_like(rk, float(q)), rk)
    return rk


def _peer_topk_kernel(xn_ref, wqt_ref, k0_ref, k1_ref, r1_ref, e1_ref, n_ref, c_ref, s0_ref, s1_ref):
    tb = xn_ref.shape[0]
    half_w = PEER_HEADS * PEER_HALF
    qt = _dot_nt(wqt_ref[...], xn_ref[...]).astype(BF16)
    s0 = _dot(k0_ref[...], qt[0:half_w])
    s1 = _dot(k1_ref[...], qt[half_w:2 * half_w])
    for g in range(tb // LANES):
        s0_ref[g] = s0[:, g * LANES:(g + 1) * LANES]
        s1_ref[g] = s1[:, g * LANES:(g + 1) * LANES]

    sub_iota = lax.broadcasted_iota(jnp.int32, (_ROWS16, LANES), 0).astype(F32)

    def slab(gi, carry):
        ls = pl.ds(pl.multiple_of(gi * LANES, LANES), LANES)
        load0 = lambda k: s0_ref[gi, pl.ds(k * SUBLANES, SUBLANES), :]
        load1 = lambda k: s1_ref[gi, pl.ds(k * SUBLANES, SUBLANES), :]
        load1_head = lambda h, kb: s1_ref[gi, pl.ds(kb * _ROWS16 * SUBLANES + h, _ROWS16, stride=SUBLANES), :]

        av = _top16_v_of_128(load0)
        bv = _top16_v_of_128(load1)
        n_row, inv_z_half = _select_pairs(av, bv)
        distinct = _top16_is_distinct(load0, av) * _top16_is_distinct(load1, bv)
        has_tie = jnp.min(distinct) < 0.5

        def emit(keys0, probe0, keys1, probe1):
            for k in range(N_KEYS):
                rs = pl.ds(k * SUBLANES, SUBLANES)
                s0k = load0(k)
                n_ref[gi, rs, :] = _match_count(keys0, probe0(k, s0k), n_row)
                c_ref[gi, rs, :] = jnp.exp(s0k - av[0]) * inv_z_half
            for h in range(PEER_HEADS):
                rows = [jnp.broadcast_to(keys1[q][h:h + 1, :], (_ROWS16, LANES)) for q in range(PEER_TOPK)]
                b0h = jnp.broadcast_to(bv[0][h:h + 1, :], (_ROWS16, LANES))
                for kb in range(N_KEYS // _ROWS16):
                    rs = pl.ds(h * N_KEYS + kb * _ROWS16, _ROWS16)
                    tile = load1_head(h, kb)
                    r1_ref[rs, ls] = _match_rank(probe1(kb, tile), rows).astype(BF16)
                    e1_ref[rs, ls] = jnp.exp(tile - b0h).astype(BF16)

        @pl.when(jnp.logical_not(has_tie))
        def _():
            emit(av, lambda k, s0k: s0k, bv, lambda kb, tile: tile)

        @pl.when(has_tie)
        def _():
            a = _top16_vi_of_128(load0)
            b = _top16_vi_of_128(load1)
            emit([it[1] for it in a], lambda k, s0k: float(k),
                 [it[1] for it in b], lambda kb, tile: sub_iota + float(kb * _ROWS16))

        return carry

    lax.fori_loop(0, tb // LANES, slab, 0)


def _outproj_topk_kernel(x_ref, att_ref, dn_ref, wo_ref, g_ref, wqt_ref, k0_ref, k1_ref,
                         h_ref, xn_ref, r1_ref, e1_ref, n_ref, c_ref, s0_ref, s1_ref):
    _outproj_kernel(x_ref, att_ref, dn_ref, wo_ref, g_ref, h_ref, xn_ref)
    _peer_topk_kernel(xn_ref, wqt_ref, k0_ref, k1_ref, r1_ref, e1_ref, n_ref, c_ref, s0_ref, s1_ref)


def _outproj_topk(x2, att_n, dn_o, w_out, g, wqt, k0, k1, tb):
    T = x2.shape[0]
    rows = PEER_HEADS * N_KEYS
    tok_spec = lambda w: pl.BlockSpec((tb, w), lambda i: (i, 0))
    kspec = pl.BlockSpec((rows, PEER_HEADS * PEER_HALF), lambda i: (0, 0))
    ospec = pl.BlockSpec((rows, tb), lambda i: (0, i))
    slab_spec = pl.BlockSpec((tb // LANES, rows, LANES), lambda i: (i, 0, 0))
    return pl.pallas_call(
        _outproj_topk_kernel,
        grid=(T // tb,),
        in_specs=[tok_spec(D_MODEL), tok_spec(ATT_WIDTH), tok_spec(DN_WIDTH),
                  pl.BlockSpec((ATT_WIDTH + DN_WIDTH, D_MODEL), lambda i: (0, 0)),
                  pl.BlockSpec((1, D_MODEL), lambda i: (0, 0)),
                  pl.BlockSpec((D_MODEL, D_MODEL), lambda i: (0, 0)),
                  kspec, kspec],
        out_specs=[tok_spec(D_MODEL), tok_spec(D_MODEL), ospec, ospec, slab_spec, slab_spec],
        out_shape=[jax.ShapeDtypeStruct((T, D_MODEL), F32),
                   jax.ShapeDtypeStruct((T, D_MODEL), BF16),
                   jax.ShapeDtypeStruct((rows, T), BF16),
                   jax.ShapeDtypeStruct((rows, T), BF16),
                   jax.ShapeDtypeStruct((T // LANES, rows, LANES), F32),
                   jax.ShapeDtypeStruct((T // LANES, rows, LANES), F32)],
        scratch_shapes=[pltpu.VMEM((tb // LANES, rows, LANES), F32),
                        pltpu.VMEM((tb // LANES, rows, LANES), F32)],
        compiler_params=pltpu.CompilerParams(dimension_semantics=("arbitrary",),
                                             vmem_limit_bytes=VMEM_LIMIT_BYTES),
        name="outproj_peer_topk",
    )(x2, att_n, dn_o, w_out, g, wqt, k0, k1)


_SUB_E = 512
_W_TILE = 256


def _peer_dense_kernel(xn_ref, xn_ahead_ref, h_ref, r1_ref, e1_ref, n_ref, c_ref, u_ref, u_ahead_ref,
                       vt_ref, o_ref, acc_ref, hm_ref, at_ref):
    i = pl.program_id(0)
    c = pl.program_id(1)
    tb = xn_ref.shape[0]
    n_sub = vt_ref.shape[0]
    i0_per_sub = _SUB_E // N_KEYS
    xn = xn_ref[...]

    def gelu2(a):
        a = a.astype(BF16)
        return a * (1.0 + lax.erf(a * 0.7071067811865476))

    @pl.when(c == 0)
    def _():
        acc_ref[...] = jnp.zeros_like(acc_ref)

    @pl.when((c == 0) & (i == 0))
    def _():
        at_ref[0] = gelu2(_dot_nt(u_ref[0:_SUB_E, :], xn))

    tw = min(tb, _W_TILE)
    zero16 = jnp.zeros((_ROWS16, tw), BF16)
    n_j = N_KEYS // _ROWS16

    def gated_acts(sub, slot, out_ref):
        for ii in range(i0_per_sub):
            i0 = (c * n_sub + sub) * i0_per_sub + ii
            for lt in range(tb // tw):
                ls = slice(lt * tw, (lt + 1) * tw)
                wacc = [None] * n_j
                for h in range(PEER_HEADS):
                    rowi = pl.ds(i0 * PEER_HEADS + h, SUBLANES, stride=0)
                    slabs = range(lt * tw // LANES, (lt + 1) * tw // LANES)
                    n8 = jnp.concatenate([n_ref[g, rowi, :] for g in slabs], axis=1)
                    c8 = jnp.concatenate([c_ref[g, rowi, :] for g in slabs], axis=1)
                    nb = jnp.concatenate([n8, n8], axis=0).astype(BF16)
                    cb = jnp.concatenate([c8, c8], axis=0).astype(BF16)
                    for j in range(n_j):
                        rs = slice(h * N_KEYS + j * _ROWS16, h * N_KEYS + (j + 1) * _ROWS16)
                        term = jnp.where(r1_ref[rs, ls] < nb, e1_ref[rs, ls] * cb, zero16)
                        wacc[j] = term if wacc[j] is None else wacc[j] + term
                for j in range(n_j):
                    er = slice(ii * N_KEYS + j * _ROWS16, ii * N_KEYS + (j + 1) * _ROWS16)
                    out_ref[er, ls] = at_ref[slot, er, ls] * wacc[j]

    for sub in range(n_sub):
        par = sub % 2
        if sub + 1 < n_sub:
            at_ref[1 - par] = gelu2(_dot_nt(u_ref[(sub + 1) * _SUB_E:(sub + 2) * _SUB_E, :], xn))
        else:
            at_ref[1 - par] = gelu2(_dot_nt(u_ahead_ref[...], xn_ahead_ref[...]))
        if sub > 0:
            acc_ref[...] += _dot(vt_ref[sub - 1], hm_ref[1 - par])
        gated_acts(sub, par, hm_ref.at[par])
    acc_ref[...] += _dot(vt_ref[n_sub - 1], hm_ref[(n_sub - 1) % 2])

    @pl.when(c == pl.num_programs(1) - 1)
    def _():
        o_ref[...] = h_ref[...] + acc_ref[...].T


def _peer_dense(xn2, h, r1, e1, n_sel, c_gate, u16, vt_slabs, tb, ec):
    T = xn2.shape[0]
    rows = PEER_HEADS * N_KEYS
    n_sub = ec // _SUB_E
    assert n_sub >= 2
    tspec = pl.BlockSpec((rows, tb), lambda i, c: (0, i))
    slab_spec = pl.BlockSpec((tb // LANES, rows, LANES), lambda i, c: (i, 0, 0))
    n_tok_blocks = T // tb
    n_chunks = N_EXPERTS // ec
    n_slabs = N_EXPERTS // _SUB_E

    def ahead_tokens(i, c):
        return (jnp.where(c == n_chunks - 1, jnp.minimum(i + 1, n_tok_blocks - 1), i), 0)

    return pl.pallas_call(
        _peer_dense_kernel,
        grid=(n_tok_blocks, n_chunks),
        in_specs=[pl.BlockSpec((tb, D_MODEL), lambda i, c: (i, 0)),
                  pl.BlockSpec((tb, D_MODEL), ahead_tokens),
                  pl.BlockSpec((tb, D_MODEL), lambda i, c: (i, 0)),
                  tspec, tspec, slab_spec, slab_spec,
                  pl.BlockSpec((ec, D_MODEL), lambda i, c: (c, 0)),
                  pl.BlockSpec((_SUB_E, D_MODEL), lambda i, c: (((c + 1) * n_sub) % n_slabs, 0)),
                  pl.BlockSpec((n_sub, D_MODEL, _SUB_E), lambda i, c: (c, 0, 0))],
        out_specs=pl.BlockSpec((tb, D_MODEL), lambda i, c: (i, 0)),
        out_shape=jax.ShapeDtypeStruct((T, D_MODEL), F32),
        scratch_shapes=[pltpu.VMEM((D_MODEL, tb), F32),
                        pltpu.VMEM((2, _SUB_E, tb), BF16),
                        pltpu.VMEM((2, _SUB_E, tb), BF16)],
        compiler_params=pltpu.CompilerParams(dimension_semantics=("arbitrary", "arbitrary"),
                                             vmem_limit_bytes=VMEM_LIMIT_BYTES),
        name="peer_dense",
    )(xn2, xn2, h, r1, e1, n_sel, c_gate, u16, u16, vt_slabs)


def _block_diag_ones(width, blk):
    idx = np.arange(width) // blk
    return jnp.asarray((idx[:, None] == idx[None, :]).astype(np.float32), dtype=BF16)


def _layer(x, norm_mix_g, w_in, att_q_norm_g, att_k_norm_g, att_sinks, att_out_norm_g, dn_conv_w,
           dn_a_log, dn_dt_bias, dn_out_norm_g, w_out, norm_ffn_g, peer_w_q, peer_sub_keys, peer_u, peer_v):
    B, S, _ = x.shape
    T = B * S
    x2 = x.reshape(T, D_MODEL)

    o_q, o_k, o_v = 0, ATT_WIDTH, ATT_WIDTH + ATT_KV_WIDTH
    o_dn = ATT_WIDTH + 2 * ATT_KV_WIDTH
    o_a = o_dn + 3 * DN_WIDTH
    o_b = o_a + DN_HEADS
    o_z = o_b + DN_HEADS
    pad = jnp.zeros((D_MODEL, LANES - DN_HEADS), w_in.dtype)
    w_all = jnp.concatenate([w_in[:, o_q:o_dn], w_in[:, o_dn:o_a], w_in[:, o_z:o_z + DN_WIDTH],
                             w_in[:, o_a:o_b], pad, w_in[:, o_b:o_z], pad], axis=1).astype(BF16)
    del o_k, o_v

    def lane_pad(vec):
        return jnp.concatenate([vec.astype(F32), jnp.zeros((LANES - vec.shape[0],), F32)]).reshape(1, LANES)

    assert T % PROJ_TOKENS == 0 and S % ATT_TOKENS == 0 and S % GDN_CHUNK == 0 and B % GDN_SEQS == 0
    assert T % PEER_TOKENS == 0
    att_in, dn_in, z_in, ab_in = _inproj(x2, norm_mix_g.reshape(1, D_MODEL), w_all, tm=PROJ_TOKENS)

    att_n = _attention(
        att_in, att_sinks.astype(F32),
        jnp.tile(att_q_norm_g.astype(F32), ATT_HEADS).reshape(1, ATT_WIDTH),
        jnp.tile(att_k_norm_g.astype(F32), ATT_KV_HEADS).reshape(1, ATT_KV_WIDTH),
        att_out_norm_g.astype(F32).reshape(1, ATT_WIDTH),
        _block_diag_ones(ATT_WIDTH, ATT_HEAD_DIM), _block_diag_ones(ATT_KV_WIDTH, ATT_HEAD_DIM),
        seq=S, tq=ATT_TOKENS)

    tri = jnp.asarray(np.tril(np.ones((GDN_CHUNK, GDN_CHUNK), np.float32)))
    dn_o = _gdn(dn_in, ab_in, z_in, dn_conv_w.astype(F32),
                lane_pad(-jnp.exp(dn_a_log.astype(F32))), lane_pad(dn_dt_bias),
                dn_out_norm_g.astype(F32).reshape(1, DN_HEAD_DIM), tri, batch=B, seq=S, nb=GDN_SEQS)

    wqt = (peer_w_q.T.reshape(PEER_HEADS, 2, PEER_HALF, D_MODEL).transpose(1, 0, 2, 3)
           .reshape(D_MODEL, D_MODEL).astype(BF16))
    eye = jnp.eye(PEER_HEADS, dtype=peer_sub_keys.dtype)
    kd = jnp.einsum('hpkc,hg->pkhgc', peer_sub_keys, eye)
    k_km = kd.reshape(2, N_KEYS * PEER_HEADS, PEER_HEADS * PEER_HALF).astype(BF16)
    h, xn2, r1, e1, n_sel, c_gate = _outproj_topk(
        x2, att_n, dn_o, w_out.astype(BF16), norm_ffn_g.reshape(1, D_MODEL), wqt, k_km[0], k_km[1], tb=PEER_TOKENS)

    vt_slabs = (peer_v.reshape(N_EXPERTS // _SUB_E, _SUB_E, D_MODEL).transpose(0, 2, 1).astype(BF16))
    out = _peer_dense(xn2, h, r1, e1, n_sel, c_gate, peer_u.astype(BF16), vt_slabs,
                      tb=PEER_TOKENS, ec=PEER_CHUNK)
    return out.reshape(B, S, D_MODEL)


def kernel(x, norm_mix_g, w_in, att_q_norm_g, att_k_norm_g, att_sinks, att_out_norm_g, dn_conv_w, dn_a_log, dn_dt_bias, dn_out_norm_g, w_out, norm_ffn_g, peer_w_q, peer_sub_keys, peer_u, peer_v):
    h = x
    for l in range(norm_mix_g.shape[0]):
        h = _layer(h, norm_mix_g[l], w_in[l], att_q_norm_g[l], att_k_norm_g[l], att_sinks[l],
                   att_out_norm_g[l], dn_conv_w[l], dn_a_log[l], dn_dt_bias[l], dn_out_norm_g[l],
                   w_out[l], norm_ffn_g[l], peer_w_q[l], peer_sub_keys[l], peer_u[l], peer_v[l])
    return h
```

```python
import functools

import numpy as np
import jax
import jax.numpy as jnp
from jax import lax
from jax.experimental import pallas as pl
from jax.experimental.pallas import tpu as pltpu

F32 = jnp.float32
BF16 = jnp.bfloat16

D_MODEL = 1024
ATT_HEADS = 8
ATT_KV_HEADS = 2
ATT_HEAD_DIM = 64
ATT_WIDTH = 512
ATT_KV_WIDTH = 128
ATT_BLOCK = 128
DN_HEADS = 4
DN_HEAD_DIM = 128
DN_WIDTH = 512
CONV_WIDTH = 4
PEER_HEADS = 8
N_KEYS = 128
N_EXPERTS = N_KEYS * N_KEYS
PEER_HALF = 64
PEER_TOPK = 16
EPS = 1e-6

LANES = 128
SUBLANES = 8
V7X_VMEM_BYTES = 64 * 1024 * 1024
VMEM_LIMIT_BYTES = V7X_VMEM_BYTES - 8 * 1024 * 1024

PROJ_TOKENS = 512
ATT_TOKENS = 512
GDN_SEQS = 4
PEER_TOKENS = 512
PEER_CHUNK = 2048

ATT_IN_WIDTH = ATT_WIDTH + 2 * ATT_KV_WIDTH

_C_ATT = 0
_C_DN = ATT_IN_WIDTH
_C_Z = _C_DN + 3 * DN_WIDTH
_C_AB = _C_Z + DN_WIDTH
_C_END = _C_AB + 2 * LANES

GDN_CHUNK = 128
_ROWS16 = 2 * SUBLANES


def _dot(a, b):
    return jnp.dot(a, b, preferred_element_type=F32)


def _dot_nt(a, b):
    return lax.dot_general(a, b, (((1,), (1,)), ((), ())), preferred_element_type=F32)


def _split2(a):
    hi = a.astype(BF16)
    lo = (a - hi.astype(F32)).astype(BF16)
    return hi, lo


def _dot3s(a_split, b_split):
    ah, al = a_split
    bh, bl = b_split
    return _dot(ah, bh) + (_dot(ah, bl) + _dot(al, bh))


def _dot3(a, b):
    return _dot3s(_split2(a), _split2(b))


def _inproj_kernel(x_ref, g_ref, w_ref, att_ref, dn_ref, z_ref, ab_ref):
    x = x_ref[...]
    ms = jnp.mean(x * x, axis=-1, keepdims=True)
    xn = (x * lax.rsqrt(ms + EPS) * g_ref[...]).astype(BF16)
    att_ref[...] = _dot(xn, w_ref[:, _C_ATT:_C_DN])
    dn_ref[...] = _dot(xn, w_ref[:, _C_DN:_C_Z])
    z_ref[...] = _dot(xn, w_ref[:, _C_Z:_C_AB])
    ab_ref[...] = _dot(xn, w_ref[:, _C_AB:_C_END])


def _inproj(x2, g, w_all, tm):
    T = x2.shape[0]
    return pl.pallas_call(
        _inproj_kernel,
        grid=(T // tm,),
        in_specs=[pl.BlockSpec((tm, D_MODEL), lambda i: (i, 0)),
                  pl.BlockSpec((1, D_MODEL), lambda i: (0, 0)),
                  pl.BlockSpec((D_MODEL, _C_END), lambda i: (0, 0))],
        out_specs=[pl.BlockSpec((tm, ATT_IN_WIDTH), lambda i: (i, 0)),
                   pl.BlockSpec((tm, 3 * DN_WIDTH), lambda i: (i, 0)),
                   pl.BlockSpec((tm, DN_WIDTH), lambda i: (i, 0)),
                   pl.BlockSpec((tm, 2 * LANES), lambda i: (i, 0))],
        out_shape=[jax.ShapeDtypeStruct((T, ATT_IN_WIDTH), F32),
                   jax.ShapeDtypeStruct((T, 3 * DN_WIDTH), F32),
                   jax.ShapeDtypeStruct((T, DN_WIDTH), F32),
                   jax.ShapeDtypeStruct((T, 2 * LANES), F32)],
        compiler_params=pltpu.CompilerParams(dimension_semantics=("arbitrary",),
                                             vmem_limit_bytes=VMEM_LIMIT_BYTES),
        name="inproj",
    )(x2, g, w_all)


def _head_rms(t, bd, g):
    hi, lo = _split2(t * t)
    ss = _dot(hi, bd) + _dot(lo, bd)
    return t * lax.rsqrt(ss * (1.0 / ATT_HEAD_DIM) + EPS) * g


def _attn_kernel(sink_ref, cur_ref, prev_ref, gq_ref, gk_ref, go_ref, bdq_ref, bdk_ref, o_ref,
                 *, tiles_per_seq):
    i = pl.program_id(0)
    tq = cur_ref.shape[0]
    nblk = tq // ATT_BLOCK
    first = (i % tiles_per_seq) == 0

    q = cur_ref[:, 0:ATT_WIDTH]
    k = cur_ref[:, ATT_WIDTH:ATT_WIDTH + ATT_KV_WIDTH]
    v = cur_ref[:, ATT_WIDTH + ATT_KV_WIDTH:ATT_WIDTH + 2 * ATT_KV_WIDTH]
    kp = prev_ref[:, 0:ATT_KV_WIDTH]
    vp = prev_ref[:, ATT_KV_WIDTH:2 * ATT_KV_WIDTH]

    qn = (_head_rms(q, bdq_ref[...], gq_ref[...]) * (ATT_HEAD_DIM ** -0.5)).astype(BF16)
    kn = jnp.concatenate([_head_rms(kp, bdk_ref[...], gk_ref[...]),
                          _head_rms(k, bdk_ref[...], gk_ref[...])], axis=0)
    vf = jnp.concatenate([vp, v], axis=0)

    lane = lax.broadcasted_iota(jnp.int32, kn.shape, 1)
    left = lane < ATT_HEAD_DIM
    zero = jnp.zeros_like(kn)
    k0l = jnp.where(left, kn, zero)
    k1r = jnp.where(left, zero, kn)
    v0l = jnp.where(left, vf, zero)
    v1r = jnp.where(left, zero, vf)
    kl = (k0l.astype(BF16), pltpu.roll(k1r, ATT_HEAD_DIM, 1).astype(BF16))
    kr = (pltpu.roll(k0l, ATT_HEAD_DIM, 1).astype(BF16), k1r.astype(BF16))
    vl = (v0l.astype(BF16), pltpu.roll(v1r, ATT_HEAD_DIM, 1).astype(BF16))
    vr = (pltpu.roll(v0l, ATT_HEAD_DIM, 1).astype(BF16), v1r.astype(BF16))

    qi = lax.broadcasted_iota(jnp.int32, (ATT_BLOCK, 2 * ATT_BLOCK), 0)
    kj = lax.broadcasted_iota(jnp.int32, (ATT_BLOCK, 2 * ATT_BLOCK), 1)
    rel = qi + ATT_BLOCK - kj
    in_window = (rel >= 0) & (rel < ATT_BLOCK)
    first_key = jnp.where(first, ATT_BLOCK, 0)
    neg_inf = jnp.full((ATT_BLOCK, 2 * ATT_BLOCK), -jnp.inf, F32)

    def softmax_rows(s, sink):
        m = jnp.maximum(jnp.max(s, axis=-1, keepdims=True), sink)
        p = jnp.exp(s - m)
        den = jnp.sum(p, axis=-1, keepdims=True) + jnp.exp(sink - m)
        return (p * (1.0 / den)).astype(BF16)

    for j in range(nblk):
        rows = slice(j * ATT_BLOCK, (j + 1) * ATT_BLOCK)
        krows = slice(j * ATT_BLOCK, (j + 2) * ATT_BLOCK)
        if j == 0:
            mask = in_window & (kj >= first_key)
        else:
            mask = in_window
        pairs = []
        for c in range(ATT_KV_HEADS):
            qe = jnp.concatenate([qn[rows, (2 * c) * LANES:(2 * c + 1) * LANES],
                                  qn[rows, (2 * c + 1) * LANES:(2 * c + 2) * LANES]], axis=0)
            s_even = _dot_nt(qe, kl[c][krows])
            s_odd = _dot_nt(qe, kr[c][krows])
            for half in range(2):
                hr = slice(half * ATT_BLOCK, (half + 1) * ATT_BLOCK)
                h_even = 4 * c + 2 * half
                p_e = softmax_rows(jnp.where(mask, s_even[hr], neg_inf), sink_ref[h_even])
                p_o = softmax_rows(jnp.where(mask, s_odd[hr], neg_inf), sink_ref[h_even + 1])
                pairs.append(_dot(p_e, vl[c][krows]) + _dot(p_o, vr[c][krows]))
        att = jnp.concatenate(pairs, axis=1)
        ms = jnp.mean(att * att, axis=-1, keepdims=True)
        o_ref[rows, :] = (att * lax.rsqrt(ms + EPS) * go_ref[...]).astype(BF16)


def _inproj_attn_kernel(sink_ref, x_ref, xp_ref, g_ref, w_ref, gq_ref, gk_ref, go_ref, bdq_ref, bdk_ref,
                        o_ref, dn_ref, z_ref, ab_ref, cur_ref, prev_ref, *, tiles_per_seq):
    def normed(x):
        ms = jnp.mean(x * x, axis=-1, keepdims=True)
        return (x * lax.rsqrt(ms + EPS) * g_ref[...]).astype(BF16)

    xn = normed(x_ref[...])
    cur_ref[...] = _dot(xn, w_ref[:, _C_ATT:_C_DN])
    dn_ref[...] = _dot(xn, w_ref[:, _C_DN:_C_Z])
    z_ref[...] = _dot(xn, w_ref[:, _C_Z:_C_AB])
    ab_ref[...] = _dot(xn, w_ref[:, _C_AB:_C_END])
    kv0 = _C_ATT + ATT_WIDTH
    prev_ref[...] = _dot(normed(xp_ref[...]), w_ref[:, kv0:kv0 + 2 * ATT_KV_WIDTH])
    _attn_kernel(sink_ref, cur_ref, prev_ref, gq_ref, gk_ref, go_ref, bdq_ref, bdk_ref, o_ref,
                 tiles_per_seq=tiles_per_seq)


def _inproj_attention(x2, g, w_all, sinks, gq, gk, go, bdq, bdk, seq, tq):
    T = x2.shape[0]
    tiles_per_seq = seq // tq
    blk_per_tile = tq // ATT_BLOCK
    const = lambda shape: pl.BlockSpec(shape, lambda i, s: (0, 0))
    tok = lambda w: pl.BlockSpec((tq, w), lambda i, s: (i, 0))
    return pl.pallas_call(
        functools.partial(_inproj_attn_kernel, tiles_per_seq=tiles_per_seq),
        grid_spec=pltpu.PrefetchScalarGridSpec(
            num_scalar_prefetch=1,
            grid=(T // tq,),
            in_specs=[tok(D_MODEL),
                      pl.BlockSpec((ATT_BLOCK, D_MODEL),
                                   lambda i, s: (jnp.maximum(i * blk_per_tile - 1, 0), 0)),
                      const((1, D_MODEL)), const((D_MODEL, _C_END)),
                      const((1, ATT_WIDTH)), const((1, ATT_KV_WIDTH)), const((1, ATT_WIDTH)),
                      const((ATT_WIDTH, ATT_WIDTH)), const((ATT_KV_WIDTH, ATT_KV_WIDTH))],
            out_specs=[tok(ATT_WIDTH), tok(3 * DN_WIDTH), tok(DN_WIDTH), tok(2 * LANES)],
            scratch_shapes=[pltpu.VMEM((tq, ATT_IN_WIDTH), F32),
                            pltpu.VMEM((ATT_BLOCK, 2 * ATT_KV_WIDTH), F32)],
        ),
        out_shape=[jax.ShapeDtypeStruct((T, ATT_WIDTH), BF16),
                   jax.ShapeDtypeStruct((T, 3 * DN_WIDTH), F32),
                   jax.ShapeDtypeStruct((T, DN_WIDTH), F32),
                   jax.ShapeDtypeStruct((T, 2 * LANES), F32)],
        compiler_params=pltpu.CompilerParams(dimension_semantics=("arbitrary",),
                                             vmem_limit_bytes=VMEM_LIMIT_BYTES),
        name="inproj_swa_attention",
    )(sinks, x2, x2, g, w_all, gq, gk, go, bdq, bdk)


def _unit_lower_inverses(m_lows, row, col):
    n = m_lows[0].shape[0]
    zero = jnp.zeros_like(m_lows[0])
    zero16 = jnp.zeros(zero.shape, BF16)
    eye = jnp.where(row == col, jnp.ones_like(zero), zero)
    same = (row >> 3) == (col >> 3)
    m16 = [m.astype(BF16) for m in m_lows]
    nm16 = [jnp.where(same, -mh, zero16) for mh in m16]
    ps = [eye + jnp.where(same, -m, zero) for m in m_lows]
    n2_16 = [_dot(nm, nm).astype(BF16) for nm in nm16]
    ps = [p + _dot(p.astype(BF16), n2) for p, n2 in zip(ps, n2_16)]
    n4_16 = [_dot(n2, n2).astype(BF16) for n2 in n2_16]
    ps = [p + _dot(p.astype(BF16), n4) for p, n4 in zip(ps, n4_16)]
    shift = 3
    while (1 << shift) < n:
        same_next = (row >> (shift + 1)) == (col >> (shift + 1))
        lower_left = same_next & jnp.logical_not(same)
        p16 = [p.astype(BF16) for p in ps]
        ts = [_dot(ph, jnp.where(lower_left, mh, zero16)) for ph, mh in zip(p16, m16)]
        ps = [p - _dot(t.astype(BF16), ph) for p, t, ph in zip(ps, ts, p16)]
        same = same_next
        shift += 1
    return ps


def _silu(v):
    return v * (1.0 / (1.0 + jnp.exp(-v)))


def _gdn_kernel(dn_ref, ab_ref, z_ref, cw_ref, nega_ref, dtb_ref, gn_ref, tri_ref, o_ref,
                state_ref, carry_ref):
    t = pl.program_id(1)
    C = GDN_CHUNK
    nb = dn_ref.shape[0]

    @pl.when(t == 0)
    def _():
        state_ref[...] = jnp.zeros_like(state_ref)
        carry_ref[...] = jnp.zeros_like(carry_ref)

    row = lax.broadcasted_iota(jnp.int32, (C, C), 0)
    col = lax.broadcasted_iota(jnp.int32, (C, C), 1)
    causal = row >= col
    strict = row > col
    zero_cc = jnp.zeros((C, C), F32)
    cw = cw_ref[...]
    row8 = lax.broadcasted_iota(jnp.int32, (SUBLANES, 3 * DN_WIDTH), 0)

    ids, qs, ks, vbs, kbes, decs, egs, kdecs, cds = [], [], [], [], [], [], [], [], []
    for bb in range(nb):
        x = dn_ref[bb]
        tail = carry_ref[bb]
        carry_ref[bb] = x[C - SUBLANES:C, :]
        y = x * cw[CONV_WIDTH - 1:CONV_WIDTH, :]
        for back in range(1, CONV_WIDTH):
            xr = pltpu.roll(x, back, 0)
            top = jnp.where(row8 < back, pltpu.roll(tail, back, 0), xr[0:SUBLANES])
            shifted = jnp.concatenate([top, xr[SUBLANES:]], axis=0)
            y = y + shifted * cw[CONV_WIDTH - 1 - back:CONV_WIDTH - back, :]
        y = _silu(y)

        ab = ab_ref[bb]
        sp_in = ab[:, 0:LANES] + dtb_ref[...]
        softplus = jnp.maximum(sp_in, 0.0) + jnp.log1p(jnp.exp(-jnp.abs(sp_in)))
        g_all = nega_ref[...] * softplus
        beta_all = 1.0 / (1.0 + jnp.exp(-ab[:, LANES:2 * LANES]))
        gc_all = _dot3(tri_ref[...], g_all)
        gc_t = gc_all.T

        for h in range(DN_HEADS):
            qh = y[:, h * DN_HEAD_DIM:(h + 1) * DN_HEAD_DIM]
            kh = y[:, DN_WIDTH + h * DN_HEAD_DIM:DN_WIDTH + (h + 1) * DN_HEAD_DIM]
            vh = y[:, 2 * DN_WIDTH + h * DN_HEAD_DIM:2 * DN_WIDTH + (h + 1) * DN_HEAD_DIM]
            qh = qh * lax.rsqrt(jnp.sum(qh * qh, axis=-1, keepdims=True) + EPS) * (DN_HEAD_DIM ** -0.5)
            kh = kh * lax.rsqrt(jnp.sum(kh * kh, axis=-1, keepdims=True) + EPS)
            gcol = gc_all[:, h:h + 1]
            grow = gc_t[h:h + 1, :]
            bcol = beta_all[:, h:h + 1]
            glast = gc_all[C - 1:C, h:h + 1]
            eg = jnp.exp(gcol)
            kb = kh * bcol
            ids.append((bb, h))
            qs.append(qh)
            ks.append(kh.astype(BF16))
            vbs.append(jnp.concatenate([vh * bcol, kb * eg], axis=1).astype(BF16))
            kbes.append(kb.astype(BF16))
            decs.append(jnp.where(causal, jnp.exp(jnp.where(causal, gcol - grow, zero_cc)), zero_cc))
            egs.append(eg)
            kdecs.append((kh * jnp.exp(glast - gcol)).T.astype(BF16))
            cds.append(jnp.exp(glast))

    n = len(ids)
    m_lows = [jnp.where(strict, _dot_nt(kbes[i], ks[i]) * decs[i], zero_cc) for i in range(n)]
    qks = [jnp.where(causal, _dot_nt(qs[i].astype(BF16), ks[i]) * decs[i], zero_cc).astype(BF16)
           for i in range(n)]
    tinvs = _unit_lower_inverses(m_lows, row, col)
    uws = [_dot(tinvs[i].astype(BF16), vbs[i]) for i in range(n)]
    s_olds = [state_ref[bb, h] for (bb, h) in ids]
    s16s = [s.astype(BF16) for s in s_olds]
    v16s = [(uws[i][:, 0:DN_HEAD_DIM] - _dot(uws[i][:, DN_HEAD_DIM:].astype(BF16), s16s[i])).astype(BF16)
            for i in range(n)]
    outs = [_dot((qs[i] * egs[i]).astype(BF16), s16s[i]) + _dot(qks[i], v16s[i]) for i in range(n)]
    for i, (bb, h) in enumerate(ids):
        state_ref[bb, h] = s_olds[i] * cds[i] + _dot(kdecs[i], v16s[i])

    for i, (bb, h) in enumerate(ids):
        hs = slice(h * DN_HEAD_DIM, (h + 1) * DN_HEAD_DIM)
        o = outs[i]
        ms = jnp.mean(o * o, axis=-1, keepdims=True)
        o_ref[bb, :, hs] = (o * lax.rsqrt(ms + EPS) * gn_ref[...] * _silu(z_ref[bb, :, hs])).astype(BF16)


def _gdn(dn_in, ab, z, conv_w, neg_a, dt_bias, gnorm, tri, batch, seq, nb):
    C = GDN_CHUNK
    nc = seq // C
    blk3 = lambda w: pl.BlockSpec((nb, C, w), lambda bi, t: (bi, t, 0))
    const = lambda shape: pl.BlockSpec(shape, lambda bi, t: (0, 0))
    out = pl.pallas_call(
        _gdn_kernel,
        grid=(batch // nb, nc),
        in_specs=[blk3(3 * DN_WIDTH), blk3(2 * LANES), blk3(DN_WIDTH),
                  const((CONV_WIDTH, 3 * DN_WIDTH)), const((1, LANES)), const((1, LANES)),
                  const((1, DN_HEAD_DIM)), const((C, C))],
        out_specs=blk3(DN_WIDTH),
        out_shape=jax.ShapeDtypeStruct((batch, seq, DN_WIDTH), BF16),
        scratch_shapes=[pltpu.VMEM((nb, DN_HEADS, DN_HEAD_DIM, DN_HEAD_DIM), F32),
                        pltpu.VMEM((nb, SUBLANES, 3 * DN_WIDTH), F32)],
        compiler_params=pltpu.CompilerParams(dimension_semantics=("arbitrary", "arbitrary"),
                                             vmem_limit_bytes=VMEM_LIMIT_BYTES),
        name="gated_deltanet",
    )(dn_in.reshape(batch, seq, -1), ab.reshape(batch, seq, -1), z.reshape(batch, seq, -1),
      conv_w, neg_a, dt_bias, gnorm, tri)
    return out.reshape(batch * seq, DN_WIDTH)


def _outproj_kernel(x_ref, att_ref, dn_ref, wo_ref, g_ref, h_ref, xn_ref):
    h = x_ref[...] + (_dot(att_ref[...], wo_ref[0:ATT_WIDTH, :])
                      + _dot(dn_ref[...], wo_ref[ATT_WIDTH:ATT_WIDTH + DN_WIDTH, :]))
    h_ref[...] = h
    ms = jnp.mean(h * h, axis=-1, keepdims=True)
    xn_ref[...] = (h * lax.rsqrt(ms + EPS) * g_ref[...]).astype(BF16)


def _oddeven_merge(lo, hi, r):
    step = r * 2
    if step < hi - lo:
        yield from _oddeven_merge(lo, hi, step)
        yield from _oddeven_merge(lo + r, hi, step)
        for i in range(lo + r, hi - r, step):
            yield (i, i + r)
    else:
        yield (lo, lo + r)


def _oddeven_sort_net(lo, hi):
    if hi - lo >= 1:
        mid = lo + (hi - lo) // 2
        yield from _oddeven_sort_net(lo, mid)
        yield from _oddeven_sort_net(mid + 1, hi)
        yield from _oddeven_merge(lo, hi, 1)


_SORT16 = tuple(_oddeven_sort_net(0, PEER_TOPK - 1))
_BITONIC16 = tuple((i, i + d) for d in (8, 4, 2, 1) for i in range(PEER_TOPK) if (i & d) == 0)
_STAIR = tuple((r, q) for r in range(PEER_TOPK) for q in range(PEER_TOPK) if (r + 1) * (q + 1) <= PEER_TOPK)


def _ce_vi(a, b):
    va, ia = a
    vb, ib = b
    a_first = (va > vb) | ((va == vb) & (ia < ib))
    return ((jnp.where(a_first, va, vb), jnp.where(a_first, ia, ib)),
            (jnp.where(a_first, vb, va), jnp.where(a_first, ib, ia)))


def _first_vi(a, b):
    va, ia = a
    vb, ib = b
    a_first = (va > vb) | ((va == vb) & (ia < ib))
    return (jnp.where(a_first, va, vb), jnp.where(a_first, ia, ib))


def _sort16_vi(items):
    items = list(items)
    for i, j in _SORT16:
        items[i], items[j] = _ce_vi(items[i], items[j])
    return items


def _merge_top16_vi(a, b):
    n = PEER_TOPK
    items = [_first_vi(a[i], b[n - 1 - i]) for i in range(n)]
    for i, j in _BITONIC16:
        items[i], items[j] = _ce_vi(items[i], items[j])
    return items


def _sort16_v(vals):
    vals = list(vals)
    for i, j in _SORT16:
        hi = jnp.maximum(vals[i], vals[j])
        lo = jnp.minimum(vals[i], vals[j])
        vals[i], vals[j] = hi, lo
    return vals


def _merge_top16_v(a, b):
    n = PEER_TOPK
    vals = [jnp.maximum(a[i], b[n - 1 - i]) for i in range(n)]
    for i, j in _BITONIC16:
        hi = jnp.maximum(vals[i], vals[j])
        lo = jnp.minimum(vals[i], vals[j])
        vals[i], vals[j] = hi, lo
    return vals


def _top16_vi_of_128(load_key):
    groups = []
    for gi in range(N_KEYS // PEER_TOPK):
        items = [(load_key(gi * PEER_TOPK + t), jnp.full((SUBLANES, LANES), float(gi * PEER_TOPK + t), F32))
                 for t in range(PEER_TOPK)]
        groups.append(_sort16_vi(items))
    while len(groups) > 1:
        groups = [_merge_top16_vi(groups[2 * t], groups[2 * t + 1]) for t in range(len(groups) // 2)]
    return groups[0]


def _top16_v_of_128(load_key):
    groups = [_sort16_v([load_key(gi * PEER_TOPK + t) for t in range(PEER_TOPK)])
              for gi in range(N_KEYS // PEER_TOPK)]
    while len(groups) > 1:
        groups = [_merge_top16_v(groups[2 * t], groups[2 * t + 1]) for t in range(len(groups) // 2)]
    return groups[0]


def _top16_is_distinct(load_key, vals):
    zero = jnp.zeros((SUBLANES, LANES), F32)
    one = jnp.ones((SUBLANES, LANES), F32)
    n_ge = zero
    for k in range(N_KEYS):
        n_ge = n_ge + jnp.where(load_key(k) >= vals[PEER_TOPK - 1], one, zero)
    ok = n_ge == float(PEER_TOPK)
    for r in range(PEER_TOPK - 1):
        ok = ok & (vals[r] > vals[r + 1])
    return jnp.where(ok, one, zero)


def _select_pairs(av, bv):
    zero = jnp.zeros((SUBLANES, LANES), F32)
    one = jnp.ones((SUBLANES, LANES), F32)
    cand = {(r, q): av[r] + bv[q] for (r, q) in _STAIR}
    row0 = [cand[(0, q)] for q in range(PEER_TOPK)]
    rest = [cand[rq] for rq in _STAIR if rq[0] > 0]
    neg_inf = jnp.full((SUBLANES, LANES), -jnp.inf, F32)
    best = row0
    for s in range(0, len(rest), PEER_TOPK):
        grp = rest[s:s + PEER_TOPK]
        grp = grp + [neg_inf] * (PEER_TOPK - len(grp))
        best = _merge_top16_v(best, _sort16_v(grp))
    thr = best[PEER_TOPK - 1]
    zsum = one
    for jj in range(1, PEER_TOPK):
        zsum = zsum + jnp.exp(best[jj] - best[0])
    inv_z_half = 0.5 / zsum

    n_gt = zero
    for rq in _STAIR:
        n_gt = n_gt + jnp.where(cand[rq] > thr, one, zero)
    need = float(PEER_TOPK) - n_gt
    cnt = zero
    n_row = [zero] * PEER_TOPK
    for (r, q) in _STAIR:
        c = cand[(r, q)]
        eq = c == thr
        take = (c > thr) | (eq & (cnt < need))
        cnt = cnt + jnp.where(eq, one, zero)
        n_row[r] = n_row[r] + jnp.where(take, one, zero)
    return n_row, inv_z_half


def _match_count(keys, probe, n_row):
    n_k = jnp.zeros((SUBLANES, LANES), F32)
    for r in reversed(range(PEER_TOPK)):
        n_k = jnp.where(keys[r] == probe, n_row[r], n_k)
    return n_k


def _match_rank(probe, rows):
    rk = jnp.full(probe.shape, float(PEER_TOPK), F32)
    for q in reversed(range(PEER_TOPK)):
        rk = jnp.where(rows[q] == probe, jnp.full_like(rk, float(q)), rk)
    return rk


def _peer_topk_kernel(xn_ref, wqt_ref, k0_ref, k1_ref, r1_ref, e1_ref, n_ref, c_ref, s0_ref, s1_ref):
    tb = xn_ref.shape[0]
    half_w = PEER_HEADS * PEER_HALF
    qt = _dot_nt(wqt_ref[...], xn_ref[...]).astype(BF16)
    s0 = _dot(k0_ref[...], qt[0:half_w])
    s1 = _dot(k1_ref[...], qt[half_w:2 * half_w])
    for g in range(tb // LANES):
        s0_ref[g] = s0[:, g * LANES:(g + 1) * LANES]
        s1_ref[g] = s1[:, g * LANES:(g + 1) * LANES]

    sub_iota = lax.broadcasted_iota(jnp.int32, (_ROWS16, LANES), 0).astype(F32)

    def slab(gi, carry):
        ls = pl.ds(pl.multiple_of(gi * LANES, LANES), LANES)
        load0 = lambda k: s0_ref[gi, pl.ds(k * SUBLANES, SUBLANES), :]
        load1 = lambda k: s1_ref[gi, pl.ds(k * SUBLANES, SUBLANES), :]
        load1_head = lambda h, kb: s1_ref[gi, pl.ds(kb * _ROWS16 * SUBLANES + h, _ROWS16, stride=SUBLANES), :]

        av = _top16_v_of_128(load0)
        bv = _top16_v_of_128(load1)
        n_row, inv_z_half = _select_pairs(av, bv)
        distinct = _top16_is_distinct(load0, av) * _top16_is_distinct(load1, bv)
        has_tie = jnp.min(distinct) < 0.5

        def emit(keys0, probe0, keys1, probe1):
            for k in range(N_KEYS):
                rs = pl.ds(k * SUBLANES, SUBLANES)
                s0k = load0(k)
                n_ref[gi, rs, :] = _match_count(keys0, probe0(k, s0k), n_row)
                c_ref[gi, rs, :] = jnp.exp(s0k - av[0]) * inv_z_half
            for h in range(PEER_HEADS):
                rows = [jnp.broadcast_to(keys1[q][h:h + 1, :], (_ROWS16, LANES)) for q in range(PEER_TOPK)]
                b0h = jnp.broadcast_to(bv[0][h:h + 1, :], (_ROWS16, LANES))
                for kb in range(N_KEYS // _ROWS16):
                    rs = pl.ds(h * N_KEYS + kb * _ROWS16, _ROWS16)
                    tile = load1_head(h, kb)
                    r1_ref[rs, ls] = _match_rank(probe1(kb, tile), rows).astype(BF16)
                    e1_ref[rs, ls] = jnp.exp(tile - b0h).astype(BF16)

        @pl.when(jnp.logical_not(has_tie))
        def _():
            emit(av, lambda k, s0k: s0k, bv, lambda kb, tile: tile)

        @pl.when(has_tie)
        def _():
            a = _top16_vi_of_128(load0)
            b = _top16_vi_of_128(load1)
            emit([it[1] for it in a], lambda k, s0k: float(k),
                 [it[1] for it in b], lambda kb, tile: sub_iota + float(kb * _ROWS16))

        return carry

    lax.fori_loop(0, tb // LANES, slab, 0)


def _outproj_topk_kernel(x_ref, att_ref, dn_ref, wo_ref, g_ref, wqt_ref, k0_ref, k1_ref,
                         h_ref, xn_ref, r1_ref, e1_ref, n_ref, c_ref, s0_ref, s1_ref):
    _outproj_kernel(x_ref, att_ref, dn_ref, wo_ref, g_ref, h_ref, xn_ref)
    _peer_topk_kernel(xn_ref, wqt_ref, k0_ref, k1_ref, r1_ref, e1_ref, n_ref, c_ref, s0_ref, s1_ref)


def _outproj_topk(x2, att_n, dn_o, w_out, g, wqt, k0, k1, tb):
    T = x2.shape[0]
    rows = PEER_HEADS * N_KEYS
    tok_spec = lambda w: pl.BlockSpec((tb, w), lambda i: (i, 0))
    kspec = pl.BlockSpec((rows, PEER_HEADS * PEER_HALF), lambda i: (0, 0))
    ospec = pl.BlockSpec((rows, tb), lambda i: (0, i))
    slab_spec = pl.BlockSpec((tb // LANES, rows, LANES), lambda i: (i, 0, 0))
    return pl.pallas_call(
        _outproj_topk_kernel,
        grid=(T // tb,),
        in_specs=[tok_spec(D_MODEL), tok_spec(ATT_WIDTH), tok_spec(DN_WIDTH),
                  pl.BlockSpec((ATT_WIDTH + DN_WIDTH, D_MODEL), lambda i: (0, 0)),
                  pl.BlockSpec((1, D_MODEL), lambda i: (0, 0)),
                  pl.BlockSpec((D_MODEL, D_MODEL), lambda i: (0, 0)),
                  kspec, kspec],
        out_specs=[tok_spec(D_MODEL), tok_spec(D_MODEL), ospec, ospec, slab_spec, slab_spec],
        out_shape=[jax.ShapeDtypeStruct((T, D_MODEL), F32),
                   jax.ShapeDtypeStruct((T, D_MODEL), BF16),
                   jax.ShapeDtypeStruct((rows, T), BF16),
                   jax.ShapeDtypeStruct((rows, T), BF16),
                   jax.ShapeDtypeStruct((T // LANES, rows, LANES), F32),
                   jax.ShapeDtypeStruct((T // LANES, rows, LANES), F32)],
        scratch_shapes=[pltpu.VMEM((tb // LANES, rows, LANES), F32),
                        pltpu.VMEM((tb // LANES, rows, LANES), F32)],
        compiler_params=pltpu.CompilerParams(dimension_semantics=("arbitrary",),
                                             vmem_limit_bytes=VMEM_LIMIT_BYTES),
        name="outproj_peer_topk",
    )(x2, att_n, dn_o, w_out, g, wqt, k0, k1)


_SUB_E = 512
_W_TILE = 256


def _peer_dense_kernel(xn_ref, xn_ahead_ref, h_ref, r1_ref, e1_ref, n_ref, c_ref, u_ref, u_ahead_ref,
                       vt_ref, o_ref, acc_ref, hm_ref, at_ref):
    i = pl.program_id(0)
    c = pl.program_id(1)
    tb = xn_ref.shape[0]
    n_sub = vt_ref.shape[0]
    i0_per_sub = _SUB_E // N_KEYS
    xn = xn_ref[...]

    def gelu2(a):
        a = a.astype(BF16)
        return a * (1.0 + lax.erf(a * 0.7071067811865476))

    @pl.when(c == 0)
    def _():
        acc_ref[...] = jnp.zeros_like(acc_ref)

    @pl.when((c == 0) & (i == 0))
    def _():
        at_ref[0] = gelu2(_dot_nt(u_ref[0:_SUB_E, :], xn))

    tw = min(tb, _W_TILE)
    zero16 = jnp.zeros((_ROWS16, tw), BF16)
    n_j = N_KEYS // _ROWS16

    def gated_acts(sub, slot, out_ref):
        for ii in range(i0_per_sub):
            i0 = (c * n_sub + sub) * i0_per_sub + ii
            for lt in range(tb // tw):
                ls = slice(lt * tw, (lt + 1) * tw)
                wacc = [None] * n_j
                for h in range(PEER_HEADS):
                    rowi = pl.ds(i0 * PEER_HEADS + h, SUBLANES, stride=0)
                    slabs = range(lt * tw // LANES, (lt + 1) * tw // LANES)
                    n8 = jnp.concatenate([n_ref[g, rowi, :] for g in slabs], axis=1)
                    c8 = jnp.concatenate([c_ref[g, rowi, :] for g in slabs], axis=1)
                    nb = jnp.concatenate([n8, n8], axis=0).astype(BF16)
                    cb = jnp.concatenate([c8, c8], axis=0).astype(BF16)
                    for j in range(n_j):
                        rs = slice(h * N_KEYS + j * _ROWS16, h * N_KEYS + (j + 1) * _ROWS16)
                        term = jnp.where(r1_ref[rs, ls] < nb, e1_ref[rs, ls] * cb, zero16)
                        wacc[j] = term if wacc[j] is None else wacc[j] + term
                for j in range(n_j):
                    er = slice(ii * N_KEYS + j * _ROWS16, ii * N_KEYS + (j + 1) * _ROWS16)
                    out_ref[er, ls] = at_ref[slot, er, ls] * wacc[j]

    for sub in range(n_sub):
        par = sub % 2
        if sub + 1 < n_sub:
            at_ref[1 - par] = gelu2(_dot_nt(u_ref[(sub + 1) * _SUB_E:(sub + 2) * _SUB_E, :], xn))
        else:
            at_ref[1 - par] = gelu2(_dot_nt(u_ahead_ref[...], xn_ahead_ref[...]))
        if sub > 0:
            acc_ref[...] += _dot(vt_ref[sub - 1], hm_ref[1 - par])
        gated_acts(sub, par, hm_ref.at[par])
    acc_ref[...] += _dot(vt_ref[n_sub - 1], hm_ref[(n_sub - 1) % 2])

    @pl.when(c == pl.num_programs(1) - 1)
    def _():
        o_ref[...] = h_ref[...] + acc_ref[...].T


def _peer_dense(xn2, h, r1, e1, n_sel, c_gate, u16, vt_slabs, tb, ec):
    T = xn2.shape[0]
    rows = PEER_HEADS * N_KEYS
    n_sub = ec // _SUB_E
    assert n_sub >= 2
    tspec = pl.BlockSpec((rows, tb), lambda i, c: (0, i))
    slab_spec = pl.BlockSpec((tb // LANES, rows, LANES), lambda i, c: (i, 0, 0))
    n_tok_blocks = T // tb
    n_chunks = N_EXPERTS // ec
    n_slabs = N_EXPERTS // _SUB_E

    def ahead_tokens(i, c):
        return (jnp.where(c == n_chunks - 1, jnp.minimum(i + 1, n_tok_blocks - 1), i), 0)

    return pl.pallas_call(
        _peer_dense_kernel,
        grid=(n_tok_blocks, n_chunks),
        in_specs=[pl.BlockSpec((tb, D_MODEL), lambda i, c: (i, 0)),
                  pl.BlockSpec((tb, D_MODEL), ahead_tokens),
                  pl.BlockSpec((tb, D_MODEL), lambda i, c: (i, 0)),
                  tspec, tspec, slab_spec, slab_spec,
                  pl.BlockSpec((ec, D_MODEL), lambda i, c: (c, 0)),
                  pl.BlockSpec((_SUB_E, D_MODEL), lambda i, c: (((c + 1) * n_sub) % n_slabs, 0)),
                  pl.BlockSpec((n_sub, D_MODEL, _SUB_E), lambda i, c: (c, 0, 0))],
        out_specs=pl.BlockSpec((tb, D_MODEL), lambda i, c: (i, 0)),
        out_shape=jax.ShapeDtypeStruct((T, D_MODEL), F32),
        scratch_shapes=[pltpu.VMEM((D_MODEL, tb), F32),
                        pltpu.VMEM((2, _SUB_E, tb), BF16),
                        pltpu.VMEM((2, _SUB_E, tb), BF16)],
        compiler_params=pltpu.CompilerParams(dimension_semantics=("arbitrary", "arbitrary"),
                                             vmem_limit_bytes=VMEM_LIMIT_BYTES),
        name="peer_dense",
    )(xn2, xn2, h, r1, e1, n_sel, c_gate, u16, u16, vt_slabs)


def _block_diag_ones(width, blk):
    idx = np.arange(width) // blk
    return jnp.asarray((idx[:, None] == idx[None, :]).astype(np.float32), dtype=BF16)


def _layer(x, norm_mix_g, w_in, att_q_norm_g, att_k_norm_g, att_sinks, att_out_norm_g, dn_conv_w,
           dn_a_log, dn_dt_bias, dn_out_norm_g, w_out, norm_ffn_g, peer_w_q, peer_sub_keys, peer_u, peer_v):
    B, S, _ = x.shape
    T = B * S
    x2 = x.reshape(T, D_MODEL)

    o_q, o_k, o_v = 0, ATT_WIDTH, ATT_WIDTH + ATT_KV_WIDTH
    o_dn = ATT_WIDTH + 2 * ATT_KV_WIDTH
    o_a = o_dn + 3 * DN_WIDTH
    o_b = o_a + DN_HEADS
    o_z = o_b + DN_HEADS
    pad = jnp.zeros((D_MODEL, LANES - DN_HEADS), w_in.dtype)
    w_all = jnp.concatenate([w_in[:, o_q:o_dn], w_in[:, o_dn:o_a], w_in[:, o_z:o_z + DN_WIDTH],
                             w_in[:, o_a:o_b], pad, w_in[:, o_b:o_z], pad], axis=1).astype(BF16)
    del o_k, o_v

    def lane_pad(vec):
        return jnp.concatenate([vec.astype(F32), jnp.zeros((LANES - vec.shape[0],), F32)]).reshape(1, LANES)

    assert T % PROJ_TOKENS == 0 and S % ATT_TOKENS == 0 and S % GDN_CHUNK == 0 and B % GDN_SEQS == 0
    assert T % PEER_TOKENS == 0
    att_n, dn_in, z_in, ab_in = _inproj_attention(
        x2, norm_mix_g.reshape(1, D_MODEL), w_all, att_sinks.astype(F32),
        jnp.tile(att_q_norm_g.astype(F32), ATT_HEADS).reshape(1, ATT_WIDTH),
        jnp.tile(att_k_norm_g.astype(F32), ATT_KV_HEADS).reshape(1, ATT_KV_WIDTH),
        att_out_norm_g.astype(F32).reshape(1, ATT_WIDTH),
        _block_diag_ones(ATT_WIDTH, ATT_HEAD_DIM), _block_diag_ones(ATT_KV_WIDTH, ATT_HEAD_DIM),
        seq=S, tq=ATT_TOKENS)

    tri = jnp.asarray(np.tril(np.ones((GDN_CHUNK, GDN_CHUNK), np.float32)))
    dn_o = _gdn(dn_in, ab_in, z_in, dn_conv_w.astype(F32),
                lane_pad(-jnp.exp(dn_a_log.astype(F32))), lane_pad(dn_dt_bias),
                dn_out_norm_g.astype(F32).reshape(1, DN_HEAD_DIM), tri, batch=B, seq=S, nb=GDN_SEQS)

    wqt = (peer_w_q.T.reshape(PEER_HEADS, 2, PEER_HALF, D_MODEL).transpose(1, 0, 2, 3)
           .reshape(D_MODEL, D_MODEL).astype(BF16))
    eye = jnp.eye(PEER_HEADS, dtype=peer_sub_keys.dtype)
    kd = jnp.einsum('hpkc,hg->pkhgc', peer_sub_keys, eye)
    k_km = kd.reshape(2, N_KEYS * PEER_HEADS, PEER_HEADS * PEER_HALF).astype(BF16)
    h, xn2, r1, e1, n_sel, c_gate = _outproj_topk(
        x2, att_n, dn_o, w_out.astype(BF16), norm_ffn_g.reshape(1, D_MODEL), wqt, k_km[0], k_km[1], tb=PEER_TOKENS)

    vt_slabs = (peer_v.reshape(N_EXPERTS // _SUB_E, _SUB_E, D_MODEL).transpose(0, 2, 1).astype(BF16))
    out = _peer_dense(xn2, h, r1, e1, n_sel, c_gate, peer_u.astype(BF16), vt_slabs,
                      tb=PEER_TOKENS, ec=PEER_CHUNK)
    return out.reshape(B, S, D_MODEL)


def kernel(x, norm_mix_g, w_in, att_q_norm_g, att_k_norm_g, att_sinks, att_out_norm_g, dn_conv_w, dn_a_log, dn_dt_bias, dn_out_norm_g, w_out, norm_ffn_g, peer_w_q, peer_sub_keys, peer_u, peer_v):
    h = x
    for l in range(norm_mix_g.shape[0]):
        h = _layer(h, norm_mix_g[l], w_in[l], att_q_norm_g[l], att_k_norm_g[l], att_sinks[l],
                   att_out_norm_g[l], dn_conv_w[l], dn_a_log[l], dn_dt_bias[l], dn_out_norm_g[l],
                   w_out[l], norm_ffn_g[l], peer_w_q[l], peer_sub_keys[l], peer_u[l], peer_v[l])
    return h
```
